```python
import math
import jax, jax.numpy as jnp
from jax import lax
import numpy as np

D_MODEL = 1024
BATCH = 16
SEQ = 256
DEPTH = 2
DEC_BATCH = 4
DEC_SEQ = 4096
PAST_LEN = 256

GRID_W = 64
N_HEADS = 16
HEAD_DIM = D_MODEL // N_HEADS
WIN_H = 8
WIN_W = 16
Q_BLOCK = 128
SSD_EXPAND = 2
D_INNER = SSD_EXPAND * D_MODEL
SSD_HEADDIM = 64
SSD_HEADS = D_INNER // SSD_HEADDIM
N_GROUPS = 4
D_STATE = 128
D_CONV = 5
SSD_CHUNK = 128
CONV_DIM = D_INNER + 2 * N_GROUPS * D_STATE
SSD_IN_DIM = D_INNER + CONV_DIM + 2 * SSD_HEADS
N_EXPERTS = 16
CAPACITY_FACTOR = 2
D_EXPERT = 2 * D_MODEL
N_MOD = 6
RMS_EPS = 1e-6

kernel_name = "hybrid_na_ssd_ec_diffusion_step"


def rmsnorm(x, w):
    x32 = x.astype(jnp.float32)
    x32 = x32 * lax.rsqrt(jnp.mean(x32 * x32, axis=-1, keepdims=True) + RMS_EPS)
    return x32.astype(x.dtype) * w


def adaln(cvec, w, b):
    return jnp.split(jax.nn.silu(cvec) @ w + b, N_MOD, axis=-1)


def modulate(h, shift, scale):
    return h * (1 + scale) + shift


def split_heads(qkv):
    b_, t, _ = qkv.shape
    q, k, v = jnp.split(qkv, 3, axis=-1)
    return (q.reshape(b_, t, N_HEADS, HEAD_DIM), k.reshape(b_, t, N_HEADS, HEAD_DIM),
            v.reshape(b_, t, N_HEADS, HEAD_DIM))


def ctx_attention(q, k, v):
    b_, l, h, dh = q.shape
    nb = l // Q_BLOCK
    qb = (q * dh ** -0.5).reshape(b_, nb, Q_BLOCK, h, dh).swapaxes(0, 1)

    def blk(qi):
        s = jnp.einsum('bqhd,bkhd->bhqk', qi, k).astype(jnp.float32)
        p = jax.nn.softmax(s, axis=-1).astype(v.dtype)
        return jnp.einsum('bhqk,bkhd->bqhd', p, v)

    o = lax.map(blk, qb)
    return o.swapaxes(0, 1).reshape(b_, l, h * dh)


def na_latent(q, k, v, k_ctx, v_ctx, rpb):
    b_, t, h, dh = q.shape
    rows = t // GRID_W
    kh = min(WIN_H, rows)
    kw = WIN_W
    qg = (q * dh ** -0.5).reshape(b_, rows, GRID_W, h, dh)
    kg = k.reshape(b_, rows, GRID_W, h, dh)
    vg = v.reshape(b_, rows, GRID_W, h, dh)
    col = jnp.arange(GRID_W)
    col_start = jnp.clip(col - kw // 2, 0, GRID_W - kw)
    key_cols = col_start[:, None] + jnp.arange(kw)[None, :]
    dc = key_cols - col[:, None] + (WIN_W - 1)
    n_win = kh * kw

    def row_fn(r):
        r0 = jnp.clip(r - kh // 2, 0, rows - kh)
        k_rows = lax.dynamic_slice_in_dim(kg, r0, kh, axis=1)
        v_rows = lax.dynamic_slice_in_dim(vg, r0, kh, axis=1)
        k_win = k_rows[:, :, key_cols]
        v_win = v_rows[:, :, key_cols]
        q_r = lax.dynamic_index_in_dim(qg, r, axis=1, keepdims=False)
        dr = r0 + jnp.arange(kh) - r + (WIN_H - 1)
        bias = rpb[:, dr[None, :, None], dc[:, None, :]]
        s_win = jnp.einsum('bwhd,biwjhd->bhwij', q_r, k_win) + bias[None]
        s_ctx = jnp.einsum('bwhd,blhd->bhwl', q_r, k_ctx)
        logits = jnp.concatenate([s_win.reshape(b_, h, GRID_W, n_win), s_ctx], axis=-1)
        p = jax.nn.softmax(logits.astype(jnp.float32), axis=-1).astype(v.dtype)
        p_win = p[..., :n_win].reshape(b_, h, GRID_W, kh, kw)
        p_ctx = p[..., n_win:]
        return (jnp.einsum('bhwij,biwjhd->bwhd', p_win, v_win)
                + jnp.einsum('bhwl,blhd->bwhd', p_ctx, v_ctx))

    out = lax.map(row_fn, jnp.arange(rows))
    return jnp.moveaxis(out, 0, 1).reshape(b_, t, h * dh)


def depthwise_conv_centred(x, w, b):
    c_ = x.shape[-1]
    y = lax.conv_general_dilated(x, w[:, None, :].astype(x.dtype), window_strides=(1,),
                                 padding=[(D_CONV // 2, D_CONV // 2)],
                                 dimension_numbers=('NWC', 'WIO', 'NWC'),
                                 feature_group_count=c_)
    return y + b


def ssd_scan(x, dt, a, bm, cm, s0):
    b_, t, h, p = x.shape
    g, n = bm.shape[2], bm.shape[3]
    e = h // g
    q = SSD_CHUNK
    nc = t // q
    xc = x.reshape(b_, nc, q, g, e, p)
    dtc = dt.reshape(b_, nc, q, g, e)
    bc = bm.reshape(b_, nc, q, g, n)
    cc = cm.reshape(b_, nc, q, g, n)
    acs = jnp.cumsum(dtc * a.reshape(g, e), axis=2)
    seg = acs[:, :, :, None] - acs[:, :, None]
    mask = jnp.tril(jnp.ones((q, q), dtype=bool))
    lmat = jnp.exp(jnp.where(mask[:, :, None, None], seg, -jnp.inf))
    xdt = xc * dtc[..., None]
    cb = jnp.einsum('bclgn,bcsgn->bclsg', cc, bc)
    y_diag = jnp.einsum('bclsg,bclsge,bcsgep->bclgep', cb, lmat, xdt)
    decay_last = jnp.exp(acs[:, :, -1:] - acs)
    chunk_states = jnp.einsum('bclgn,bclge,bclgep->bcgepn', bc, decay_last, xdt)
    chunk_decay = jnp.exp(acs[:, :, -1])

    def step(s, inp):
        st, dcy = inp
        return s * dcy[..., None, None] + st, s

    s_init = s0.astype(jnp.float32).reshape(b_, g, e, p, n)
    s_fin, s_ins = lax.scan(step, s_init, (jnp.moveaxis(chunk_states, 1, 0),
                                           jnp.moveaxis(chunk_decay, 1, 0)))
    s_ins = jnp.moveaxis(s_ins, 0, 1)
    y_off = jnp.einsum('bclgn,bcgepn,bclge->bclgep', cc, s_ins, jnp.exp(acs))
    y = (y_diag + y_off).reshape(b_, t, h, p)
    return y, s_fin.reshape(b_, h, p, n)


def ssd_mixer(hin, in_w, conv_w, conv_b, dt_bias, a_log, d_skip, norm_w, out_w, s0_f, s0_b):
    b_, t, _ = hin.shape
    zxbcdt = hin @ in_w
    z = zxbcdt[..., :D_INNER]
    xbc = zxbcdt[..., D_INNER:D_INNER + CONV_DIM]
    dt_raw = zxbcdt[..., D_INNER + CONV_DIM:]
    xbc = jax.nn.silu(depthwise_conv_centred(xbc, conv_w, conv_b))
    gn = N_GROUPS * D_STATE
    x = xbc[..., :D_INNER].reshape(b_, t, SSD_HEADS, SSD_HEADDIM)
    bm = xbc[..., D_INNER:D_INNER + gn].reshape(b_, t, N_GROUPS, D_STATE)
    cm = xbc[..., D_INNER + gn:].reshape(b_, t, N_GROUPS, D_STATE)
    dt = jax.nn.softplus(dt_raw.astype(jnp.float32).reshape(b_, t, 2, SSD_HEADS)
                         + dt_bias.astype(jnp.float32))
    a = -jnp.exp(a_log.astype(jnp.float32))
    y_f, s_f = ssd_scan(x, dt[:, :, 0], a[0], bm, cm, s0_f)
    y_b, s_b = ssd_scan(jnp.flip(x, 1), jnp.flip(dt[:, :, 1], 1), a[1],
                        jnp.flip(bm, 1), jnp.flip(cm, 1), s0_b)
    y = (y_f + jnp.flip(y_b, 1)).astype(hin.dtype) + d_skip[:, None] * x
    y = rmsnorm(y.reshape(b_, t, D_INNER) * jax.nn.silu(z), norm_w)
    return y @ out_w, s_f, s_b


def expert_choice_ffn(hin, w_router, w_gate, w_up, w_down):
    b_, t, d = hin.shape
    cap = CAPACITY_FACTOR * t // N_EXPERTS
    aff = jax.nn.softmax((hin @ w_router).astype(jnp.float32), axis=-1)
    gate, idx = lax.top_k(jnp.swapaxes(aff, 1, 2), cap)
    xs = jax.vmap(lambda hb, ib: hb[ib])(hin, idx)
    hid = (jax.nn.silu(jnp.einsum('becd,edf->becf', xs, w_gate))
           * jnp.einsum('becd,edf->becf', xs, w_up))
    ys = jnp.einsum('becf,efd->becd', hid, w_down) * gate[..., None].astype(hin.dtype)
    return jax.vmap(lambda ib, yb: jnp.zeros((t, d), yb.dtype).at[ib.reshape(-1)].add(
        yb.reshape(-1, d)))(idx, ys)


def setup_inputs(seed: int = 0) -> dict:
    key = jax.random.key(seed)
    ks = iter(jax.random.split(key, 40))
    n_na = (DEPTH + 1) // 2
    n_ssd = DEPTH // 2
    f32 = jnp.float32

    def nrm(shape, scale):
        return jax.random.normal(next(ks), shape, f32) * scale

    dt0 = jnp.exp(jax.random.uniform(next(ks), (n_ssd, 2, SSD_HEADS), f32)
                  * (math.log(0.1) - math.log(0.001)) + math.log(0.001))
    return {
        "x_prompt": nrm((BATCH, SEQ, D_MODEL), 1.0),
        "x_sample": nrm((DEC_BATCH, DEC_SEQ, D_MODEL), 1.0),
        "cache_k_na": nrm((DEC_BATCH, n_na, PAST_LEN, N_HEADS, HEAD_DIM), 1.0),
        "cache_v_na": nrm((DEC_BATCH, n_na, PAST_LEN, N_HEADS, HEAD_DIM), 1.0),
        "state_ssd_fwd": nrm((DEC_BATCH, n_ssd, SSD_HEADS, SSD_HEADDIM, D_STATE), 0.1),
        "state_ssd_bwd": nrm((DEC_BATCH, n_ssd, SSD_HEADS, SSD_HEADDIM, D_STATE), 0.1),
        "c": nrm((DEC_BATCH, D_MODEL), 1.0),
        "c_ctx": nrm((D_MODEL,), 1.0),
        "ada_w": nrm((DEPTH, D_MODEL, N_MOD * D_MODEL), D_MODEL ** -0.5),
        "ada_b": nrm((DEPTH, N_MOD * D_MODEL), 0.02),
        "norm1_w": 1.0 + nrm((DEPTH, D_MODEL), 0.02),
        "norm2_w": 1.0 + nrm((DEPTH, D_MODEL), 0.02),
        "final_norm_w": 1.0 + nrm((D_MODEL,), 0.02),
        "na_qkv_w": nrm((n_na, D_MODEL, 3 * D_MODEL), D_MODEL ** -0.5),
        "na_out_w": nrm((n_na, D_MODEL, D_MODEL), D_MODEL ** -0.5),
        "na_rpb": nrm((n_na, N_HEADS, 2 * WIN_H - 1, 2 * WIN_W - 1), 0.2),
        "ssd_in_w": nrm((n_ssd, D_MODEL, SSD_IN_DIM), D_MODEL ** -0.5),
        "ssd_conv_w": nrm((n_ssd, D_CONV, CONV_DIM), D_CONV ** -0.5),
        "ssd_conv_b": nrm((n_ssd, CONV_DIM), 0.02),
        "ssd_dt_bias": dt0 + jnp.log(-jnp.expm1(-dt0)),
        "ssd_a_log": jnp.log(jax.random.uniform(next(ks), (n_ssd, 2, SSD_HEADS), f32, 1.0, 16.0)),
        "ssd_d": 1.0 + nrm((n_ssd, SSD_HEADS), 0.1),
        "ssd_norm_w": 1.0 + nrm((n_ssd, D_INNER), 0.02),
        "ssd_out_w": nrm((n_ssd, D_INNER, D_MODEL), D_INNER ** -0.5),
        "router_w": nrm((DEPTH, D_MODEL, N_EXPERTS), D_MODEL ** -0.5),
        "moe_w_gate": nrm((DEPTH, N_EXPERTS, D_MODEL, D_EXPERT), D_MODEL ** -0.5),
        "moe_w_up": nrm((DEPTH, N_EXPERTS, D_MODEL, D_EXPERT), D_MODEL ** -0.5),
        "moe_w_down": nrm((DEPTH, N_EXPERTS, D_EXPERT, D_MODEL), D_EXPERT ** -0.5),
    }


def reference(x_prompt, x_sample, cache_k_na, cache_v_na, state_ssd_fwd, state_ssd_bwd, c, c_ctx,
              ada_w, ada_b, norm1_w, norm2_w, final_norm_w, na_qkv_w, na_out_w, na_rpb,
              ssd_in_w, ssd_conv_w, ssd_conv_b, ssd_dt_bias, ssd_a_log, ssd_d, ssd_norm_w, ssd_out_w,
              router_w, moe_w_gate, moe_w_up, moe_w_down):
    xp, xs = x_prompt, x_sample
    bp = xp.shape[0]
    new_k, new_v, new_sf, new_sb = [], [], [], []
    for layer in range(DEPTH):
        j = layer // 2
        m_ctx = [m[None, None] for m in adaln(c_ctx, ada_w[layer], ada_b[layer])]
        m_lat = [m[:, None] for m in adaln(c, ada_w[layer], ada_b[layer])]
        hp = modulate(rmsnorm(xp, norm1_w[layer]), m_ctx[0], m_ctx[1])
        hs = modulate(rmsnorm(xs, norm1_w[layer]), m_lat[0], m_lat[1])
        if layer % 2 == 0:
            qp, kp, vp = split_heads(hp @ na_qkv_w[j])
            mix_p = ctx_attention(qp, kp, vp) @ na_out_w[j]
            qs, ks_, vs = split_heads(hs @ na_qkv_w[j])
            mix_s = na_latent(qs, ks_, vs, cache_k_na[:, j], cache_v_na[:, j], na_rpb[j]) @ na_out_w[j]
            new_k.append(kp)
            new_v.append(vp)
        else:
            prm = (ssd_in_w[j], ssd_conv_w[j], ssd_conv_b[j], ssd_dt_bias[j], ssd_a_log[j],
                   ssd_d[j], ssd_norm_w[j], ssd_out_w[j])
            zeros = jnp.zeros((bp, SSD_HEADS, SSD_HEADDIM, D_STATE), jnp.float32)
            mix_p, s_f, s_b = ssd_mixer(hp, *prm, zeros, zeros)
            mix_s, _, _ = ssd_mixer(hs, *prm, state_ssd_fwd[:, j], state_ssd_bwd[:, j])
            new_sf.append(s_f.astype(xp.dtype))
            new_sb.append(s_b.astype(xp.dtype))
        xp = xp + m_ctx[2] * mix_p
        xs = xs + m_lat[2] * mix_s
        hp = modulate(rmsnorm(xp, norm2_w[layer]), m_ctx[3], m_ctx[4])
        hs = modulate(rmsnorm(xs, norm2_w[layer]), m_lat[3], m_lat[4])
        xp = xp + m_ctx[5] * expert_choice_ffn(hp, router_w[layer], moe_w_gate[layer],
                                               moe_w_up[layer], moe_w_down[layer])
        xs = xs + m_lat[5] * expert_choice_ffn(hs, router_w[layer], moe_w_gate[layer],
                                               moe_w_up[layer], moe_w_down[layer])
    y_prompt = rmsnorm(xp, final_norm_w)
    y_sample = rmsnorm(xs, final_norm_w)
    new_cache_k_na = jnp.stack(new_k, axis=1)
    new_cache_v_na = jnp.stack(new_v, axis=1)
    new_state_ssd_fwd = jnp.stack(new_sf, axis=1)
    new_state_ssd_bwd = jnp.stack(new_sb, axis=1)
    return (y_prompt, y_sample, new_cache_k_na, new_cache_v_na, new_state_ssd_fwd, new_state_ssd_bwd)
```

```python
import functools

import jax
import jax.numpy as jnp
from jax import lax
from jax.experimental import pallas as pl
from jax.experimental.pallas import tpu as pltpu

N_HEADS = 16
GRID_W = 64
WIN_H = 8
WIN_W = 16
SSD_HEADDIM = 64
N_GROUPS = 4
D_STATE = 128
D_CONV = 5
SSD_CHUNK = 128
CAPACITY_FACTOR = 2
N_MOD = 6
RMS_EPS = 1e-6

LANES = 128
SUBLANES = 8
MOD_ROWS = 8
VMEM_LIMIT = 56 * 1024 * 1024
NEG_INF = -1e30

F32 = jnp.float32
BF16 = jnp.bfloat16
HIGHEST = lax.Precision.HIGHEST
NT_DIMS = (((1,), (1,)), ((), ()))


def _params(*sem):
    return pltpu.CompilerParams(dimension_semantics=sem, vmem_limit_bytes=VMEM_LIMIT)


def _silu(x):
    return x / (1.0 + jnp.exp(-x))


def _rms(x, w):
    ms = jnp.mean(x * x, axis=-1, keepdims=True)
    return x * lax.rsqrt(ms + RMS_EPS) * w


def _ada_kernel(c_ref, w_ref, b_ref, o_ref):
    o_ref[0] = jnp.dot(_silu(c_ref[...]), w_ref[0], preferred_element_type=F32,
                       precision=HIGHEST) + b_ref[0]


def ada_modulation(cvec, ada_w, ada_b):
    n_layers, d, n = ada_w.shape
    tn = n // 4
    return pl.pallas_call(
        _ada_kernel,
        out_shape=jax.ShapeDtypeStruct((n_layers, MOD_ROWS, n), F32),
        grid=(n_layers, n // tn),
        in_specs=[pl.BlockSpec((MOD_ROWS, d), lambda l, j: (0, 0)),
                  pl.BlockSpec((1, d, tn), lambda l, j: (l, 0, j)),
                  pl.BlockSpec((1, 1, tn), lambda l, j: (l, 0, j))],
        out_specs=pl.BlockSpec((1, MOD_ROWS, tn), lambda l, j: (l, 0, j)),
        compiler_params=_params("arbitrary", "arbitrary"),
        name="ada_modulation",
    )(cvec, ada_w, ada_b.reshape(n_layers, 1, n))


def _prenorm_kernel(x_ref, nw_ref, sh_ref, sc_ref, h_ref):
    h = _rms(x_ref[...], nw_ref[...]) * (1.0 + sc_ref[0]) + sh_ref[0]
    h_ref[...] = h.astype(h_ref.dtype)


def prenorm(x, nw, shift, scale, tg, out_dtype, tm=512):
    m, d = x.shape
    per = tg // tm
    vec = pl.BlockSpec((1, 1, d), lambda i: (i // per, 0, 0))
    return pl.pallas_call(
        _prenorm_kernel,
        out_shape=jax.ShapeDtypeStruct((m, d), out_dtype),
        grid=(m // tm,),
        in_specs=[pl.BlockSpec((tm, d), lambda i: (i, 0)),
                  pl.BlockSpec((1, d), lambda i: (0, 0)), vec, vec],
        out_specs=pl.BlockSpec((tm, d), lambda i: (i, 0)),
        compiler_params=_params("arbitrary"),
        name="prenorm",
    )(x, nw.reshape(1, d), shift, scale)


def _mm_kernel(x_ref, w_ref, o_ref, wb_ref):
    @pl.when(pl.program_id(1) == 0)
    def _():
        wb_ref[...] = w_ref[...].astype(BF16)

    o_ref[...] = jnp.dot(x_ref[...], wb_ref[...], preferred_element_type=F32).astype(o_ref.dtype)


def matmul(x, w, out_dtype, tm=1024, tn=1024):
    m, k = x.shape
    n = w.shape[1]
    tn = max(t for t in range(LANES, min(tn, n) + 1, LANES) if n % t == 0)
    tm = min(tm, m)
    return pl.pallas_call(
        _mm_kernel,
        out_shape=jax.ShapeDtypeStruct((m, n), out_dtype),
        grid=(n // tn, m // tm),
        in_specs=[pl.BlockSpec((tm, k), lambda j, i: (i, 0)),
                  pl.BlockSpec((k, tn), lambda j, i: (0, j))],
        out_specs=pl.BlockSpec((tm, tn), lambda j, i: (i, j)),
        scratch_shapes=[pltpu.VMEM((k, tn), BF16)],
        compiler_params=_params("arbitrary", "arbitrary"),
        name="matmul",
    )(x, w)


def _softmax_pv(s_list, v_list):
    m = s_list[0].max(axis=-1, keepdims=True)
    for s in s_list[1:]:
        m = jnp.maximum(m, s.max(axis=-1, keepdims=True))
    den = None
    acc = None
    for s, v in zip(s_list, v_list):
        p = jnp.exp(s - m)
        ps = p.sum(axis=-1, keepdims=True)
        pv = jnp.dot(p.astype(BF16), v, preferred_element_type=F32)
        den = ps if den is None else den + ps
        acc = pv if acc is None else acc + pv
    return acc / den


def _ctx_attn_kernel(q_ref, k_ref, v_ref, o_ref, *, n_heads):
    dh = q_ref.shape[1] // n_heads
    scale = dh ** -0.5
    for h in range(n_heads):
        sl = slice(h * dh, (h + 1) * dh)
        q = q_ref[:, sl].astype(BF16)
        k = k_ref[:, sl].astype(BF16)
        v = v_ref[:, sl].astype(BF16)
        s = lax.dot_general(q, k, NT_DIMS, preferred_element_type=F32) * scale
        o_ref[:, sl] = _softmax_pv([s], [v]).astype(o_ref.dtype)


def ctx_attention(qkv, seq):
    m, d3 = qkv.shape
    d = d3 // 3
    return pl.pallas_call(
        functools.partial(_ctx_attn_kernel, n_heads=N_HEADS),
        out_shape=jax.ShapeDtypeStruct((m, d), BF16),
        grid=(m // seq,),
        in_specs=[pl.BlockSpec((seq, d), lambda r: (r, 0)),
                  pl.BlockSpec((seq, d), lambda r: (r, 1)),
                  pl.BlockSpec((seq, d), lambda r: (r, 2))],
        out_specs=pl.BlockSpec((seq, d), lambda r: (r, 0)),
        compiler_params=_params("arbitrary"),
        name="ctx_attention",
    )(qkv, qkv, qkv)


def _na_row_start(r, rows, kh):
    return jnp.clip(r - kh // 2, 0, rows - kh)


def na_bias_table(rpb, rows):
    w, kw = GRID_W, WIN_W
    kh = min(WIN_H, rows)
    col = jnp.arange(w)
    cs = jnp.clip(col - kw // 2, 0, w - kw)
    dc = col[None, :] - col[:, None] + (WIN_W - 1)
    valid = (col[None, :] >= cs[:, None]) & (col[None, :] < cs[:, None] + kw)
    c_full = rpb[:, :, jnp.clip(dc, 0, 2 * WIN_W - 2)]
    c_full = jnp.where(valid[None, None], c_full, NEG_INF)
    n_off = WIN_H
    return jnp.stack([jnp.concatenate([c_full[:, off + j] for j in range(kh)], axis=-1)
                      for off in range(n_off)])


def _na_kernel(q_ref, k_ref, v_ref, kc_ref, vc_ref, b_ref, o_ref, *, n_heads, rows, kh, w):
    dh = q_ref.shape[2] // n_heads
    scale = dh ** -0.5
    r0 = _na_row_start(pl.program_id(1), rows, kh)
    start = pl.multiple_of(r0 * w, w)
    for h in range(n_heads):
        sl = slice(h * dh, (h + 1) * dh)
        q = q_ref[0, :, sl]
        kwin = k_ref[0, pl.ds(start, kh * w), sl]
        vwin = v_ref[0, pl.ds(start, kh * w), sl]
        s_win = lax.dot_general(q, kwin, NT_DIMS, preferred_element_type=F32) * scale + b_ref[0, h]
        s_ctx = lax.dot_general(q, kc_ref[0, :, sl], NT_DIMS, preferred_element_type=F32) * scale
        o = _softmax_pv([s_win, s_ctx], [vwin, vc_ref[0, :, sl]])
        o_ref[0, :, sl] = o.astype(o_ref.dtype)


def na_attention(qkv, k_ctx, v_ctx, bias):
    b, t, d3 = qkv.shape
    d = d3 // 3
    w = GRID_W
    rows = t // w
    kh = min(WIN_H, rows)
    n_ctx = k_ctx.shape[1]

    def bias_map(bi, r):
        return (_na_row_start(r, rows, kh) - r + WIN_H - 1, 0, 0, 0)

    return pl.pallas_call(
        functools.partial(_na_kernel, n_heads=N_HEADS, rows=rows, kh=kh, w=w),
        out_shape=jax.ShapeDtypeStruct((b, t, d), BF16),
        grid=(b, rows),
        in_specs=[pl.BlockSpec((1, w, d), lambda bi, r: (bi, r, 0)),
                  pl.BlockSpec((1, t, d), lambda bi, r: (bi, 0, 1)),
                  pl.BlockSpec((1, t, d), lambda bi, r: (bi, 0, 2)),
                  pl.BlockSpec((1, n_ctx, d), lambda bi, r: (bi, 0, 0)),
                  pl.BlockSpec((1, n_ctx, d), lambda bi, r: (bi, 0, 0)),
                  pl.BlockSpec((1, N_HEADS, w, kh * w), bias_map)],
        out_specs=pl.BlockSpec((1, w, d), lambda bi, r: (bi, r, 0)),
        compiler_params=_params("arbitrary", "arbitrary"),
        name="na_attention",
    )(qkv, qkv, qkv, k_ctx, v_ctx, bias)


def _residual_router_tail(mix, x_ref, g_ref, nw_ref, sh_ref, sc_ref, rw_ref, x1_ref, h_ref, lg_ref):
    x1 = x_ref[...] + g_ref[0] * mix
    x1_ref[...] = x1
    h = _rms(x1, nw_ref[...]) * (1.0 + sc_ref[0]) + sh_ref[0]
    h_ref[...] = h
    lg_ref[...] = lax.dot_general(rw_ref[...], h, NT_DIMS, preferred_element_type=F32,
                                  precision=HIGHEST)


def _attn_out_kernel(a_ref, w_ref, x_ref, g_ref, nw_ref, sh_ref, sc_ref, rw_ref,
                     x1_ref, h_ref, lg_ref, wb_ref):
    @pl.when(pl.program_id(0) == 0)
    def _():
        wb_ref[...] = w_ref[...].astype(BF16)

    mix = jnp.dot(a_ref[...], wb_ref[...], preferred_element_type=F32)
    _residual_router_tail(mix, x_ref, g_ref, nw_ref, sh_ref, sc_ref, rw_ref, x1_ref, h_ref, lg_ref)


def _ssd_out_kernel(yf_ref, yb_ref, xc_ref, z_ref, dsk_ref, snw_ref, w_ref, x_ref, g_ref, nw_ref,
                    sh_ref, sc_ref, rw_ref, x1_ref, h_ref, lg_ref, wb_ref):
    @pl.when(pl.program_id(0) == 0)
    def _():
        wb_ref[...] = w_ref[...].astype(BF16)

    y = (yf_ref[0].astype(F32) + yb_ref[0].astype(F32)) + dsk_ref[...] * xc_ref[...].astype(F32)
    y = _rms(y * _silu(z_ref[...].astype(F32)), snw_ref[...])
    mix = jnp.dot(y.astype(BF16), wb_ref[...], preferred_element_type=F32)
    _residual_router_tail(mix, x_ref, g_ref, nw_ref, sh_ref, sc_ref, rw_ref, x1_ref, h_ref, lg_ref)


def _tail_specs(d, n_exp, tm, per):
    vec = pl.BlockSpec((1, 1, d), lambda i: (i // per, 0, 0))
    row = pl.BlockSpec((tm, d), lambda i: (i, 0))
    in_specs = [row, vec, pl.BlockSpec((1, d), lambda i: (0, 0)), vec, vec,
                pl.BlockSpec((n_exp, d), lambda i: (0, 0))]
    out_specs = [row, row, pl.BlockSpec((n_exp, tm), lambda i: (0, i))]
    return in_specs, out_specs


def _tail_out_shape(m, d, n_exp):
    return [jax.ShapeDtypeStruct((m, d), F32), jax.ShapeDtypeStruct((m, d), F32),
            jax.ShapeDtypeStruct((n_exp, m), F32)]


def attn_out(a, w, x, gate, nw, shift, scale, router_w, tg, tm=512):
    m, k = a.shape
    d = w.shape[1]
    n_exp = router_w.shape[1]
    tail_in, tail_out = _tail_specs(d, n_exp, tm, tg // tm)
    return pl.pallas_call(
        _attn_out_kernel,
        out_shape=_tail_out_shape(m, d, n_exp),
        grid=(m // tm,),
        in_specs=[pl.BlockSpec((tm, k), lambda i: (i, 0)),
                  pl.BlockSpec((k, d), lambda i: (0, 0))] + tail_in,
        out_specs=tail_out,
        scratch_shapes=[pltpu.VMEM((k, d), BF16)],
        compiler_params=_params("arbitrary"),
        name="attn_out",
    )(a, w, x, gate, nw.reshape(1, d), shift, scale, router_w.T)


def ssd_out(y2, xconv, z, d_cols, snw, w, x, gate, nw, shift, scale, router_w, tg, tm=512):
    k, d = w.shape
    m = x.shape[0]
    n_exp = router_w.shape[1]
    tail_in, tail_out = _tail_specs(d, n_exp, tm, tg // tm)
    return pl.pallas_call(
        _ssd_out_kernel,
        out_shape=_tail_out_shape(m, d, n_exp),
        grid=(m // tm,),
        in_specs=[pl.BlockSpec((1, tm, k), lambda i: (0, i, 0)),
                  pl.BlockSpec((1, tm, k), lambda i: (1, i, 0)),
                  pl.BlockSpec((tm, k), lambda i: (i, 0)),
                  pl.BlockSpec((tm, k), lambda i: (i, 0)),
                  pl.BlockSpec((1, k), lambda i: (0, 0)),
                  pl.BlockSpec((1, k), lambda i: (0, 0)),
                  pl.BlockSpec((k, d), lambda i: (0, 0))] + tail_in,
        out_specs=tail_out,
        scratch_shapes=[pltpu.VMEM((k, d), BF16)],
        compiler_params=_params("arbitrary"),
        name="ssd_out",
    )(y2, y2, xconv, z, d_cols.reshape(1, k), snw.reshape(1, k), w, x, gate, nw.reshape(1, d),
      shift, scale, router_w.T)


def _excl_prefix_lanes(m01):
    e, t = m01.shape
    r = lax.broadcasted_iota(jnp.int32, (LANES, LANES), 0)
    c = lax.broadcasted_iota(jnp.int32, (LANES, LANES), 1)
    upper = jnp.where(r < c, 1.0, 0.0).astype(BF16)
    outs = []
    carry = jnp.zeros((e, 1), F32)
    for j in range(t // LANES):
        blk = m01[:, j * LANES:(j + 1) * LANES]
        outs.append(jnp.dot(blk.astype(BF16), upper, preferred_element_type=F32) + carry)
        carry = carry + blk.sum(axis=1, keepdims=True)
    return jnp.concatenate(outs, axis=1)


def _route_kernel(lg_ref, idx_ref, gate_ref, aff_ref, pos_ref, *, cap, tchunk):
    n_exp, t = lg_ref.shape
    lg = lg_ref[...]
    ex = jnp.exp(lg - lg.max(axis=0, keepdims=True))
    aff = ex / ex.sum(axis=0, keepdims=True)
    bits = pltpu.bitcast(aff, jnp.int32)

    def search(i, cur):
        cand = cur | jnp.left_shift(jnp.int32(1), 30 - i)
        cnt = jnp.where(bits >= cand, 1.0, 0.0).sum(axis=1, keepdims=True)
        return jnp.where(cnt >= cap, cand, cur)

    thr = lax.fori_loop(0, 31, search, jnp.zeros((n_exp, 1), jnp.int32))
    gt = bits > thr
    eq = bits == thr
    need = cap - jnp.where(gt, 1.0, 0.0).sum(axis=1, keepdims=True)
    eq_rank = _excl_prefix_lanes(jnp.where(eq, 1.0, 0.0))
    sel = gt | (eq & (eq_rank < need))
    pos = _excl_prefix_lanes(jnp.where(sel, 1.0, 0.0))
    aff_ref[...] = aff
    pos_ref[...] = jnp.where(sel, pos, -1.0).astype(jnp.int32)

    tok = lax.broadcasted_iota(jnp.int32, (1, tchunk), 1)
    slot = lax.broadcasted_iota(jnp.int32, (cap, tchunk), 0)
    feat_rows = 2 * SUBLANES
    zero_rows = jnp.zeros((feat_rows - 5, tchunk), F32)

    def per_expert(e, carry):
        res = jnp.zeros((feat_rows, cap), F32)
        for j in range(t // tchunk):
            cols = pl.ds(j * tchunk, tchunk)
            a = aff_ref[pl.ds(e, 1), cols]
            a_hi = a.astype(BF16).astype(F32)
            a_mid = (a - a_hi).astype(BF16).astype(F32)
            a_lo = a - a_hi - a_mid
            tj = tok + j * tchunk
            feats = jnp.concatenate(
                [jnp.right_shift(tj, 6).astype(F32), jnp.bitwise_and(tj, 63).astype(F32),
                 a_hi, a_mid, a_lo, zero_rows], axis=0).astype(BF16)
            onehot = jnp.where(pos_ref[pl.ds(e, 1), cols] == slot, 1.0, 0.0).astype(BF16)
            res = res + lax.dot_general(feats, onehot, NT_DIMS, preferred_element_type=F32)
        idx_ref[0, pl.ds(e, 1), :] = (res[0:1] * 64.0 + res[1:2]).astype(jnp.int32)
        gate_ref[0, pl.ds(e, 1), :] = res[2:3] + res[3:4] + res[4:5]
        return carry

    lax.fori_loop(0, n_exp, per_expert, 0)


def route(logits_t, set_len, col0, n_sets):
    n_exp = logits_t.shape[0]
    cap = CAPACITY_FACTOR * set_len // n_exp
    blk0 = col0 // set_len
    return pl.pallas_call(
        functools.partial(_route_kernel, cap=cap, tchunk=min(set_len, 1024)),
        out_shape=[jax.ShapeDtypeStruct((n_sets, n_exp, cap), jnp.int32),
                   jax.ShapeDtypeStruct((n_sets, n_exp, cap), F32)],
        grid=(n_sets,),
        in_specs=[pl.BlockSpec((n_exp, set_len), lambda s: (0, blk0 + s))],
        out_specs=[pl.BlockSpec((1, n_exp, cap), lambda s: (s, 0, 0)),
                   pl.BlockSpec((1, n_exp, cap), lambda s: (s, 0, 0))],
        scratch_shapes=[pltpu.VMEM((n_exp, set_len), F32), pltpu.VMEM((n_exp, set_len), jnp.int32)],
        compiler_params=_params("arbitrary"),
        name="route",
    )(logits_t)


GATHER_ROWS = 16


def _gather_kernel(idx_ref, h_ref, o_ref, *, n_groups, slots):
    g = pl.program_id(0)
    e = pl.program_id(1)
    base = (e * n_groups + g) * slots

    def body(j, carry):
        s0 = pl.multiple_of(j * GATHER_ROWS, GATHER_ROWS)
        rows = [h_ref[0, pl.ds(idx_ref[base + s0 + i], 1), :] for i in range(GATHER_ROWS)]
        o_ref[0, 0, pl.ds(s0, GATHER_ROWS), :] = jnp.concatenate(rows, axis=0).astype(o_ref.dtype)
        return carry

    lax.fori_loop(0, slots // GATHER_ROWS, body, 0)


def moe_gather(idx_flat, h, n_exp, n_groups, slots):
    _, tg, d = h.shape
    return pl.pallas_call(
        functools.partial(_gather_kernel, n_groups=n_groups, slots=slots),
        out_shape=jax.ShapeDtypeStruct((n_exp, n_groups, slots, d), BF16),
        grid_spec=pltpu.PrefetchScalarGridSpec(
            num_scalar_prefetch=1,
            grid=(n_groups, n_exp),
            in_specs=[pl.BlockSpec((1, tg, d), lambda g, e, idx: (g, 0, 0))],
            out_specs=pl.BlockSpec((1, 1, slots, d), lambda g, e, idx: (e, g, 0, 0)),
        ),
        compiler_params=_params("arbitrary", "arbitrary"),
        name="moe_gather",
    )(idx_flat, h)


def _ffn_kernel(xs_ref, wg_ref, wu_ref, wd_ref, gt_ref, o_ref, *, tm):
    f = pl.program_id(1)
    wg = wg_ref[0].astype(BF16)
    wu = wu_ref[0].astype(BF16)
    wd = wd_ref[0].astype(BF16)

    @pl.when(f == 0)
    def _():
        o_ref[...] = jnp.zeros_like(o_ref)

    for i in range(xs_ref.shape[1] // tm):
        rows = slice(i * tm, (i + 1) * tm)
        x = xs_ref[0, rows, :]
        hid = _silu(jnp.dot(x, wg, preferred_element_type=F32)) * jnp.dot(
            x, wu, preferred_element_type=F32)
        o_ref[0, rows, :] += jnp.dot(hid.astype(BF16), wd, preferred_element_type=F32)

    @pl.when(f == pl.num_programs(1) - 1)
    def _():
        o_ref[0] = o_ref[0] * gt_ref[0]


def moe_ffn(xs, w_gate, w_up, w_down, gate_col, tf=512, tm=512):
    n_exp, m, d = xs.shape
    f = w_gate.shape[2]
    tf = min(tf, f)
    return pl.pallas_call(
        functools.partial(_ffn_kernel, tm=min(tm, m)),
        out_shape=jax.ShapeDtypeStruct((n_exp, m, d), F32),
        grid=(n_exp, f // tf),
        in_specs=[pl.BlockSpec((1, m, d), lambda e, j: (e, 0, 0)),
                  pl.BlockSpec((1, d, tf), lambda e, j: (e, 0, j)),
                  pl.BlockSpec((1, d, tf), lambda e, j: (e, 0, j)),
                  pl.BlockSpec((1, tf, d), lambda e, j: (e, j, 0)),
                  pl.BlockSpec((1, m, 1), lambda e, j: (e, 0, 0))],
        out_specs=pl.BlockSpec((1, m, d), lambda e, j: (e, 0, 0)),
        compiler_params=_params("arbitrary", "arbitrary"),
        name="moe_ffn",
    )(xs, w_gate, w_up, w_down, gate_col)


def _combine_kernel(idx_ref, ys_ref, o_ref, *, n_groups, slots):
    g = pl.program_id(0)
    e = pl.program_id(1)
    base = (e * n_groups + g) * slots

    @pl.when(e == 0)
    def _():
        o_ref[...] = jnp.zeros_like(o_ref)

    def body(s, carry):
        t = idx_ref[base + s]
        o_ref[0, pl.ds(t, 1), :] += ys_ref[0, 0, pl.ds(s, 1), :]
        return carry

    lax.fori_loop(0, slots, body, 0)


def moe_combine(idx_flat, ys, tg):
    n_exp, n_groups, slots, d = ys.shape
    return pl.pallas_call(
        functools.partial(_combine_kernel, n_groups=n_groups, slots=slots),
        out_shape=jax.ShapeDtypeStruct((n_groups, tg, d), F32),
        grid_spec=pltpu.PrefetchScalarGridSpec(
            num_scalar_prefetch=1,
            grid=(n_groups, n_exp),
            in_specs=[pl.BlockSpec((1, 1, slots, d), lambda g, e, idx: (e, g, 0, 0))],
            out_specs=pl.BlockSpec((1, tg, d), lambda g, e, idx: (g, 0, 0)),
        ),
        compiler_params=_params("arbitrary", "arbitrary"),
        name="moe_combine",
    )(idx_flat, ys)


def _moe_res_kernel(x_ref, y_ref, g_ref, nw_ref, sh_ref, sc_ref, x2_ref, h_ref):
    x2 = x_ref[...] + g_ref[0] * y_ref[...]
    x2_ref[...] = x2
    h_ref[...] = (_rms(x2, nw_ref[...]) * (1.0 + sc_ref[0]) + sh_ref[0]).astype(h_ref.dtype)


def moe_residual(x, y, gate, nw, shift, scale, tg, tm=512):
    m, d = x.shape
    per = tg // tm
    vec = pl.BlockSpec((1, 1, d), lambda i: (i // per, 0, 0))
    row = pl.BlockSpec((tm, d), lambda i: (i, 0))
    return pl.pallas_call(
        _moe_res_kernel,
        out_shape=[jax.ShapeDtypeStruct((m, d), F32), jax.ShapeDtypeStruct((m, d), BF16)],
        grid=(m // tm,),
        in_specs=[row, row, vec, pl.BlockSpec((1, d), lambda i: (0, 0)), vec, vec],
        out_specs=[row, row],
        compiler_params=_params("arbitrary"),
        name="moe_residual",
    )(x, y, gate, nw.reshape(1, d), shift, scale)


def _final_kernel(x_ref, y_ref, g_ref, nw_ref, o_ref):
    o_ref[...] = _rms(x_ref[...] + g_ref[0] * y_ref[...], nw_ref[...])


def final_norm(x, y, gate, nw, tg, tm=512):
    m, d = x.shape
    per = tg // tm
    row = pl.BlockSpec((tm, d), lambda i: (i, 0))
    return pl.pallas_call(
        _final_kernel,
        out_shape=jax.ShapeDtypeStruct((m, d), F32),
        grid=(m // tm,),
        in_specs=[row, row, pl.BlockSpec((1, 1, d), lambda i: (i // per, 0, 0)),
                  pl.BlockSpec((1, d), lambda i: (0, 0))],
        out_specs=row,
        compiler_params=_params("arbitrary"),
        name="final_norm",
    )(x, y, gate, nw.reshape(1, d))


CONV_HALO = 16


def _conv_kernel(prev_ref, cur_ref, next_ref, w_ref, b_ref, o_ref, *, tm, seq_a, n_a, seq_b):
    i = pl.program_id(0)
    row0 = i * tm
    seq = jnp.where(row0 < n_a, seq_a, seq_b)
    off = jnp.where(row0 < n_a, row0, row0 - n_a)
    first = lax.rem(off, seq) == 0
    last = lax.rem(off + tm, seq) == 0
    half = D_CONV // 2
    pad = SUBLANES
    prev = jnp.where(first, 0.0, prev_ref[...].astype(F32)[CONV_HALO - pad:])
    nxt = jnp.where(last, 0.0, next_ref[...].astype(F32)[:pad])
    ext = jnp.concatenate([prev, cur_ref[...].astype(F32), nxt], axis=0)
    acc = b_ref[...] + w_ref[0:1, :] * ext[pad - half:pad - half + tm]
    for k in range(1, D_CONV):
        acc = acc + w_ref[k:k + 1, :] * ext[pad - half + k:pad - half + k + tm]
    o_ref[...] = _silu(acc).astype(o_ref.dtype)


def ssd_conv(xbc, w, b, seq_a, n_a, seq_b, tm=256):
    m, c = xbc.shape
    hb = tm // CONV_HALO
    n_halo = m // CONV_HALO
    return pl.pallas_call(
        functools.partial(_conv_kernel, tm=tm, seq_a=seq_a, n_a=n_a, seq_b=seq_b),
        out_shape=jax.ShapeDtypeStruct((m, c), BF16),
        grid=(m // tm,),
        in_specs=[pl.BlockSpec((CONV_HALO, c), lambda i: (jnp.maximum(i * hb - 1, 0), 0)),
                  pl.BlockSpec((tm, c), lambda i: (i, 0)),
                  pl.BlockSpec((CONV_HALO, c), lambda i: (jnp.minimum((i + 1) * hb, n_halo - 1), 0)),
                  pl.BlockSpec((D_CONV, c), lambda i: (0, 0)),
                  pl.BlockSpec((1, c), lambda i: (0, 0))],
        out_specs=pl.BlockSpec((tm, c), lambda i: (i, 0)),
        compiler_params=_params("arbitrary"),
        name="ssd_conv",
    )(xbc, xbc, xbc, w, b.reshape(1, c))


def _softplus(x):
    return jnp.maximum(x, 0.0) + jnp.log(1.0 + jnp.exp(-jnp.abs(x)))


def _scan_kernel(*refs, n_heads, zero_init):
    if zero_init:
        xbc_ref, dt_ref, dtb_ref, alog_ref, y_ref, sout_ref, st_ref = refs
    else:
        xbc_ref, dt_ref, dtb_ref, alog_ref, s0_ref, y_ref, sout_ref, st_ref = refs
    d = pl.program_id(1)
    c = pl.program_id(2)
    q = xbc_ref.shape[0]
    p, n, g_n = SSD_HEADDIM, D_STATE, N_GROUPS
    hpg = n_heads // g_n
    d_inner = n_heads * p
    fwd = d == 0

    @pl.when(c == 0)
    def _():
        if zero_init:
            st_ref[...] = jnp.zeros_like(st_ref)
        else:
            st_ref[...] = s0_ref[0, 0]

    dt = _softplus(dt_ref[...] + dtb_ref[0])
    da = dt * (-jnp.exp(alog_ref[0]))
    li = lax.broadcasted_iota(jnp.int32, (q, q), 0)
    si = lax.broadcasted_iota(jnp.int32, (q, q), 1)
    tri = (li - si) * jnp.where(fwd, 1, -1) >= 0
    acs = jnp.dot(jnp.where(tri, 1.0, 0.0), da, preferred_element_type=F32, precision=HIGHEST)
    acs_t = acs.T
    acs_end = jnp.where(fwd, acs[q - 1:q, :], acs[0:1, :])
    e_in = jnp.exp(acs)
    e_out = jnp.exp(acs_end - acs)
    e_all = jnp.exp(acs_end)

    for g in range(g_n):
        bm = xbc_ref[:, d_inner + g * n:d_inner + (g + 1) * n]
        cm = xbc_ref[:, d_inner + g_n * n + g * n:d_inner + g_n * n + (g + 1) * n]
        cb = lax.dot_general(cm, bm, NT_DIMS, preferred_element_type=F32)
        st = st_ref[g]
        y_off = jnp.dot(cm, st.astype(BF16), preferred_element_type=F32)
        xd_cols = []
        dec_cols = []
        for e in range(hpg):
            h = g * hpg + e
            x = xbc_ref[:, h * p:(h + 1) * p].astype(F32)
            xdt = x * dt[:, h:h + 1]
            seg = acs[:, h:h + 1] - acs_t[h:h + 1, :]
            lmat = jnp.where(tri, jnp.exp(seg), 0.0)
            y = jnp.dot((cb * lmat).astype(BF16), xdt.astype(BF16), preferred_element_type=F32)
            y = y + y_off[:, e * p:(e + 1) * p] * e_in[:, h:h + 1]
            y_ref[0, :, h * p:(h + 1) * p] = y.astype(y_ref.dtype)
            xd_cols.append(xdt * e_out[:, h:h + 1])
            dec_cols.append(jnp.broadcast_to(e_all[:, h:h + 1], (1, p)))
        xd = jnp.concatenate(xd_cols, axis=1).astype(BF16)
        bm_t = bm.astype(F32).T.astype(BF16)
        s_new = jnp.dot(bm_t, xd, preferred_element_type=F32)
        st_ref[g] = st * jnp.concatenate(dec_cols, axis=1) + s_new

    @pl.when(c == pl.num_programs(2) - 1)
    def _():
        sout_ref[0, 0] = st_ref[...]


def ssd_scan(xconv, dt2, dtb2, alog2, s0_t, n_heads, row0, n_seq, seq_len):
    m, c = xconv.shape
    q = SSD_CHUNK
    nc = seq_len // q
    blk0 = row0 // q
    hp = n_heads // N_GROUPS * SSD_HEADDIM
    d_inner = n_heads * SSD_HEADDIM

    def rowblk(s, d, ci):
        return blk0 + s * nc + jnp.where(d == 0, ci, nc - 1 - ci)

    st_spec = pl.BlockSpec((1, 1, N_GROUPS, D_STATE, hp), lambda s, d, ci: (d, s, 0, 0, 0))
    in_specs = [pl.BlockSpec((q, c), lambda s, d, ci: (rowblk(s, d, ci), 0)),
                pl.BlockSpec((q, LANES), lambda s, d, ci: (rowblk(s, d, ci), d)),
                pl.BlockSpec((1, 1, LANES), lambda s, d, ci: (d, 0, 0)),
                pl.BlockSpec((1, 1, LANES), lambda s, d, ci: (d, 0, 0))]
    args = [xconv, dt2, dtb2, alog2]
    if s0_t is not None:
        in_specs.append(st_spec)
        args.append(s0_t)
    return pl.pallas_call(
        functools.partial(_scan_kernel, n_heads=n_heads, zero_init=s0_t is None),
        out_shape=[jax.ShapeDtypeStruct((2, m, d_inner), BF16),
                   jax.ShapeDtypeStruct((2, n_seq, N_GROUPS, D_STATE, hp), F32)],
        grid=(n_seq, 2, nc),
        in_specs=in_specs,
        out_specs=[pl.BlockSpec((1, q, d_inner), lambda s, d, ci: (d, rowblk(s, d, ci), 0)), st_spec],
        scratch_shapes=[pltpu.VMEM((N_GROUPS, D_STATE, hp), F32)],
        compiler_params=_params("arbitrary", "arbitrary", "arbitrary"),
        name="ssd_scan",
    )(*args)


def _pad_lanes(v, n_heads):
    return jnp.pad(v.astype(F32), ((0, 0), (0, LANES - n_heads))).reshape(2, 1, LANES)


def _state_to_t(s, n_heads):
    b = s.shape[0]
    hpg = n_heads // N_GROUPS
    s = s.reshape(b, N_GROUPS, hpg, SSD_HEADDIM, D_STATE)
    return s.transpose(0, 1, 4, 2, 3).reshape(b, N_GROUPS, D_STATE, hpg * SSD_HEADDIM)


def _state_from_t(st, n_heads):
    b = st.shape[0]
    hpg = n_heads // N_GROUPS
    st = st.reshape(b, N_GROUPS, D_STATE, hpg, SSD_HEADDIM)
    return st.transpose(0, 1, 3, 4, 2).reshape(b, n_heads, SSD_HEADDIM, D_STATE)


def _moe(h, logits_t, x1, gate_vec, w_gate, w_up, w_down, dims):
    n_req, seq, n_lat, tg, d = dims
    n_groups = 1 + n_lat
    n_exp = logits_t.shape[0]
    idx_p, gate_p = route(logits_t, seq, 0, n_req)
    idx_s, gate_s = route(logits_t, tg, tg, n_lat)
    slots = idx_s.shape[2]
    idx_p = idx_p + (jnp.arange(n_req, dtype=jnp.int32) * seq)[:, None, None]
    idx = jnp.concatenate([idx_p.transpose(1, 0, 2).reshape(n_exp, 1, slots),
                           idx_s.transpose(1, 0, 2)], axis=1)
    gate = jnp.concatenate([gate_p.transpose(1, 0, 2).reshape(n_exp, 1, slots),
                            gate_s.transpose(1, 0, 2)], axis=1)
    idx_flat = idx.reshape(-1)
    xs = moe_gather(idx_flat, h.reshape(n_groups, tg, d), n_exp, n_groups, slots)
    ys = moe_ffn(xs.reshape(n_exp, n_groups * slots, d), w_gate, w_up, w_down,
                 gate.reshape(n_exp, n_groups * slots, 1))
    out = moe_combine(idx_flat, ys.reshape(n_exp, n_groups, slots, d), tg)
    return out.reshape(n_groups * tg, d)


def kernel(x_prompt, x_sample, cache_k_na, cache_v_na, state_ssd_fwd, state_ssd_bwd, c, c_ctx, ada_w, ada_b, norm1_w, norm2_w, final_norm_w, na_qkv_w, na_out_w, na_rpb, ssd_in_w, ssd_conv_w, ssd_conv_b, ssd_dt_bias, ssd_a_log, ssd_d, ssd_norm_w, ssd_out_w, router_w, moe_w_gate, moe_w_up, moe_w_down):
    n_req, seq, d = x_prompt.shape
    n_lat, tg, _ = x_sample.shape
    assert n_req * seq == tg, "token groups must have equal size"
    assert n_lat + 1 <= MOD_ROWS
    depth = ada_w.shape[0]
    n_groups = 1 + n_lat
    m_p = n_req * seq
    dims = (n_req, seq, n_lat, tg, d)

    cvec = jnp.zeros((MOD_ROWS, d), F32).at[0].set(c_ctx).at[1:1 + n_lat].set(c)
    mod = ada_modulation(cvec, ada_w, ada_b)
    mod = mod.reshape(depth, MOD_ROWS, N_MOD, 1, d).transpose(0, 2, 1, 3, 4)

    x = jnp.concatenate([x_prompt.reshape(m_p, d), x_sample.reshape(n_lat * tg, d)], axis=0)
    h = prenorm(x, norm1_w[0], mod[0, 0], mod[0, 1], tg, BF16)

    new_k, new_v, new_sf, new_sb = [], [], [], []
    y_out = None
    for layer in range(depth):
        j = layer // 2
        ml = mod[layer]
        if layer % 2 == 0:
            qkv_p = matmul(h[:m_p], na_qkv_w[j], F32)
            qkv_s = matmul(h[m_p:], na_qkv_w[j], BF16)
            new_k.append(qkv_p[:, d:2 * d].reshape(n_req, seq, N_HEADS, d // N_HEADS))
            new_v.append(qkv_p[:, 2 * d:].reshape(n_req, seq, N_HEADS, d // N_HEADS))
            att_p = ctx_attention(qkv_p, seq)
            past = cache_k_na.shape[2]
            att_s = na_attention(qkv_s.reshape(n_lat, tg, 3 * d),
                                 cache_k_na[:, j].reshape(n_lat, past, d).astype(BF16),
                                 cache_v_na[:, j].reshape(n_lat, past, d).astype(BF16),
                                 na_bias_table(na_rpb[j], tg // GRID_W))
            att = jnp.concatenate([att_p, att_s.reshape(n_lat * tg, d)], axis=0)
            x1, hm, logits_t = attn_out(att, na_out_w[j], x, ml[2], norm2_w[layer], ml[3], ml[4],
                                        router_w[layer], tg)
        else:
            n_heads = ssd_d.shape[1]
            d_inner = n_heads * SSD_HEADDIM
            conv_dim = ssd_conv_w.shape[2]
            in_w = ssd_in_w[j]
            z = matmul(h, in_w[:, :d_inner], BF16)
            xbc = matmul(h, in_w[:, d_inner:d_inner + conv_dim], BF16)
            w_dt = in_w[:, d_inner + conv_dim:]
            w_dt2 = jnp.zeros((d, 2 * LANES), F32)
            w_dt2 = w_dt2.at[:, :n_heads].set(w_dt[:, :n_heads])
            w_dt2 = w_dt2.at[:, LANES:LANES + n_heads].set(w_dt[:, n_heads:])
            dt2 = matmul(h, w_dt2, F32)
            xconv = ssd_conv(xbc, ssd_conv_w[j], ssd_conv_b[j], seq, m_p, tg)
            dtb2 = _pad_lanes(ssd_dt_bias[j], n_heads)
            alog2 = _pad_lanes(ssd_a_log[j], n_heads)
            y_p, st_p = ssd_scan(xconv, dt2, dtb2, alog2, None, n_heads, 0, n_req, seq)
            s0 = jnp.stack([_state_to_t(state_ssd_fwd[:, j], n_heads),
                            _state_to_t(state_ssd_bwd[:, j], n_heads)])
            y_s, _ = ssd_scan(xconv, dt2, dtb2, alog2, s0, n_heads, m_p, n_lat, tg)
            y2 = jnp.concatenate([y_p[:, :m_p], y_s[:, m_p:]], axis=1)
            new_sf.append(_state_from_t(st_p[0], n_heads))
            new_sb.append(_state_from_t(st_p[1], n_heads))
            d_cols = jnp.repeat(ssd_d[j], SSD_HEADDIM)
            x1, hm, logits_t = ssd_out(y2, xconv, z, d_cols, ssd_norm_w[j], ssd_out_w[j], x, ml[2],
                                       norm2_w[layer], ml[3], ml[4], router_w[layer], tg)
        y_moe = _moe(hm, logits_t, x1, ml[5], moe_w_gate[layer], moe_w_up[layer], moe_w_down[layer],
                     dims)
        if layer + 1 < depth:
            mn = mod[layer + 1]
            x, h = moe_residual(x1, y_moe, ml[5], norm1_w[layer + 1], mn[0], mn[1], tg)
        else:
            y_out = final_norm(x1, y_moe, ml[5], final_norm_w, tg)

    y_prompt = y_out[:m_p].reshape(n_req, seq, d)
    y_sample = y_out[m_p:].reshape(n_lat, tg, d)
    return (y_prompt, y_sample, jnp.stack(new_k, axis=1), jnp.stack(new_v, axis=1),
            jnp.stack(new_sf, axis=1), jnp.stack(new_sb, axis=1))
```

```python
import functools

import jax
import jax.numpy as jnp
from jax import lax
from jax.experimental import pallas as pl
from jax.experimental.pallas import tpu as pltpu

N_HEADS = 16
GRID_W = 64
WIN_H = 8
WIN_W = 16
SSD_HEADDIM = 64
N_GROUPS = 4
D_STATE = 128
D_CONV = 5
SSD_CHUNK = 128
CAPACITY_FACTOR = 2
N_MOD = 6
RMS_EPS = 1e-6

LANES = 128
SUBLANES = 8
MOD_ROWS = 8
VMEM_LIMIT = 56 * 1024 * 1024
NEG_INF = -1e30

F32 = jnp.float32
BF16 = jnp.bfloat16
HIGHEST = lax.Precision.HIGHEST
NT_DIMS = (((1,), (1,)), ((), ()))


def _params(*sem):
    return pltpu.CompilerParams(dimension_semantics=sem, vmem_limit_bytes=VMEM_LIMIT)


def _silu(x):
    return x / (1.0 + jnp.exp(-x))


def _rms(x, w):
    ms = jnp.mean(x * x, axis=-1, keepdims=True)
    return x * lax.rsqrt(ms + RMS_EPS) * w


def _ada_kernel(c_ref, w_ref, b_ref, o_ref):
    o_ref[0] = jnp.dot(_silu(c_ref[...]), w_ref[0], preferred_element_type=F32,
                       precision=HIGHEST) + b_ref[0]


def ada_modulation(cvec, ada_w, ada_b):
    n_layers, d, n = ada_w.shape
    tn = n // 4
    return pl.pallas_call(
        _ada_kernel,
        out_shape=jax.ShapeDtypeStruct((n_layers, MOD_ROWS, n), F32),
        grid=(n_layers, n // tn),
        in_specs=[pl.BlockSpec((MOD_ROWS, d), lambda l, j: (0, 0)),
                  pl.BlockSpec((1, d, tn), lambda l, j: (l, 0, j)),
                  pl.BlockSpec((1, 1, tn), lambda l, j: (l, 0, j))],
        out_specs=pl.BlockSpec((1, MOD_ROWS, tn), lambda l, j: (l, 0, j)),
        compiler_params=_params("arbitrary", "arbitrary"),
        name="ada_modulation",
    )(cvec, ada_w, ada_b.reshape(n_layers, 1, n))


def _two_source_specs(tm, width, n_first, off_second):
    first = pl.BlockSpec((tm, width), lambda i: (jnp.minimum(i, n_first - 1), 0))
    second = pl.BlockSpec((tm, width), lambda i: (jnp.maximum(i - n_first, 0) + off_second, 0))
    return [first, second]


def _pick(first_ref, second_ref, n_first):
    return jnp.where(pl.program_id(0) < n_first, first_ref[...], second_ref[...])


def _prenorm_kernel(xa_ref, xb_ref, nw_ref, sh_ref, sc_ref, h_ref, *, n_first):
    x = _pick(xa_ref, xb_ref, n_first)
    h = _rms(x, nw_ref[...]) * (1.0 + sc_ref[0]) + sh_ref[0]
    h_ref[...] = h.astype(h_ref.dtype)


def prenorm(xa, xb, nw, shift, scale, tg, out_dtype, tm=512):
    d = xa.shape[1]
    m = xa.shape[0] + xb.shape[0]
    per = tg // tm
    n_first = xa.shape[0] // tm
    vec = pl.BlockSpec((1, 1, d), lambda i: (i // per, 0, 0))
    return pl.pallas_call(
        functools.partial(_prenorm_kernel, n_first=n_first),
        out_shape=jax.ShapeDtypeStruct((m, d), out_dtype),
        grid=(m // tm,),
        in_specs=_two_source_specs(tm, d, n_first, 0) + [pl.BlockSpec((1, d), lambda i: (0, 0)), vec, vec],
        out_specs=pl.BlockSpec((tm, d), lambda i: (i, 0)),
        compiler_params=_params("arbitrary"),
        name="prenorm",
    )(xa, xb, nw.reshape(1, d), shift, scale)


def _mm_kernel(x_ref, w_ref, o_ref, wb_ref):
    @pl.when(pl.program_id(1) == 0)
    def _():
        wb_ref[...] = w_ref[0].astype(BF16)

    o_ref[...] = jnp.dot(x_ref[...], wb_ref[...], preferred_element_type=F32).astype(o_ref.dtype)


def matmul(x, w, layer, out_dtype, row0=0, rows=None, col0=0, cols=None, tm=1024, tn=1024):
    k = x.shape[1]
    rows = x.shape[0] - row0 if rows is None else rows
    cols = w.shape[2] - col0 if cols is None else cols
    tn = max(t for t in range(LANES, min(tn, cols) + 1, LANES) if cols % t == 0 and col0 % t == 0)
    tm = min(tm, rows)
    r_off, c_off = row0 // tm, col0 // tn
    return pl.pallas_call(
        _mm_kernel,
        out_shape=jax.ShapeDtypeStruct((rows, cols), out_dtype),
        grid=(cols // tn, rows // tm),
        in_specs=[pl.BlockSpec((tm, k), lambda j, i: (r_off + i, 0)),
                  pl.BlockSpec((1, k, tn), lambda j, i: (layer, 0, c_off + j))],
        out_specs=pl.BlockSpec((tm, tn), lambda j, i: (i, j)),
        scratch_shapes=[pltpu.VMEM((k, tn), BF16)],
        compiler_params=_params("arbitrary", "arbitrary"),
        name="matmul",
    )(x, w)


def _softmax_pv(s_list, v_list):
    m = s_list[0].max(axis=-1, keepdims=True)
    for s in s_list[1:]:
        m = jnp.maximum(m, s.max(axis=-1, keepdims=True))
    den = None
    acc = None
    for s, v in zip(s_list, v_list):
        p = jnp.exp(s - m)
        ps = p.sum(axis=-1, keepdims=True)
        pv = jnp.dot(p.astype(BF16), v, preferred_element_type=F32)
        den = ps if den is None else den + ps
        acc = pv if acc is None else acc + pv
    return acc / den


def _ctx_attn_kernel(q_ref, k_ref, v_ref, o_ref, *, n_heads):
    dh = q_ref.shape[1] // n_heads
    scale = dh ** -0.5
    for h in range(n_heads):
        sl = slice(h * dh, (h + 1) * dh)
        q = q_ref[:, sl].astype(BF16)
        k = k_ref[:, sl].astype(BF16)
        v = v_ref[:, sl].astype(BF16)
        s = lax.dot_general(q, k, NT_DIMS, preferred_element_type=F32) * scale
        o_ref[:, sl] = _softmax_pv([s], [v]).astype(o_ref.dtype)


def ctx_attention(q, k, v, seq):
    m, d = q.shape
    blk = pl.BlockSpec((seq, d), lambda r: (r, 0))
    return pl.pallas_call(
        functools.partial(_ctx_attn_kernel, n_heads=N_HEADS),
        out_shape=jax.ShapeDtypeStruct((m, d), BF16),
        grid=(m // seq,),
        in_specs=[blk, blk, blk],
        out_specs=blk,
        compiler_params=_params("arbitrary"),
        name="ctx_attention",
    )(q, k, v)


def _na_row_start(r, rows, kh):
    return jnp.clip(r - kh // 2, 0, rows - kh)


def na_bias_table(rpb, rows):
    w, kw = GRID_W, WIN_W
    kh = min(WIN_H, rows)
    col = jnp.arange(w)
    cs = jnp.clip(col - kw // 2, 0, w - kw)
    dc = col[None, :] - col[:, None] + (WIN_W - 1)
    valid = (col[None, :] >= cs[:, None]) & (col[None, :] < cs[:, None] + kw)
    c_full = rpb[:, :, jnp.clip(dc, 0, 2 * WIN_W - 2)]
    c_full = jnp.where(valid[None, None], c_full, NEG_INF)
    n_off = WIN_H
    tab = jnp.stack([jnp.concatenate([c_full[:, off + j] for j in range(kh)], axis=-1)
                     for off in range(n_off)])
    return tab.reshape(n_off, rpb.shape[0] // 2, 2 * w, kh * w)


def _na_kernel(q_ref, k_ref, v_ref, kc_ref, vc_ref, b_ref, o_ref, *, n_heads, rows, kh, w):
    dh = q_ref.shape[2] // n_heads
    scale = dh ** -0.5
    r0 = _na_row_start(pl.program_id(1), rows, kh)
    start = pl.multiple_of(r0 * w, w)
    first = lax.broadcasted_iota(jnp.int32, (w, 2 * dh), 1) < dh
    for j in range(n_heads // 2):
        sl = slice(2 * j * dh, 2 * (j + 1) * dh)
        q2 = q_ref[0, :, sl]
        qq = jnp.concatenate([jnp.where(first, q2, 0), jnp.where(first, 0, q2)], axis=0)
        kwin = k_ref[0, pl.ds(start, kh * w), sl]
        vwin = v_ref[0, pl.ds(start, kh * w), sl]
        s_win = lax.dot_general(qq, kwin, NT_DIMS, preferred_element_type=F32) * scale + b_ref[0, j]
        s_ctx = lax.dot_general(qq, kc_ref[0, :, sl], NT_DIMS, preferred_element_type=F32) * scale
        o2 = _softmax_pv([s_win, s_ctx], [vwin, vc_ref[0, :, sl]])
        o_ref[0, :, sl] = jnp.where(first, o2[:w], o2[w:]).astype(o_ref.dtype)


def na_attention(qkv, k_ctx, v_ctx, bias):
    b, t, d3 = qkv.shape
    d = d3 // 3
    w = GRID_W
    rows = t // w
    kh = min(WIN_H, rows)
    n_ctx = k_ctx.shape[1]

    def bias_map(bi, r):
        return (_na_row_start(r, rows, kh) - r + WIN_H - 1, 0, 0, 0)

    return pl.pallas_call(
        functools.partial(_na_kernel, n_heads=N_HEADS, rows=rows, kh=kh, w=w),
        out_shape=jax.ShapeDtypeStruct((b, t, d), BF16),
        grid=(b, rows),
        in_specs=[pl.BlockSpec((1, w, d), lambda bi, r: (bi, r, 0)),
                  pl.BlockSpec((1, t, d), lambda bi, r: (bi, 0, 1)),
                  pl.BlockSpec((1, t, d), lambda bi, r: (bi, 0, 2)),
                  pl.BlockSpec((1, n_ctx, d), lambda bi, r: (bi, 0, 0)),
                  pl.BlockSpec((1, n_ctx, d), lambda bi, r: (bi, 0, 0)),
                  pl.BlockSpec((1, N_HEADS // 2, 2 * w, kh * w), bias_map)],
        out_specs=pl.BlockSpec((1, w, d), lambda bi, r: (bi, r, 0)),
        compiler_params=_params("arbitrary", "arbitrary"),
        name="na_attention",
    )(qkv, qkv, qkv, k_ctx, v_ctx, bias)


def _residual_router_tail(mix, x, g_ref, nw_ref, sh_ref, sc_ref, rw_ref, x1_ref, h_ref, lg_ref):
    x1 = x + g_ref[0] * mix
    x1_ref[...] = x1
    h = _rms(x1, nw_ref[...]) * (1.0 + sc_ref[0]) + sh_ref[0]
    h_ref[...] = h
    lg_ref[...] = lax.dot_general(rw_ref[...], h, NT_DIMS, preferred_element_type=F32,
                                  precision=HIGHEST)


def _attn_out_kernel(aa_ref, ab_ref, w_ref, xa_ref, xb_ref, g_ref, nw_ref, sh_ref, sc_ref, rw_ref,
                     x1_ref, h_ref, lg_ref, wb_ref, *, n_first):
    @pl.when(pl.program_id(0) == 0)
    def _():
        wb_ref[...] = w_ref[0].astype(BF16)

    mix = jnp.dot(_pick(aa_ref, ab_ref, n_first), wb_ref[...], preferred_element_type=F32)
    _residual_router_tail(mix, _pick(xa_ref, xb_ref, n_first), g_ref, nw_ref, sh_ref, sc_ref,
                          rw_ref, x1_ref, h_ref, lg_ref)


def _ssd_out_kernel(yf_ref, yb_ref, xc_ref, z_ref, dsk_ref, snw_ref, w_ref, x_ref, g_ref, nw_ref,
                    sh_ref, sc_ref, rw_ref, x1_ref, h_ref, lg_ref, wb_ref):
    @pl.when(pl.program_id(0) == 0)
    def _():
        wb_ref[...] = w_ref[0].astype(BF16)

    y = (yf_ref[0].astype(F32) + yb_ref[0].astype(F32)) + dsk_ref[...] * xc_ref[...].astype(F32)
    y = _rms(y * _silu(z_ref[...].astype(F32)), snw_ref[...])
    mix = jnp.dot(y.astype(BF16), wb_ref[...], preferred_element_type=F32)
    _residual_router_tail(mix, x_ref[...], g_ref, nw_ref, sh_ref, sc_ref, rw_ref, x1_ref, h_ref,
                          lg_ref)


def _tail_specs(d, n_exp, tm, per):
    vec = pl.BlockSpec((1, 1, d), lambda i: (i // per, 0, 0))
    row = pl.BlockSpec((tm, d), lambda i: (i, 0))
    in_specs = [vec, pl.BlockSpec((1, d), lambda i: (0, 0)), vec, vec,
                pl.BlockSpec((n_exp, d), lambda i: (0, 0))]
    out_specs = [row, row, pl.BlockSpec((n_exp, tm), lambda i: (0, i))]
    return in_specs, out_specs


def _tail_out_shape(m, d, n_exp):
    return [jax.ShapeDtypeStruct((m, d), F32), jax.ShapeDtypeStruct((m, d), F32),
            jax.ShapeDtypeStruct((n_exp, m), F32)]


def attn_out(a_first, a_second, w, layer, x_first, x_second, x_second_row0, gate, nw, shift, scale,
             router_w, tg, tm=512):
    k = a_first.shape[1]
    d = w.shape[2]
    n_first = a_first.shape[0] // tm
    m = a_first.shape[0] + a_second.shape[0]
    n_exp = router_w.shape[1]
    tail_in, tail_out = _tail_specs(d, n_exp, tm, tg // tm)
    return pl.pallas_call(
        functools.partial(_attn_out_kernel, n_first=n_first),
        out_shape=_tail_out_shape(m, d, n_exp),
        grid=(m // tm,),
        in_specs=_two_source_specs(tm, k, n_first, 0)
        + [pl.BlockSpec((1, k, d), lambda i: (layer, 0, 0))]
        + _two_source_specs(tm, d, n_first, x_second_row0 // tm) + tail_in,
        out_specs=tail_out,
        scratch_shapes=[pltpu.VMEM((k, d), BF16)],
        compiler_params=_params("arbitrary"),
        name="attn_out",
    )(a_first, a_second, w, x_first, x_second, gate, nw.reshape(1, d), shift, scale, router_w.T)


def ssd_out(y2, xconv, z, d_cols, snw, w, layer, x, gate, nw, shift, scale, router_w, tg, tm=512):
    _, k, d = w.shape
    m = x.shape[0]
    n_exp = router_w.shape[1]
    tail_in, tail_out = _tail_specs(d, n_exp, tm, tg // tm)
    return pl.pallas_call(
        _ssd_out_kernel,
        out_shape=_tail_out_shape(m, d, n_exp),
        grid=(m // tm,),
        in_specs=[pl.BlockSpec((1, tm, k), lambda i: (0, i, 0)),
                  pl.BlockSpec((1, tm, k), lambda i: (1, i, 0)),
                  pl.BlockSpec((tm, k), lambda i: (i, 0)),
                  pl.BlockSpec((tm, k), lambda i: (i, 0)),
                  pl.BlockSpec((1, k), lambda i: (0, 0)),
                  pl.BlockSpec((1, k), lambda i: (0, 0)),
                  pl.BlockSpec((1, k, d), lambda i: (layer, 0, 0)),
                  pl.BlockSpec((tm, d), lambda i: (i, 0))] + tail_in,
        out_specs=tail_out,
        scratch_shapes=[pltpu.VMEM((k, d), BF16)],
        compiler_params=_params("arbitrary"),
        name="ssd_out",
    )(y2, y2, xconv, z, d_cols.reshape(1, k), snw.reshape(1, k), w, x, gate, nw.reshape(1, d),
      shift, scale, router_w.T)


def _excl_prefix_lanes(m01):
    e, t = m01.shape
    r = lax.broadcasted_iota(jnp.int32, (LANES, LANES), 0)
    c = lax.broadcasted_iota(jnp.int32, (LANES, LANES), 1)
    upper = jnp.where(r < c, 1.0, 0.0).astype(BF16)
    outs = []
    carry = jnp.zeros((e, 1), F32)
    for j in range(t // LANES):
        blk = m01[:, j * LANES:(j + 1) * LANES]
        outs.append(jnp.dot(blk.astype(BF16), upper, preferred_element_type=F32) + carry)
        carry = carry + blk.sum(axis=1, keepdims=True)
    return jnp.concatenate(outs, axis=1)


def _route_kernel(lg_ref, idx_ref, gate_ref, aff_ref, pos_ref, *, cap, tchunk):
    n_exp, t = lg_ref.shape
    lg = lg_ref[...]
    ex = jnp.exp(lg - lg.max(axis=0, keepdims=True))
    aff = ex / ex.sum(axis=0, keepdims=True)
    bits = pltpu.bitcast(aff, jnp.int32)

    def search(i, cur):
        cand = cur | jnp.left_shift(jnp.int32(1), 30 - i)
        cnt = jnp.where(bits >= cand, 1.0, 0.0).sum(axis=1, keepdims=True)
        return jnp.where(cnt >= cap, cand, cur)

    thr = lax.fori_loop(0, 31, search, jnp.zeros((n_exp, 1), jnp.int32))
    gt = bits > thr
    eq = bits == thr
    need = cap - jnp.where(gt, 1.0, 0.0).sum(axis=1, keepdims=True)
    eq_rank = _excl_prefix_lanes(jnp.where(eq, 1.0, 0.0))
    sel = gt | (eq & (eq_rank < need))
    pos = _excl_prefix_lanes(jnp.where(sel, 1.0, 0.0))
    aff_ref[...] = aff
    pos_ref[...] = jnp.where(sel, pos, -1.0).astype(jnp.int32)

    tok = lax.broadcasted_iota(jnp.int32, (1, tchunk), 1)
    slot = lax.broadcasted_iota(jnp.int32, (cap, tchunk), 0)
    feat_rows = 2 * SUBLANES
    zero_rows = jnp.zeros((feat_rows - 5, tchunk), F32)

    def per_expert(e, carry):
        res = jnp.zeros((feat_rows, cap), F32)
        for j in range(t // tchunk):
            cols = pl.ds(j * tchunk, tchunk)
            a = aff_ref[pl.ds(e, 1), cols]
            a_hi = a.astype(BF16).astype(F32)
            a_mid = (a - a_hi).astype(BF16).astype(F32)
            a_lo = a - a_hi - a_mid
            tj = tok + j * tchunk
            feats = jnp.concatenate(
                [jnp.right_shift(tj, 6).astype(F32), jnp.bitwise_and(tj, 63).astype(F32),
                 a_hi, a_mid, a_lo, zero_rows], axis=0).astype(BF16)
            onehot = jnp.where(pos_ref[pl.ds(e, 1), cols] == slot, 1.0, 0.0).astype(BF16)
            res = res + lax.dot_general(feats, onehot, NT_DIMS, preferred_element_type=F32)
        idx_ref[0, pl.ds(e, 1), :] = (res[0:1] * 64.0 + res[1:2]).astype(jnp.int32)
        gate_ref[0, pl.ds(e, 1), :] = res[2:3] + res[3:4] + res[4:5]
        return carry

    lax.fori_loop(0, n_exp, per_expert, 0)


def route(logits_t, set_len, col0, n_sets):
    n_exp = logits_t.shape[0]
    cap = CAPACITY_FACTOR * set_len // n_exp
    blk0 = col0 // set_len
    return pl.pallas_call(
        functools.partial(_route_kernel, cap=cap, tchunk=min(set_len, 1024)),
        out_shape=[jax.ShapeDtypeStruct((n_sets, n_exp, cap), jnp.int32),
                   jax.ShapeDtypeStruct((n_sets, n_exp, cap), F32)],
        grid=(n_sets,),
        in_specs=[pl.BlockSpec((n_exp, set_len), lambda s: (0, blk0 + s))],
        out_specs=[pl.BlockSpec((1, n_exp, cap), lambda s: (s, 0, 0)),
                   pl.BlockSpec((1, n_exp, cap), lambda s: (s, 0, 0))],
        scratch_shapes=[pltpu.VMEM((n_exp, set_len), F32), pltpu.VMEM((n_exp, set_len), jnp.int32)],
        compiler_params=_params("arbitrary"),
        name="route",
    )(logits_t)


GATHER_ROWS = 16


def _gather_kernel(idx_ref, h_ref, o_ref, *, n_groups, slots):
    g = pl.program_id(0)
    e = pl.program_id(1)
    base = (e * n_groups + g) * slots

    def body(j, carry):
        s0 = pl.multiple_of(j * GATHER_ROWS, GATHER_ROWS)
        rows = [h_ref[0, pl.ds(idx_ref[base + s0 + i], 1), :] for i in range(GATHER_ROWS)]
        o_ref[0, 0, pl.ds(s0, GATHER_ROWS), :] = jnp.concatenate(rows, axis=0).astype(o_ref.dtype)
        return carry

    lax.fori_loop(0, slots // GATHER_ROWS, body, 0)


def moe_gather(idx_flat, h, n_exp, n_groups, slots):
    _, tg, d = h.shape
    return pl.pallas_call(
        functools.partial(_gather_kernel, n_groups=n_groups, slots=slots),
        out_shape=jax.ShapeDtypeStruct((n_exp, n_groups, slots, d), BF16),
        grid_spec=pltpu.PrefetchScalarGridSpec(
            num_scalar_prefetch=1,
            grid=(n_groups, n_exp),
            in_specs=[pl.BlockSpec((1, tg, d), lambda g, e, idx: (g, 0, 0))],
            out_specs=pl.BlockSpec((1, 1, slots, d), lambda g, e, idx: (e, g, 0, 0)),
        ),
        compiler_params=_params("arbitrary", "arbitrary"),
        name="moe_gather",
    )(idx_flat, h)


def _ffn_kernel(xs_ref, wg_ref, wu_ref, wd_ref, gt_ref, o_ref, *, tm):
    f = pl.program_id(1)
    wg = wg_ref[0, 0].astype(BF16)
    wu = wu_ref[0, 0].astype(BF16)
    wd = wd_ref[0, 0].astype(BF16)

    @pl.when(f == 0)
    def _():
        o_ref[...] = jnp.zeros_like(o_ref)

    for i in range(xs_ref.shape[1] // tm):
        rows = slice(i * tm, (i + 1) * tm)
        x = xs_ref[0, rows, :]
        hid = _silu(jnp.dot(x, wg, preferred_element_type=F32)) * jnp.dot(
            x, wu, preferred_element_type=F32)
        o_ref[0, rows, :] += jnp.dot(hid.astype(BF16), wd, preferred_element_type=F32)

    @pl.when(f == pl.num_programs(1) - 1)
    def _():
        o_ref[0] = o_ref[0] * gt_ref[0]


def moe_ffn(xs, w_gate, w_up, w_down, layer, gate_col, tf=512, tm=512):
    n_exp, m, d = xs.shape
    f = w_gate.shape[3]
    tf = min(tf, f)
    return pl.pallas_call(
        functools.partial(_ffn_kernel, tm=min(tm, m)),
        out_shape=jax.ShapeDtypeStruct((n_exp, m, d), F32),
        grid=(n_exp, f // tf),
        in_specs=[pl.BlockSpec((1, m, d), lambda e, j: (e, 0, 0)),
                  pl.BlockSpec((1, 1, d, tf), lambda e, j: (layer, e, 0, j)),
                  pl.BlockSpec((1, 1, d, tf), lambda e, j: (layer, e, 0, j)),
                  pl.BlockSpec((1, 1, tf, d), lambda e, j: (layer, e, j, 0)),
                  pl.BlockSpec((1, m, 1), lambda e, j: (e, 0, 0))],
        out_specs=pl.BlockSpec((1, m, d), lambda e, j: (e, 0, 0)),
        compiler_params=_params("arbitrary", "arbitrary"),
        name="moe_ffn",
    )(xs, w_gate, w_up, w_down, gate_col)


def _rows_to_tiles(tile_ref, rows):
    for k in range(tile_ref.shape[1]):
        tile_ref[:, k, :] = rows[:, k * LANES:(k + 1) * LANES]


def _tiles_to_rows(tile_ref):
    return jnp.concatenate([tile_ref[:, k, :] for k in range(tile_ref.shape[1])], axis=1)


COMBINE_UNROLL = 8


def _combine_kernel(idx_ref, ys_ref, o_ref, stage_ref, *, n_groups, slots):
    g = pl.program_id(0)
    e = pl.program_id(1)
    base = (e * n_groups + g) * slots

    @pl.when(e == 0)
    def _():
        o_ref[...] = jnp.zeros_like(o_ref)

    _rows_to_tiles(stage_ref, ys_ref[0, 0])

    def body(j, carry):
        s0 = pl.multiple_of(j * COMBINE_UNROLL, COMBINE_UNROLL)
        toks = [idx_ref[base + s0 + i] for i in range(COMBINE_UNROLL)]
        sums = [o_ref[0, t] + stage_ref[s0 + i] for i, t in enumerate(toks)]
        for t, v in zip(toks, sums):
            o_ref[0, t] = v
        return carry

    lax.fori_loop(0, slots // COMBINE_UNROLL, body, 0)


def moe_combine(idx_flat, ys, tg):
    n_exp, n_groups, slots, d = ys.shape
    c = d // LANES
    return pl.pallas_call(
        functools.partial(_combine_kernel, n_groups=n_groups, slots=slots),
        out_shape=jax.ShapeDtypeStruct((n_groups, tg, c, LANES), F32),
        grid_spec=pltpu.PrefetchScalarGridSpec(
            num_scalar_prefetch=1,
            grid=(n_groups, n_exp),
            in_specs=[pl.BlockSpec((1, 1, slots, d), lambda g, e, idx: (e, g, 0, 0))],
            out_specs=pl.BlockSpec((1, tg, c, LANES), lambda g, e, idx: (g, 0, 0, 0)),
            scratch_shapes=[pltpu.VMEM((slots, c, LANES), F32)],
        ),
        compiler_params=_params("arbitrary", "arbitrary"),
        name="moe_combine",
    )(idx_flat, ys)


def _moe_res_kernel(x_ref, y_ref, g_ref, nw_ref, sh_ref, sc_ref, x2_ref, h_ref):
    x2 = x_ref[...] + g_ref[0] * _tiles_to_rows(y_ref)
    x2_ref[...] = x2
    h_ref[...] = (_rms(x2, nw_ref[...]) * (1.0 + sc_ref[0]) + sh_ref[0]).astype(h_ref.dtype)


def moe_residual(x, y, gate, nw, shift, scale, tg, tm=512):
    m, d = x.shape
    per = tg // tm
    vec = pl.BlockSpec((1, 1, d), lambda i: (i // per, 0, 0))
    row = pl.BlockSpec((tm, d), lambda i: (i, 0))
    tiles = pl.BlockSpec((tm,) + y.shape[1:], lambda i: (i, 0, 0))
    return pl.pallas_call(
        _moe_res_kernel,
        out_shape=[jax.ShapeDtypeStruct((m, d), F32), jax.ShapeDtypeStruct((m, d), BF16)],
        grid=(m // tm,),
        in_specs=[row, tiles, vec, pl.BlockSpec((1, d), lambda i: (0, 0)), vec, vec],
        out_specs=[row, row],
        compiler_params=_params("arbitrary"),
        name="moe_residual",
    )(x, y, gate, nw.reshape(1, d), shift, scale)


def _final_kernel(x_ref, y_ref, g_ref, nw_ref, o_ref):
    o_ref[...] = _rms(x_ref[...] + g_ref[0] * _tiles_to_rows(y_ref), nw_ref[...])


def final_norm(x, y, gate, nw, tg, row0, rows, tm=512):
    d = x.shape[1]
    per = tg // tm
    off = row0 // tm
    row = pl.BlockSpec((tm, d), lambda i: (off + i, 0))
    return pl.pallas_call(
        _final_kernel,
        out_shape=jax.ShapeDtypeStruct((rows, d), F32),
        grid=(rows // tm,),
        in_specs=[row, pl.BlockSpec((tm,) + y.shape[1:], lambda i: (off + i, 0, 0)),
                  pl.BlockSpec((1, 1, d), lambda i: ((off + i) // per, 0, 0)),
                  pl.BlockSpec((1, d), lambda i: (0, 0))],
        out_specs=pl.BlockSpec((tm, d), lambda i: (i, 0)),
        compiler_params=_params("arbitrary"),
        name="final_norm",
    )(x, y, gate, nw.reshape(1, d))


CONV_HALO = 16


def _conv_kernel(prev_ref, cur_ref, next_ref, w_ref, b_ref, o_ref, ext_ref, *, tm, seq_a, n_a, seq_b):
    i = pl.program_id(0)
    row0 = i * tm
    seq = jnp.where(row0 < n_a, seq_a, seq_b)
    off = jnp.where(row0 < n_a, row0, row0 - n_a)
    first = lax.rem(off, seq) == 0
    last = lax.rem(off + tm, seq) == 0
    half = D_CONV // 2
    pad = SUBLANES
    ext_ref[0:pad] = jnp.where(first, 0.0, prev_ref[...].astype(F32)[CONV_HALO - pad:])
    ext_ref[pad:pad + tm] = cur_ref[...].astype(F32)
    ext_ref[pad + tm:] = jnp.where(last, 0.0, next_ref[...].astype(F32)[:pad])
    acc = b_ref[...] + w_ref[0:1, :] * ext_ref[pad - half:pad - half + tm]
    for k in range(1, D_CONV):
        acc = acc + w_ref[k:k + 1, :] * ext_ref[pad - half + k:pad - half + k + tm]
    o_ref[...] = _silu(acc).astype(o_ref.dtype)


def ssd_conv(xbc, w, b, seq_a, n_a, seq_b, tm=256):
    m, c = xbc.shape
    hb = tm // CONV_HALO
    n_halo = m // CONV_HALO
    return pl.pallas_call(
        functools.partial(_conv_kernel, tm=tm, seq_a=seq_a, n_a=n_a, seq_b=seq_b),
        out_shape=jax.ShapeDtypeStruct((m, c), BF16),
        grid=(m // tm,),
        in_specs=[pl.BlockSpec((CONV_HALO, c), lambda i: (jnp.maximum(i * hb - 1, 0), 0)),
                  pl.BlockSpec((tm, c), lambda i: (i, 0)),
                  pl.BlockSpec((CONV_HALO, c), lambda i: (jnp.minimum((i + 1) * hb, n_halo - 1), 0)),
                  pl.BlockSpec((D_CONV, c), lambda i: (0, 0)),
                  pl.BlockSpec((1, c), lambda i: (0, 0))],
        out_specs=pl.BlockSpec((tm, c), lambda i: (i, 0)),
        scratch_shapes=[pltpu.VMEM((tm + 2 * SUBLANES, c), F32)],
        compiler_params=_params("arbitrary"),
        name="ssd_conv",
    )(xbc, xbc, xbc, w, b.reshape(1, c))


def _softplus(x):
    return jnp.maximum(x, 0.0) + jnp.log(1.0 + jnp.exp(-jnp.abs(x)))


DT_REPLICAS = 3


def _split3(v, lane, n_heads):
    hi = v.astype(BF16)
    r1 = v - hi.astype(F32)
    mid = r1.astype(BF16)
    lo = (r1 - mid.astype(F32)).astype(BF16)
    return jnp.where(lane < n_heads, hi, jnp.where(lane < 2 * n_heads, mid, lo))


def _scan_kernel(rb_ref, dir_ref, first_ref, last_ref, zero_ref, s0i_ref, soi_ref,
                 xbc_ref, dt_ref, dtb_ref, alog_ref, e64_ref, e128_ref, s0_ref,
                 y_ref, sout_ref, st_ref, *, n_heads):
    i = pl.program_id(0)
    q = xbc_ref.shape[0]
    p, n, g_n = SSD_HEADDIM, D_STATE, N_GROUPS
    hpg = n_heads // g_n
    d_inner = n_heads * p
    fwd = dir_ref[i] == 0

    @pl.when((first_ref[i] == 1) & (zero_ref[i] == 1))
    def _():
        st_ref[...] = jnp.zeros_like(st_ref)

    @pl.when((first_ref[i] == 1) & (zero_ref[i] == 0))
    def _():
        st_ref[...] = s0_ref[0, 0]

    dt = _softplus(dt_ref[...] + dtb_ref[0])
    da = dt * (-jnp.exp(alog_ref[0]))
    li = lax.broadcasted_iota(jnp.int32, (q, q), 0)
    si = lax.broadcasted_iota(jnp.int32, (q, q), 1)
    tri = (li - si) * jnp.where(fwd, 1, -1) >= 0
    acs = jnp.dot(jnp.where(tri, 1.0, 0.0), da, preferred_element_type=F32, precision=HIGHEST)
    acs_t = acs.T

    lane = lax.broadcasted_iota(jnp.int32, (q, LANES), 1)
    f_dt = _split3(dt, lane, n_heads)
    f_acs = _split3(acs, lane, n_heads)
    dt_all = jnp.dot(f_dt, e64_ref[...], preferred_element_type=F32)
    acs_all = jnp.dot(f_acs, e64_ref[...], preferred_element_type=F32)
    acs_end = jnp.where(fwd, acs_all[q - 1:q, :], acs_all[0:1, :])
    e_in = jnp.exp(acs_all)
    dec = jnp.exp(acs_end)
    xdt = xbc_ref[:, :d_inner].astype(F32) * dt_all
    xdt_b = xdt.astype(BF16)
    xd_b = (xdt * jnp.exp(acs_end - acs_all)).astype(BF16)
    first_head = lax.broadcasted_iota(jnp.int32, (q, 2 * p), 1) < p

    for g in range(g_n):
        gcols = slice(g * hpg * p, (g + 1) * hpg * p)
        bm = xbc_ref[:, d_inner + g * n:d_inner + (g + 1) * n]
        cm = xbc_ref[:, d_inner + g_n * n + g * n:d_inner + g_n * n + (g + 1) * n]
        cb = lax.dot_general(cm, bm, NT_DIMS, preferred_element_type=F32)
        st = st_ref[g]
        y_off = jnp.dot(cm, st.astype(BF16), preferred_element_type=F32) * e_in[:, gcols]
        a_g = jnp.dot(f_acs, e128_ref[:, g * hpg * q:(g + 1) * hpg * q],
                      preferred_element_type=F32)
        for e2 in range(hpg // 2):
            ms = []
            for e in (2 * e2, 2 * e2 + 1):
                h = g * hpg + e
                seg = a_g[:, e * q:(e + 1) * q] - acs_t[h:h + 1, :]
                ms.append((jnp.where(tri, jnp.exp(seg), 0.0) * cb).astype(BF16))
            pc = slice((g * hpg + 2 * e2) * p, (g * hpg + 2 * e2 + 2) * p)
            yd = jnp.dot(jnp.concatenate(ms, axis=0), xdt_b[:, pc], preferred_element_type=F32)
            y = jnp.where(first_head, yd[:q], yd[q:]) + y_off[:, 2 * e2 * p:(2 * e2 + 2) * p]
            y_ref[0, :, pc] = y.astype(y_ref.dtype)
        bm_t = bm.astype(F32).T.astype(BF16)
        s_new = jnp.dot(bm_t, xd_b[:, gcols], preferred_element_type=F32)
        st_ref[g] = st * dec[:, gcols] + s_new

    @pl.when(last_ref[i] == 1)
    def _():
        sout_ref[0, 0] = st_ref[...]


def _expansion(n_heads, width):
    row_head = jnp.arange(LANES) % n_heads
    row_ok = jnp.arange(LANES) < DT_REPLICAS * n_heads
    col_head = jnp.arange(n_heads * width) // width
    return ((row_head[:, None] == col_head[None, :]) & row_ok[:, None]).astype(BF16)


def ssd_scan(xconv, dt2, dtb2, alog2, s0_t, n_heads, seqs):
    m, c = xconv.shape
    q = SSD_CHUNK
    hp = n_heads // N_GROUPS * SSD_HEADDIM
    d_inner = n_heads * SSD_HEADDIM
    assert DT_REPLICAS * n_heads <= LANES and 2 * SSD_HEADDIM == LANES and (n_heads // N_GROUPS) % 2 == 0
    tabs = [[] for _ in range(7)]
    for row0, length, s0_slot, out_slot in seqs:
        nc = length // q
        for d in range(2):
            for ci in range(nc):
                vals = (row0 // q + (ci if d == 0 else nc - 1 - ci), d, int(ci == 0),
                        int(ci == nc - 1), int(s0_slot is None), s0_slot or 0, out_slot)
                for t, v in zip(tabs, vals):
                    t.append(v)
    n_steps = len(tabs[0])
    n_out = max(t[3] for t in seqs) + 1
    tabs = [jnp.asarray(t, jnp.int32) for t in tabs]

    st_block = (1, 1, N_GROUPS, D_STATE, hp)
    const = lambda i, *_: (0, 0)
    return pl.pallas_call(
        functools.partial(_scan_kernel, n_heads=n_heads),
        out_shape=[jax.ShapeDtypeStruct((2, m, d_inner), BF16),
                   jax.ShapeDtypeStruct((2, n_out, N_GROUPS, D_STATE, hp), F32)],
        grid_spec=pltpu.PrefetchScalarGridSpec(
            num_scalar_prefetch=7,
            grid=(n_steps,),
            in_specs=[pl.BlockSpec((q, c), lambda i, rb, dr, *_: (rb[i], 0)),
                      pl.BlockSpec((q, LANES), lambda i, rb, dr, *_: (rb[i], dr[i])),
                      pl.BlockSpec((1, 1, LANES), lambda i, rb, dr, *_: (dr[i], 0, 0)),
                      pl.BlockSpec((1, 1, LANES), lambda i, rb, dr, *_: (dr[i], 0, 0)),
                      pl.BlockSpec((LANES, d_inner), const),
                      pl.BlockSpec((LANES, n_heads * q), const),
                      pl.BlockSpec(st_block, lambda i, rb, dr, f, l, z, s0i, soi: (dr[i], s0i[i], 0, 0, 0))],
            out_specs=[pl.BlockSpec((1, q, d_inner), lambda i, rb, dr, *_: (dr[i], rb[i], 0)),
                       pl.BlockSpec(st_block, lambda i, rb, dr, f, l, z, s0i, soi: (dr[i], soi[i], 0, 0, 0))],
            scratch_shapes=[pltpu.VMEM((N_GROUPS, D_STATE, hp), F32)],
        ),
        compiler_params=_params("arbitrary"),
        name="ssd_scan",
    )(*tabs, xconv, dt2, dtb2, alog2, _expansion(n_heads, SSD_HEADDIM), _expansion(n_heads, q), s0_t)


def _replicate_heads(v, n_heads):
    lead = v.shape[:-1]
    v = v.astype(F32).reshape(lead + (2, 1, n_heads))
    v = jnp.broadcast_to(v, lead + (2, DT_REPLICAS, n_heads)).reshape(lead + (2, DT_REPLICAS * n_heads))
    pad = [(0, 0)] * (len(lead) + 1) + [(0, LANES - DT_REPLICAS * n_heads)]
    return jnp.pad(v, pad).reshape(lead + (2 * LANES,))


def _state_to_t(s, n_heads):
    b = s.shape[0]
    hpg = n_heads // N_GROUPS
    s = s.reshape(b, N_GROUPS, hpg, SSD_HEADDIM, D_STATE)
    return s.transpose(0, 1, 4, 2, 3).reshape(b, N_GROUPS, D_STATE, hpg * SSD_HEADDIM)


def _state_from_t(st, n_heads):
    b = st.shape[0]
    hpg = n_heads // N_GROUPS
    st = st.reshape(b, N_GROUPS, D_STATE, hpg, SSD_HEADDIM)
    return st.transpose(0, 1, 3, 4, 2).reshape(b, n_heads, SSD_HEADDIM, D_STATE)


def _moe(h, logits_t, w_gate, w_up, w_down, layer, dims):
    n_req, seq, n_lat, tg, d = dims
    n_groups = 1 + n_lat
    n_exp = logits_t.shape[0]
    idx_p, gate_p = route(logits_t, seq, 0, n_req)
    idx_s, gate_s = route(logits_t, tg, tg, n_lat)
    slots = idx_s.shape[2]
    idx_p = idx_p + (jnp.arange(n_req, dtype=jnp.int32) * seq)[:, None, None]
    idx = jnp.concatenate([idx_p.transpose(1, 0, 2).reshape(n_exp, 1, slots),
                           idx_s.transpose(1, 0, 2)], axis=1)
    gate = jnp.concatenate([gate_p.transpose(1, 0, 2).reshape(n_exp, 1, slots),
                            gate_s.transpose(1, 0, 2)], axis=1)
    idx_flat = idx.reshape(-1)
    xs = moe_gather(idx_flat, h.reshape(n_groups, tg, d), n_exp, n_groups, slots)
    ys = moe_ffn(xs.reshape(n_exp, n_groups * slots, d), w_gate, w_up, w_down, layer,
                 gate.reshape(n_exp, n_groups * slots, 1))
    out = moe_combine(idx_flat, ys.reshape(n_exp, n_groups, slots, d), tg)
    return out.reshape((n_groups * tg,) + out.shape[2:])


def kernel(x_prompt, x_sample, cache_k_na, cache_v_na, state_ssd_fwd, state_ssd_bwd, c, c_ctx, ada_w, ada_b, norm1_w, norm2_w, final_norm_w, na_qkv_w, na_out_w, na_rpb, ssd_in_w, ssd_conv_w, ssd_conv_b, ssd_dt_bias, ssd_a_log, ssd_d, ssd_norm_w, ssd_out_w, router_w, moe_w_gate, moe_w_up, moe_w_down):
    n_req, seq, d = x_prompt.shape
    n_lat, tg, _ = x_sample.shape
    assert n_req * seq == tg, "token groups must have equal size"
    assert n_lat + 1 <= MOD_ROWS
    depth = ada_w.shape[0]
    n_groups = 1 + n_lat
    m_p = n_req * seq
    dims = (n_req, seq, n_lat, tg, d)

    cvec = jnp.zeros((MOD_ROWS, d), F32).at[0].set(c_ctx).at[1:1 + n_lat].set(c)
    mod = ada_modulation(cvec, ada_w, ada_b)
    mod = mod.reshape(depth, MOD_ROWS, N_MOD, 1, d).transpose(0, 2, 1, 3, 4)

    m_s = n_lat * tg
    x_first, x_second, x_second_row0 = x_prompt.reshape(m_p, d), x_sample.reshape(m_s, d), 0
    h = prenorm(x_first, x_second, norm1_w[0], mod[0, 0], mod[0, 1], tg, BF16)

    new_k, new_v, new_sf, new_sb = [], [], [], []
    y_prompt = y_sample = None
    for layer in range(depth):
        j = layer // 2
        ml = mod[layer]
        if layer % 2 == 0:
            q_p, k_p, v_p = (matmul(h, na_qkv_w, j, F32, rows=m_p, col0=i * d, cols=d)
                             for i in range(3))
            qkv_s = matmul(h, na_qkv_w, j, BF16, row0=m_p)
            new_k.append(k_p.reshape(n_req, seq, N_HEADS, d // N_HEADS))
            new_v.append(v_p.reshape(n_req, seq, N_HEADS, d // N_HEADS))
            att_p = ctx_attention(q_p, k_p, v_p, seq)
            past = cache_k_na.shape[2]
            att_s = na_attention(qkv_s.reshape(n_lat, tg, 3 * d),
                                 cache_k_na[:, j].reshape(n_lat, past, d).astype(BF16),
                                 cache_v_na[:, j].reshape(n_lat, past, d).astype(BF16),
                                 na_bias_table(na_rpb[j], tg // GRID_W))
            x1, hm, logits_t = attn_out(att_p, att_s.reshape(m_s, d), na_out_w, j, x_first, x_second,
                                        x_second_row0, ml[2], norm2_w[layer], ml[3], ml[4],
                                        router_w[layer], tg)
        else:
            n_heads = ssd_d.shape[1]
            d_inner = n_heads * SSD_HEADDIM
            conv_dim = ssd_conv_w.shape[2]
            z = matmul(h, ssd_in_w, j, BF16, cols=d_inner)
            xbc = matmul(h, ssd_in_w, j, BF16, col0=d_inner, cols=conv_dim)
            w_dt = _replicate_heads(ssd_in_w[j, :, d_inner + conv_dim:], n_heads)
            dt2 = matmul(h, w_dt[None], 0, F32)
            xconv = ssd_conv(xbc, ssd_conv_w[j], ssd_conv_b[j], seq, m_p, tg)
            dtb2 = _replicate_heads(ssd_dt_bias[j].reshape(-1), n_heads).reshape(2, 1, LANES)
            alog2 = _replicate_heads(ssd_a_log[j].reshape(-1), n_heads).reshape(2, 1, LANES)
            s0 = jnp.stack([_state_to_t(state_ssd_fwd[:, j], n_heads),
                            _state_to_t(state_ssd_bwd[:, j], n_heads)])
            seqs = [(r * seq, seq, None, r) for r in range(n_req)]
            seqs += [(m_p + b * tg, tg, b, n_req + b) for b in range(n_lat)]
            y2, st = ssd_scan(xconv, dt2, dtb2, alog2, s0, n_heads, seqs)
            new_sf.append(_state_from_t(st[0, :n_req], n_heads))
            new_sb.append(_state_from_t(st[1, :n_req], n_heads))
            d_cols = jnp.repeat(ssd_d[j], SSD_HEADDIM)
            x1, hm, logits_t = ssd_out(y2, xconv, z, d_cols, ssd_norm_w[j], ssd_out_w, j, x_first,
                                       ml[2], norm2_w[layer], ml[3], ml[4], router_w[layer], tg)
        y_moe = _moe(hm, logits_t, moe_w_gate, moe_w_up, moe_w_down, layer, dims)
        if layer + 1 < depth:
            mn = mod[layer + 1]
            x, h = moe_residual(x1, y_moe, ml[5], norm1_w[layer + 1], mn[0], mn[1], tg)
            x_first, x_second, x_second_row0 = x, x, m_p
        else:
            y_prompt = final_norm(x1, y_moe, ml[5], final_norm_w, tg, 0, m_p)
            y_sample = final_norm(x1, y_moe, ml[5], final_norm_w, tg, m_p, m_s)

    return (y_prompt.reshape(n_req, seq, d), y_sample.reshape(n_lat, tg, d),
            jnp.stack(new_k, axis=1), jnp.stack(new_v, axis=1),
            jnp.stack(new_sf, axis=1), jnp.stack(new_sb, axis=1))
```

```python
import functools

import jax
import jax.numpy as jnp
from jax import lax
from jax.experimental import pallas as pl
from jax.experimental.pallas import tpu as pltpu

N_HEADS = 16
GRID_W = 64
WIN_H = 8
WIN_W = 16
SSD_HEADDIM = 64
N_GROUPS = 4
D_STATE = 128
D_CONV = 5
SSD_CHUNK = 128
CAPACITY_FACTOR = 2
N_MOD = 6
RMS_EPS = 1e-6

LANES = 128
SUBLANES = 8
MOD_ROWS = 8
VMEM_LIMIT = 56 * 1024 * 1024
NEG_INF = -1e30

F32 = jnp.float32
BF16 = jnp.bfloat16
HIGHEST = lax.Precision.HIGHEST
NT_DIMS = (((1,), (1,)), ((), ()))


def _params(*sem):
    return pltpu.CompilerParams(dimension_semantics=sem, vmem_limit_bytes=VMEM_LIMIT)


def _silu(x):
    return x / (1.0 + jnp.exp(-x))


def _rms(x, w):
    ms = jnp.mean(x * x, axis=-1, keepdims=True)
    return x * lax.rsqrt(ms + RMS_EPS) * w


def _ada_kernel(c_ref, w_ref, b_ref, o_ref):
    o_ref[0] = jnp.dot(_silu(c_ref[...]), w_ref[0], preferred_element_type=F32,
                       precision=HIGHEST) + b_ref[0]


def ada_modulation(cvec, ada_w, ada_b):
    n_layers, d, n = ada_w.shape
    tn = n // 4
    return pl.pallas_call(
        _ada_kernel,
        out_shape=jax.ShapeDtypeStruct((n_layers, MOD_ROWS, n), F32),
        grid=(n_layers, n // tn),
        in_specs=[pl.BlockSpec((MOD_ROWS, d), lambda l, j: (0, 0)),
                  pl.BlockSpec((1, d, tn), lambda l, j: (l, 0, j)),
                  pl.BlockSpec((1, 1, tn), lambda l, j: (l, 0, j))],
        out_specs=pl.BlockSpec((1, MOD_ROWS, tn), lambda l, j: (l, 0, j)),
        compiler_params=_params("arbitrary", "arbitrary"),
        name="ada_modulation",
    )(cvec, ada_w, ada_b.reshape(n_layers, 1, n))


def _two_source_specs(tm, width, n_first, off_second):
    first = pl.BlockSpec((tm, width), lambda i: (jnp.minimum(i, n_first - 1), 0))
    second = pl.BlockSpec((tm, width), lambda i: (jnp.maximum(i - n_first, 0) + off_second, 0))
    return [first, second]


def _pick(first_ref, second_ref, n_first):
    return jnp.where(pl.program_id(0) < n_first, first_ref[...], second_ref[...])


def _prenorm_kernel(xa_ref, xb_ref, nw_ref, sh_ref, sc_ref, h_ref, *, n_first):
    x = _pick(xa_ref, xb_ref, n_first)
    h = _rms(x, nw_ref[...]) * (1.0 + sc_ref[0]) + sh_ref[0]
    h_ref[...] = h.astype(h_ref.dtype)


def prenorm(xa, xb, nw, shift, scale, tg, out_dtype, tm=512):
    d = xa.shape[1]
    m = xa.shape[0] + xb.shape[0]
    per = tg // tm
    n_first = xa.shape[0] // tm
    vec = pl.BlockSpec((1, 1, d), lambda i: (i // per, 0, 0))
    return pl.pallas_call(
        functools.partial(_prenorm_kernel, n_first=n_first),
        out_shape=jax.ShapeDtypeStruct((m, d), out_dtype),
        grid=(m // tm,),
        in_specs=_two_source_specs(tm, d, n_first, 0) + [pl.BlockSpec((1, d), lambda i: (0, 0)), vec, vec],
        out_specs=pl.BlockSpec((tm, d), lambda i: (i, 0)),
        compiler_params=_params("arbitrary"),
        name="prenorm",
    )(xa, xb, nw.reshape(1, d), shift, scale)


def _mm_kernel(x_ref, w_ref, o_ref, wb_ref, *, silu):
    @pl.when(pl.program_id(1) == 0)
    def _():
        wb_ref[...] = w_ref[0].astype(BF16)

    y = jnp.dot(x_ref[...], wb_ref[...], preferred_element_type=F32)
    o_ref[...] = (_silu(y) if silu else y).astype(o_ref.dtype)


def matmul(x, w, layer, out_dtype, row0=0, rows=None, col0=0, cols=None, silu=False, tm=1024,
           tn=1024):
    k = x.shape[1]
    rows = x.shape[0] - row0 if rows is None else rows
    cols = w.shape[2] - col0 if cols is None else cols
    tn = max(t for t in range(LANES, min(tn, cols) + 1, LANES) if cols % t == 0 and col0 % t == 0)
    tm = min(tm, rows)
    r_off, c_off = row0 // tm, col0 // tn
    return pl.pallas_call(
        functools.partial(_mm_kernel, silu=silu),
        out_shape=jax.ShapeDtypeStruct((rows, cols), out_dtype),
        grid=(cols // tn, rows // tm),
        in_specs=[pl.BlockSpec((tm, k), lambda j, i: (r_off + i, 0)),
                  pl.BlockSpec((1, k, tn), lambda j, i: (layer, 0, c_off + j))],
        out_specs=pl.BlockSpec((tm, tn), lambda j, i: (i, j)),
        scratch_shapes=[pltpu.VMEM((k, tn), BF16)],
        compiler_params=_params("arbitrary", "arbitrary"),
        name="matmul",
    )(x, w)


LOG2E = 1.4426950408889634


def _softmax_pv(s_list, v_list):
    s = jnp.concatenate(s_list, axis=-1) if len(s_list) > 1 else s_list[0]
    p = jnp.exp2(s - s.max(axis=-1, keepdims=True))
    den = p.sum(axis=-1, keepdims=True)
    p = p.astype(BF16)
    acc = None
    col = 0
    for v in v_list:
        pv = jnp.dot(p[:, col:col + v.shape[0]], v, preferred_element_type=F32)
        acc = pv if acc is None else acc + pv
        col += v.shape[0]
    return acc / den


def _ctx_attn_kernel(q_ref, k_ref, v_ref, o_ref, *, n_heads):
    dh = q_ref.shape[1] // n_heads
    scale = dh ** -0.5 * LOG2E
    seq = q_ref.shape[0]
    first = lax.broadcasted_iota(jnp.int32, (seq, 2 * dh), 1) < dh
    for j in range(n_heads // 2):
        sl = slice(2 * j * dh, 2 * (j + 1) * dh)
        q2 = q_ref[:, sl].astype(BF16)
        qq = jnp.concatenate([jnp.where(first, q2, 0), jnp.where(first, 0, q2)], axis=0)
        k = k_ref[:, sl].astype(BF16)
        v = v_ref[:, sl].astype(BF16)
        s = lax.dot_general(qq, k, NT_DIMS, preferred_element_type=F32) * scale
        o2 = _softmax_pv([s], [v])
        o_ref[:, sl] = jnp.where(first, o2[:seq], o2[seq:]).astype(o_ref.dtype)


def ctx_attention(q, k, v, seq):
    m, d = q.shape
    blk = pl.BlockSpec((seq, d), lambda r: (r, 0))
    return pl.pallas_call(
        functools.partial(_ctx_attn_kernel, n_heads=N_HEADS),
        out_shape=jax.ShapeDtypeStruct((m, d), BF16),
        grid=(m // seq,),
        in_specs=[blk, blk, blk],
        out_specs=blk,
        compiler_params=_params("arbitrary"),
        name="ctx_attention",
    )(q, k, v)


def _na_row_start(r, rows, kh):
    return jnp.clip(r - kh // 2, 0, rows - kh)


def na_bias_table(rpb, rows):
    w, kw = GRID_W, WIN_W
    kh = min(WIN_H, rows)
    col = jnp.arange(w)
    cs = jnp.clip(col - kw // 2, 0, w - kw)
    dc = col[None, :] - col[:, None] + (WIN_W - 1)
    valid = (col[None, :] >= cs[:, None]) & (col[None, :] < cs[:, None] + kw)
    pick = (dc[None] == jnp.arange(2 * WIN_W - 1)[:, None, None]).astype(F32)
    c_full = jnp.einsum('hrd,dqk->hrqk', rpb, pick, precision=HIGHEST)
    c_full = jnp.where(valid[None, None], c_full * LOG2E, NEG_INF)
    n_off = WIN_H
    tab = jnp.stack([jnp.concatenate([c_full[:, off + j] for j in range(kh)], axis=-1)
                     for off in range(n_off)])
    return tab.reshape(n_off, rpb.shape[0] // 2, 2 * w, kh * w)


def _na_kernel(q_ref, k_ref, v_ref, kc_ref, vc_ref, b_ref, o_ref, *, n_heads, rows, kh, w):
    dh = q_ref.shape[2] // n_heads
    scale = dh ** -0.5 * LOG2E
    r0 = _na_row_start(pl.program_id(1), rows, kh)
    start = pl.multiple_of(r0 * w, w)
    first = lax.broadcasted_iota(jnp.int32, (w, 2 * dh), 1) < dh
    for j in range(n_heads // 2):
        sl = slice(2 * j * dh, 2 * (j + 1) * dh)
        q2 = q_ref[0, :, sl]
        qq = jnp.concatenate([jnp.where(first, q2, 0), jnp.where(first, 0, q2)], axis=0)
        kwin = k_ref[0, pl.ds(start, kh * w), sl]
        vwin = v_ref[0, pl.ds(start, kh * w), sl]
        s_win = lax.dot_general(qq, kwin, NT_DIMS, preferred_element_type=F32) * scale + b_ref[0, j]
        s_ctx = lax.dot_general(qq, kc_ref[0, :, sl], NT_DIMS, preferred_element_type=F32) * scale
        o2 = _softmax_pv([s_win, s_ctx], [vwin, vc_ref[0, :, sl]])
        o_ref[0, :, sl] = jnp.where(first, o2[:w], o2[w:]).astype(o_ref.dtype)


def na_attention(qkv, k_ctx, v_ctx, bias):
    b, t, d3 = qkv.shape
    d = d3 // 3
    w = GRID_W
    rows = t // w
    kh = min(WIN_H, rows)
    n_ctx = k_ctx.shape[1]

    def bias_map(bi, r):
        return (_na_row_start(r, rows, kh) - r + WIN_H - 1, 0, 0, 0)

    return pl.pallas_call(
        functools.partial(_na_kernel, n_heads=N_HEADS, rows=rows, kh=kh, w=w),
        out_shape=jax.ShapeDtypeStruct((b, t, d), BF16),
        grid=(b, rows),
        in_specs=[pl.BlockSpec((1, w, d), lambda bi, r: (bi, r, 0)),
                  pl.BlockSpec((1, t, d), lambda bi, r: (bi, 0, 1)),
                  pl.BlockSpec((1, t, d), lambda bi, r: (bi, 0, 2)),
                  pl.BlockSpec((1, n_ctx, d), lambda bi, r: (bi, 0, 0)),
                  pl.BlockSpec((1, n_ctx, d), lambda bi, r: (bi, 0, 0)),
                  pl.BlockSpec((1, N_HEADS // 2, 2 * w, kh * w), bias_map)],
        out_specs=pl.BlockSpec((1, w, d), lambda bi, r: (bi, r, 0)),
        compiler_params=_params("arbitrary", "arbitrary"),
        name="na_attention",
    )(qkv, qkv, qkv, k_ctx, v_ctx, bias)


def _residual_router_tail(mix, x, g_ref, nw_ref, sh_ref, sc_ref, rw_ref, x1_ref, h_ref, lg_ref):
    x1 = x + g_ref[0] * mix
    x1_ref[...] = x1
    h = _rms(x1, nw_ref[...]) * (1.0 + sc_ref[0]) + sh_ref[0]
    h_ref[...] = h
    lg_ref[...] = lax.dot_general(rw_ref[...], h, NT_DIMS, preferred_element_type=F32,
                                  precision=HIGHEST)


def _attn_out_kernel(aa_ref, ab_ref, w_ref, xa_ref, xb_ref, g_ref, nw_ref, sh_ref, sc_ref, rw_ref,
                     x1_ref, h_ref, lg_ref, wb_ref, *, n_first):
    @pl.when(pl.program_id(0) == 0)
    def _():
        wb_ref[...] = w_ref[0].astype(BF16)

    mix = jnp.dot(_pick(aa_ref, ab_ref, n_first), wb_ref[...], preferred_element_type=F32)
    _residual_router_tail(mix, _pick(xa_ref, xb_ref, n_first), g_ref, nw_ref, sh_ref, sc_ref,
                          rw_ref, x1_ref, h_ref, lg_ref)


def _ssd_out_kernel(yf_ref, yb_ref, zs_ref, snw_ref, w_ref, x_ref, g_ref, nw_ref,
                    sh_ref, sc_ref, rw_ref, x1_ref, h_ref, lg_ref, wb_ref):
    @pl.when(pl.program_id(0) == 0)
    def _():
        wb_ref[...] = w_ref[0].astype(BF16)

    y = yf_ref[0].astype(F32) + yb_ref[0].astype(F32)
    y = _rms(y * zs_ref[...].astype(F32), snw_ref[...])
    mix = jnp.dot(y.astype(BF16), wb_ref[...], preferred_element_type=F32)
    _residual_router_tail(mix, x_ref[...], g_ref, nw_ref, sh_ref, sc_ref, rw_ref, x1_ref, h_ref,
                          lg_ref)


def _tail_specs(d, n_exp, tm, per):
    vec = pl.BlockSpec((1, 1, d), lambda i: (i // per, 0, 0))
    row = pl.BlockSpec((tm, d), lambda i: (i, 0))
    in_specs = [vec, pl.BlockSpec((1, d), lambda i: (0, 0)), vec, vec,
                pl.BlockSpec((n_exp, d), lambda i: (0, 0))]
    out_specs = [row, row, pl.BlockSpec((n_exp, tm), lambda i: (0, i))]
    return in_specs, out_specs


def _tail_out_shape(m, d, n_exp):
    return [jax.ShapeDtypeStruct((m, d), F32), jax.ShapeDtypeStruct((m, d), F32),
            jax.ShapeDtypeStruct((n_exp, m), F32)]


def attn_out(a_first, a_second, w, layer, x_first, x_second, x_second_row0, gate, nw, shift, scale,
             router_w, tg, tm=512):
    k = a_first.shape[1]
    d = w.shape[2]
    n_first = a_first.shape[0] // tm
    m = a_first.shape[0] + a_second.shape[0]
    n_exp = router_w.shape[1]
    tail_in, tail_out = _tail_specs(d, n_exp, tm, tg // tm)
    return pl.pallas_call(
        functools.partial(_attn_out_kernel, n_first=n_first),
        out_shape=_tail_out_shape(m, d, n_exp),
        grid=(m // tm,),
        in_specs=_two_source_specs(tm, k, n_first, 0)
        + [pl.BlockSpec((1, k, d), lambda i: (layer, 0, 0))]
        + _two_source_specs(tm, d, n_first, x_second_row0 // tm) + tail_in,
        out_specs=tail_out,
        scratch_shapes=[pltpu.VMEM((k, d), BF16)],
        compiler_params=_params("arbitrary"),
        name="attn_out",
    )(a_first, a_second, w, x_first, x_second, gate, nw.reshape(1, d), shift, scale, router_w.T)


def ssd_out(y2, zs, snw, w, layer, x, gate, nw, shift, scale, router_w, tg, tm=512):
    _, k, d = w.shape
    m = x.shape[0]
    n_exp = router_w.shape[1]
    tail_in, tail_out = _tail_specs(d, n_exp, tm, tg // tm)
    return pl.pallas_call(
        _ssd_out_kernel,
        out_shape=_tail_out_shape(m, d, n_exp),
        grid=(m // tm,),
        in_specs=[pl.BlockSpec((1, tm, k), lambda i: (0, i, 0)),
                  pl.BlockSpec((1, tm, k), lambda i: (1, i, 0)),
                  pl.BlockSpec((tm, k), lambda i: (i, 0)),
                  pl.BlockSpec((1, k), lambda i: (0, 0)),
                  pl.BlockSpec((1, k, d), lambda i: (layer, 0, 0)),
                  pl.BlockSpec((tm, d), lambda i: (i, 0))] + tail_in,
        out_specs=tail_out,
        scratch_shapes=[pltpu.VMEM((k, d), BF16)],
        compiler_params=_params("arbitrary"),
        name="ssd_out",
    )(y2, y2, zs, snw.reshape(1, k), w, x, gate, nw.reshape(1, d), shift, scale, router_w.T)


def _excl_prefix_lanes(m01):
    e, t = m01.shape
    r = lax.broadcasted_iota(jnp.int32, (LANES, LANES), 0)
    c = lax.broadcasted_iota(jnp.int32, (LANES, LANES), 1)
    upper = jnp.where(r < c, 1.0, 0.0).astype(BF16)
    outs = []
    carry = jnp.zeros((e, 1), F32)
    for j in range(t // LANES):
        blk = m01[:, j * LANES:(j + 1) * LANES]
        outs.append(jnp.dot(blk.astype(BF16), upper, preferred_element_type=F32) + carry)
        carry = carry + blk.sum(axis=1, keepdims=True)
    return jnp.concatenate(outs, axis=1)


def _route_kernel(lg_ref, idx_ref, gate_ref, aff_ref, pos_ref, *, cap, set_len, tchunk):
    n_exp, t = lg_ref.shape
    n_sets = t // set_len
    lg = lg_ref[...]
    ex = jnp.exp(lg - lg.max(axis=0, keepdims=True))
    aff = ex / ex.sum(axis=0, keepdims=True)
    bits = pltpu.bitcast(aff, jnp.int32)

    def set_slices(x):
        return [x[:, s * set_len:(s + 1) * set_len] for s in range(n_sets)]

    def per_set_sum(m01):
        if n_sets == 1:
            return m01.sum(axis=1, keepdims=True)
        return jnp.concatenate([jnp.broadcast_to(blk.sum(axis=1, keepdims=True), (n_exp, set_len))
                                for blk in set_slices(m01)], axis=1)

    def per_set_prefix(m01):
        return jnp.concatenate([_excl_prefix_lanes(blk) for blk in set_slices(m01)], axis=1)

    def search(i, cur):
        cand = cur | jnp.left_shift(jnp.int32(1), 30 - i)
        cnt = per_set_sum(jnp.where(bits >= cand, 1.0, 0.0))
        return jnp.where(cnt >= cap, cand, cur)

    thr = lax.fori_loop(0, 31, search, jnp.zeros((n_exp, 1 if n_sets == 1 else t), jnp.int32))
    gt = bits > thr
    eq = bits == thr
    need = cap - per_set_sum(jnp.where(gt, 1.0, 0.0))
    eq_rank = per_set_prefix(jnp.where(eq, 1.0, 0.0))
    sel = gt | (eq & (eq_rank < need))
    pos = per_set_prefix(jnp.where(sel, 1.0, 0.0))
    aff_ref[...] = aff
    pos_ref[...] = jnp.where(sel, pos, -1.0).astype(jnp.int32)

    tok = lax.broadcasted_iota(jnp.int32, (1, tchunk), 1)
    slot = lax.broadcasted_iota(jnp.int32, (cap, tchunk), 0)
    feat_rows = 2 * SUBLANES
    zero_rows = jnp.zeros((feat_rows - 5, tchunk), F32)

    def per_expert(e, carry):
        for s in range(n_sets):
            res = jnp.zeros((feat_rows, cap), F32)
            for j in range(set_len // tchunk):
                cols = pl.ds(s * set_len + j * tchunk, tchunk)
                a = aff_ref[pl.ds(e, 1), cols]
                a_hi = a.astype(BF16).astype(F32)
                a_mid = (a - a_hi).astype(BF16).astype(F32)
                a_lo = a - a_hi - a_mid
                tj = tok + j * tchunk
                feats = jnp.concatenate(
                    [jnp.right_shift(tj, 6).astype(F32), jnp.bitwise_and(tj, 63).astype(F32),
                     a_hi, a_mid, a_lo, zero_rows], axis=0).astype(BF16)
                onehot = jnp.where(pos_ref[pl.ds(e, 1), cols] == slot, 1.0, 0.0).astype(BF16)
                res = res + lax.dot_general(feats, onehot, NT_DIMS, preferred_element_type=F32)
            idx_ref[s, pl.ds(e, 1), :] = (res[0:1] * 64.0 + res[1:2]).astype(jnp.int32)
            gate_ref[s, pl.ds(e, 1), :] = res[2:3] + res[3:4] + res[4:5]
        return carry

    lax.fori_loop(0, n_exp, per_expert, 0)


def route(logits_t, set_len, col0, n_sets, sets_per_step):
    n_exp = logits_t.shape[0]
    cap = CAPACITY_FACTOR * set_len // n_exp
    width = sets_per_step * set_len
    blk0 = col0 // width
    out_spec = pl.BlockSpec((sets_per_step, n_exp, cap), lambda s: (s, 0, 0))
    return pl.pallas_call(
        functools.partial(_route_kernel, cap=cap, set_len=set_len, tchunk=min(set_len, 1024)),
        out_shape=[jax.ShapeDtypeStruct((n_sets, n_exp, cap), jnp.int32),
                   jax.ShapeDtypeStruct((n_sets, n_exp, cap), F32)],
        grid=(n_sets // sets_per_step,),
        in_specs=[pl.BlockSpec((n_exp, width), lambda s: (0, blk0 + s))],
        out_specs=[out_spec, out_spec],
        scratch_shapes=[pltpu.VMEM((n_exp, width), F32), pltpu.VMEM((n_exp, width), jnp.int32)],
        compiler_params=_params("arbitrary"),
        name="route",
    )(logits_t)


GATHER_ROWS = 16


def _gather_kernel(idx_ref, h_ref, o_ref, *, n_groups, slots):
    g = pl.program_id(0)
    e = pl.program_id(1)
    base = (e * n_groups + g) * slots

    def body(j, carry):
        s0 = pl.multiple_of(j * GATHER_ROWS, GATHER_ROWS)
        rows = [h_ref[0, pl.ds(idx_ref[base + s0 + i], 1), :] for i in range(GATHER_ROWS)]
        o_ref[0, 0, pl.ds(s0, GATHER_ROWS), :] = jnp.concatenate(rows, axis=0).astype(o_ref.dtype)
        return carry

    lax.fori_loop(0, slots // GATHER_ROWS, body, 0)


def moe_gather(idx_flat, h, n_exp, n_groups, slots):
    _, tg, d = h.shape
    return pl.pallas_call(
        functools.partial(_gather_kernel, n_groups=n_groups, slots=slots),
        out_shape=jax.ShapeDtypeStruct((n_exp, n_groups, slots, d), BF16),
        grid_spec=pltpu.PrefetchScalarGridSpec(
            num_scalar_prefetch=1,
            grid=(n_groups, n_exp),
            in_specs=[pl.BlockSpec((1, tg, d), lambda g, e, idx: (g, 0, 0))],
            out_specs=pl.BlockSpec((1, 1, slots, d), lambda g, e, idx: (e, g, 0, 0)),
        ),
        compiler_params=_params("arbitrary", "arbitrary"),
        name="moe_gather",
    )(idx_flat, h)


def _ffn_kernel(xs_ref, wg_ref, wu_ref, wd_ref, gt_ref, o_ref, *, tm):
    f = pl.program_id(1)
    wg = wg_ref[0, 0].astype(BF16)
    wu = wu_ref[0, 0].astype(BF16)
    wd = wd_ref[0, 0].astype(BF16)

    @pl.when(f == 0)
    def _():
        o_ref[...] = jnp.zeros_like(o_ref)

    for i in range(xs_ref.shape[1] // tm):
        rows = slice(i * tm, (i + 1) * tm)
        x = xs_ref[0, rows, :]
        hid = _silu(jnp.dot(x, wg, preferred_element_type=F32)) * jnp.dot(
            x, wu, preferred_element_type=F32)
        o_ref[0, rows, :] += jnp.dot(hid.astype(BF16), wd, preferred_element_type=F32)

    @pl.when(f == pl.num_programs(1) - 1)
    def _():
        for i in range(xs_ref.shape[1] // LANES):
            g_rows = jnp.broadcast_to(gt_ref[0, :, i * LANES:(i + 1) * LANES], (LANES, LANES)).T
            for k in range(o_ref.shape[2] // LANES):
                blk = (0, slice(i * LANES, (i + 1) * LANES), slice(k * LANES, (k + 1) * LANES))
                o_ref[blk] = o_ref[blk] * g_rows


def moe_ffn(xs, w_gate, w_up, w_down, layer, gate_row, tf=512, tm=512):
    n_exp, m, d = xs.shape
    f = w_gate.shape[3]
    tf = min(tf, f)
    return pl.pallas_call(
        functools.partial(_ffn_kernel, tm=min(tm, m)),
        out_shape=jax.ShapeDtypeStruct((n_exp, m, d), F32),
        grid=(n_exp, f // tf),
        in_specs=[pl.BlockSpec((1, m, d), lambda e, j: (e, 0, 0)),
                  pl.BlockSpec((1, 1, d, tf), lambda e, j: (layer, e, 0, j)),
                  pl.BlockSpec((1, 1, d, tf), lambda e, j: (layer, e, 0, j)),
                  pl.BlockSpec((1, 1, tf, d), lambda e, j: (layer, e, j, 0)),
                  pl.BlockSpec((1, 1, m), lambda e, j: (e, 0, 0))],
        out_specs=pl.BlockSpec((1, m, d), lambda e, j: (e, 0, 0)),
        compiler_params=_params("arbitrary", "arbitrary"),
        name="moe_ffn",
    )(xs, w_gate, w_up, w_down, gate_row)


def _rows_to_tiles(tile_ref, rows):
    for k in range(tile_ref.shape[1]):
        tile_ref[:, k, :] = rows[:, k * LANES:(k + 1) * LANES]


def _tiles_to_rows(tile_ref):
    return jnp.concatenate([tile_ref[:, k, :] for k in range(tile_ref.shape[1])], axis=1)


COMBINE_UNROLL = 8


def _combine_kernel(idx_ref, ys_ref, o_ref, stage_ref, *, n_groups, slots):
    g = pl.program_id(0)
    e = pl.program_id(1)
    base = (e * n_groups + g) * slots

    @pl.when(e == 0)
    def _():
        o_ref[...] = jnp.zeros_like(o_ref)

    _rows_to_tiles(stage_ref, ys_ref[0, 0])

    def body(j, carry):
        s0 = pl.multiple_of(j * COMBINE_UNROLL, COMBINE_UNROLL)
        toks = [idx_ref[base + s0 + i] for i in range(COMBINE_UNROLL)]
        sums = [o_ref[0, t] + stage_ref[s0 + i] for i, t in enumerate(toks)]
        for t, v in zip(toks, sums):
            o_ref[0, t] = v
        return carry

    lax.fori_loop(0, slots // COMBINE_UNROLL, body, 0)


def moe_combine(idx_flat, ys, tg):
    n_exp, n_groups, slots, d = ys.shape
    c = d // LANES
    return pl.pallas_call(
        functools.partial(_combine_kernel, n_groups=n_groups, slots=slots),
        out_shape=jax.ShapeDtypeStruct((n_groups, tg, c, LANES), F32),
        grid_spec=pltpu.PrefetchScalarGridSpec(
            num_scalar_prefetch=1,
            grid=(n_groups, n_exp),
            in_specs=[pl.BlockSpec((1, 1, slots, d), lambda g, e, idx: (e, g, 0, 0))],
            out_specs=pl.BlockSpec((1, tg, c, LANES), lambda g, e, idx: (g, 0, 0, 0)),
            scratch_shapes=[pltpu.VMEM((slots, c, LANES), F32)],
        ),
        compiler_params=_params("arbitrary", "arbitrary"),
        name="moe_combine",
    )(idx_flat, ys)


def _moe_res_kernel(x_ref, y_ref, g_ref, nw_ref, sh_ref, sc_ref, x2_ref, h_ref):
    x2 = x_ref[...] + g_ref[0] * _tiles_to_rows(y_ref)
    x2_ref[...] = x2
    h_ref[...] = (_rms(x2, nw_ref[...]) * (1.0 + sc_ref[0]) + sh_ref[0]).astype(h_ref.dtype)


def moe_residual(x, y, gate, nw, shift, scale, tg, tm=512):
    m, d = x.shape
    per = tg // tm
    vec = pl.BlockSpec((1, 1, d), lambda i: (i // per, 0, 0))
    row = pl.BlockSpec((tm, d), lambda i: (i, 0))
    tiles = pl.BlockSpec((tm,) + y.shape[1:], lambda i: (i, 0, 0))
    return pl.pallas_call(
        _moe_res_kernel,
        out_shape=[jax.ShapeDtypeStruct((m, d), F32), jax.ShapeDtypeStruct((m, d), BF16)],
        grid=(m // tm,),
        in_specs=[row, tiles, vec, pl.BlockSpec((1, d), lambda i: (0, 0)), vec, vec],
        out_specs=[row, row],
        compiler_params=_params("arbitrary"),
        name="moe_residual",
    )(x, y, gate, nw.reshape(1, d), shift, scale)


def _final_kernel(x_ref, y_ref, g_ref, nw_ref, o_ref):
    o_ref[...] = _rms(x_ref[...] + g_ref[0] * _tiles_to_rows(y_ref), nw_ref[...])


def final_norm(x, y, gate, nw, tg, row0, rows, tm=512):
    d = x.shape[1]
    per = tg // tm
    off = row0 // tm
    row = pl.BlockSpec((tm, d), lambda i: (off + i, 0))
    return pl.pallas_call(
        _final_kernel,
        out_shape=jax.ShapeDtypeStruct((rows, d), F32),
        grid=(rows // tm,),
        in_specs=[row, pl.BlockSpec((tm,) + y.shape[1:], lambda i: (off + i, 0, 0)),
                  pl.BlockSpec((1, 1, d), lambda i: ((off + i) // per, 0, 0)),
                  pl.BlockSpec((1, d), lambda i: (0, 0))],
        out_specs=pl.BlockSpec((tm, d), lambda i: (i, 0)),
        compiler_params=_params("arbitrary"),
        name="final_norm",
    )(x, y, gate, nw.reshape(1, d))


CONV_HALO = 16


def _conv_kernel(prev_ref, cur_ref, next_ref, w_ref, b_ref, o_ref, ext_ref, *, tm, seq_a, n_a, seq_b):
    i = pl.program_id(0)
    row0 = i * tm
    seq = jnp.where(row0 < n_a, seq_a, seq_b)
    off = jnp.where(row0 < n_a, row0, row0 - n_a)
    first = lax.rem(off, seq) == 0
    last = lax.rem(off + tm, seq) == 0
    half = D_CONV // 2
    pad = SUBLANES
    ext_ref[0:pad] = jnp.where(first, 0.0, prev_ref[...].astype(F32)[CONV_HALO - pad:])
    ext_ref[pad:pad + tm] = cur_ref[...].astype(F32)
    ext_ref[pad + tm:] = jnp.where(last, 0.0, next_ref[...].astype(F32)[:pad])
    acc = b_ref[...] + w_ref[0:1, :] * ext_ref[pad - half:pad - half + tm]
    for k in range(1, D_CONV):
        acc = acc + w_ref[k:k + 1, :] * ext_ref[pad - half + k:pad - half + k + tm]
    o_ref[...] = _silu(acc).astype(o_ref.dtype)


def ssd_conv(xbc, w, b, seq_a, n_a, seq_b, tm=256):
    m, c = xbc.shape
    hb = tm // CONV_HALO
    n_halo = m // CONV_HALO
    return pl.pallas_call(
        functools.partial(_conv_kernel, tm=tm, seq_a=seq_a, n_a=n_a, seq_b=seq_b),
        out_shape=jax.ShapeDtypeStruct((m, c), BF16),
        grid=(m // tm,),
        in_specs=[pl.BlockSpec((CONV_HALO, c), lambda i: (jnp.maximum(i * hb - 1, 0), 0)),
                  pl.BlockSpec((tm, c), lambda i: (i, 0)),
                  pl.BlockSpec((CONV_HALO, c), lambda i: (jnp.minimum((i + 1) * hb, n_halo - 1), 0)),
                  pl.BlockSpec((D_CONV, c), lambda i: (0, 0)),
                  pl.BlockSpec((1, c), lambda i: (0, 0))],
        out_specs=pl.BlockSpec((tm, c), lambda i: (i, 0)),
        scratch_shapes=[pltpu.VMEM((tm + 2 * SUBLANES, c), F32)],
        compiler_params=_params("arbitrary"),
        name="ssd_conv",
    )(xbc, xbc, xbc, w, b.reshape(1, c))


def _softplus(x):
    return jnp.maximum(x, 0.0) + jnp.log(1.0 + jnp.exp(-jnp.abs(x)))


DT_REPLICAS = 3


def _split3(v, lane, n_heads):
    hi = v.astype(BF16)
    r1 = v - hi.astype(F32)
    mid = r1.astype(BF16)
    lo = (r1 - mid.astype(F32)).astype(BF16)
    return jnp.where(lane < n_heads, hi, jnp.where(lane < 2 * n_heads, mid, lo))


def _scan_kernel(rb_ref, dir_ref, first_ref, last_ref, zero_ref, s0i_ref, soi_ref,
                 xbc_ref, dt_ref, dtb_ref, alog_ref, dsk_ref, e64_ref, e128_ref, s0_ref,
                 y_ref, sout_ref, st_ref, *, n_heads):
    i = pl.program_id(0)
    q = xbc_ref.shape[0]
    p, n, g_n = SSD_HEADDIM, D_STATE, N_GROUPS
    hpg = n_heads // g_n
    d_inner = n_heads * p
    fwd = dir_ref[i] == 0

    @pl.when((first_ref[i] == 1) & (zero_ref[i] == 1))
    def _():
        st_ref[...] = jnp.zeros_like(st_ref)

    @pl.when((first_ref[i] == 1) & (zero_ref[i] == 0))
    def _():
        for g in range(g_n):
            st_ref[g] = s0_ref[0, 0, g].T

    dt = _softplus(dt_ref[...] + dtb_ref[0])
    da = dt * (-jnp.exp(alog_ref[0]))
    li = lax.broadcasted_iota(jnp.int32, (q, q), 0)
    si = lax.broadcasted_iota(jnp.int32, (q, q), 1)
    tri = (li - si) * jnp.where(fwd, 1, -1) >= 0
    acs = jnp.dot(jnp.where(tri, 1.0, 0.0), da, preferred_element_type=F32, precision=HIGHEST)
    acs_t = acs.T

    lane = lax.broadcasted_iota(jnp.int32, (q, LANES), 1)
    f_dt = _split3(dt, lane, n_heads)
    f_acs = _split3(acs, lane, n_heads)
    dt_all = jnp.dot(f_dt, e64_ref[...], preferred_element_type=F32)
    acs_all = jnp.dot(f_acs, e64_ref[...], preferred_element_type=F32)
    acs_end = jnp.where(fwd, acs_all[q - 1:q, :], acs_all[0:1, :])
    e_in = jnp.exp(acs_all)
    dec = jnp.exp(acs_end)
    x = xbc_ref[:, :d_inner].astype(F32)
    xdt = x * dt_all
    skip = x * jnp.where(fwd, dsk_ref[...], 0.0)
    xdt_b = xdt.astype(BF16)
    xd_b = (xdt * jnp.exp(acs_end - acs_all)).astype(BF16)
    first_head = lax.broadcasted_iota(jnp.int32, (q, 2 * p), 1) < p

    for g in range(g_n):
        gcols = slice(g * hpg * p, (g + 1) * hpg * p)
        bm = xbc_ref[:, d_inner + g * n:d_inner + (g + 1) * n]
        cm = xbc_ref[:, d_inner + g_n * n + g * n:d_inner + g_n * n + (g + 1) * n]
        cb = lax.dot_general(cm, bm, NT_DIMS, preferred_element_type=F32)
        st = st_ref[g]
        y_off = jnp.dot(cm, st.astype(BF16), preferred_element_type=F32) * e_in[:, gcols]
        a_g = jnp.dot(f_acs, e128_ref[:, g * hpg * q:(g + 1) * hpg * q],
                      preferred_element_type=F32)
        for e2 in range(hpg // 2):
            ms = []
            for e in (2 * e2, 2 * e2 + 1):
                h = g * hpg + e
                seg = a_g[:, e * q:(e + 1) * q] - acs_t[h:h + 1, :]
                ms.append((jnp.where(tri, jnp.exp(seg), 0.0) * cb).astype(BF16))
            pc = slice((g * hpg + 2 * e2) * p, (g * hpg + 2 * e2 + 2) * p)
            yd = jnp.dot(jnp.concatenate(ms, axis=0), xdt_b[:, pc], preferred_element_type=F32)
            y = (jnp.where(first_head, yd[:q], yd[q:]) + y_off[:, 2 * e2 * p:(2 * e2 + 2) * p]
                 + skip[:, pc])
            y_ref[0, :, pc] = y.astype(y_ref.dtype)
        bm_t = bm.astype(F32).T.astype(BF16)
        s_new = jnp.dot(bm_t, xd_b[:, gcols], preferred_element_type=F32)
        st_ref[g] = st * dec[:, gcols] + s_new

    @pl.when(last_ref[i] == 1)
    def _():
        for g in range(g_n):
            sout_ref[0, 0, g] = st_ref[g].T


def _expansion(n_heads, width):
    row_head = jnp.arange(LANES) % n_heads
    row_ok = jnp.arange(LANES) < DT_REPLICAS * n_heads
    col_head = jnp.arange(n_heads * width) // width
    return ((row_head[:, None] == col_head[None, :]) & row_ok[:, None]).astype(BF16)


def ssd_scan(xconv, dt2, dtb2, alog2, d_cols, s0_t, n_heads, seqs):
    m, c = xconv.shape
    q = SSD_CHUNK
    hp = n_heads // N_GROUPS * SSD_HEADDIM
    d_inner = n_heads * SSD_HEADDIM
    assert DT_REPLICAS * n_heads <= LANES and 2 * SSD_HEADDIM == LANES and (n_heads // N_GROUPS) % 2 == 0
    tabs = [[] for _ in range(7)]
    for row0, length, s0_slot, out_slot in seqs:
        nc = length // q
        for d in range(2):
            for ci in range(nc):
                vals = (row0 // q + (ci if d == 0 else nc - 1 - ci), d, int(ci == 0),
                        int(ci == nc - 1), int(s0_slot is None), s0_slot or 0, out_slot)
                for t, v in zip(tabs, vals):
                    t.append(v)
    n_steps = len(tabs[0])
    n_out = max(t[3] for t in seqs) + 1
    tabs = [jnp.asarray(t, jnp.int32) for t in tabs]

    st_block = (1, 1, N_GROUPS, hp, D_STATE)
    const = lambda i, *_: (0, 0)
    return pl.pallas_call(
        functools.partial(_scan_kernel, n_heads=n_heads),
        out_shape=[jax.ShapeDtypeStruct((2, m, d_inner), BF16),
                   jax.ShapeDtypeStruct((2, n_out, N_GROUPS, hp, D_STATE), F32)],
        grid_spec=pltpu.PrefetchScalarGridSpec(
            num_scalar_prefetch=7,
            grid=(n_steps,),
            in_specs=[pl.BlockSpec((q, c), lambda i, rb, dr, *_: (rb[i], 0)),
                      pl.BlockSpec((q, LANES), lambda i, rb, dr, *_: (rb[i], dr[i])),
                      pl.BlockSpec((1, 1, LANES), lambda i, rb, dr, *_: (dr[i], 0, 0)),
                      pl.BlockSpec((1, 1, LANES), lambda i, rb, dr, *_: (dr[i], 0, 0)),
                      pl.BlockSpec((1, d_inner), const),
                      pl.BlockSpec((LANES, d_inner), const),
                      pl.BlockSpec((LANES, n_heads * q), const),
                      pl.BlockSpec(st_block, lambda i, rb, dr, f, l, z, s0i, soi: (dr[i], s0i[i], 0, 0, 0))],
            out_specs=[pl.BlockSpec((1, q, d_inner), lambda i, rb, dr, *_: (dr[i], rb[i], 0)),
                       pl.BlockSpec(st_block, lambda i, rb, dr, f, l, z, s0i, soi: (dr[i], soi[i], 0, 0, 0))],
            scratch_shapes=[pltpu.VMEM((N_GROUPS, D_STATE, hp), F32)],
        ),
        compiler_params=_params("arbitrary"),
        name="ssd_scan",
    )(*tabs, xconv, dt2, dtb2, alog2, d_cols.reshape(1, d_inner), _expansion(n_heads, SSD_HEADDIM),
      _expansion(n_heads, q), s0_t)


def _replicate_heads(v, n_heads):
    lead = v.shape[:-1]
    v = v.astype(F32).reshape(lead + (2, 1, n_heads))
    v = jnp.broadcast_to(v, lead + (2, DT_REPLICAS, n_heads)).reshape(lead + (2, DT_REPLICAS * n_heads))
    pad = [(0, 0)] * (len(lead) + 1) + [(0, LANES - DT_REPLICAS * n_heads)]
    return jnp.pad(v, pad).reshape(lead + (2 * LANES,))


def _state_groups(s, n_heads):
    return s.reshape(s.shape[0], N_GROUPS, n_heads // N_GROUPS * SSD_HEADDIM, D_STATE)


def _moe(h, logits_t, w_gate, w_up, w_down, layer, dims):
    n_req, seq, n_lat, tg, d = dims
    n_groups = 1 + n_lat
    n_exp = logits_t.shape[0]
    idx_p, gate_p = route(logits_t, seq, 0, n_req, n_req)
    idx_s, gate_s = route(logits_t, tg, tg, n_lat, 1)
    slots = idx_s.shape[2]
    idx_p = idx_p + (jnp.arange(n_req, dtype=jnp.int32) * seq)[:, None, None]
    idx = jnp.concatenate([idx_p.transpose(1, 0, 2).reshape(n_exp, 1, slots),
                           idx_s.transpose(1, 0, 2)], axis=1)
    gate = jnp.concatenate([gate_p.transpose(1, 0, 2).reshape(n_exp, 1, slots),
                            gate_s.transpose(1, 0, 2)], axis=1)
    idx_flat = idx.reshape(-1)
    xs = moe_gather(idx_flat, h.reshape(n_groups, tg, d), n_exp, n_groups, slots)
    ys = moe_ffn(xs.reshape(n_exp, n_groups * slots, d), w_gate, w_up, w_down, layer,
                 gate.reshape(n_exp, 1, n_groups * slots))
    out = moe_combine(idx_flat, ys.reshape(n_exp, n_groups, slots, d), tg)
    return out.reshape((n_groups * tg,) + out.shape[2:])


def kernel(x_prompt, x_sample, cache_k_na, cache_v_na, state_ssd_fwd, state_ssd_bwd, c, c_ctx, ada_w, ada_b, norm1_w, norm2_w, final_norm_w, na_qkv_w, na_out_w, na_rpb, ssd_in_w, ssd_conv_w, ssd_conv_b, ssd_dt_bias, ssd_a_log, ssd_d, ssd_norm_w, ssd_out_w, router_w, moe_w_gate, moe_w_up, moe_w_down):
    n_req, seq, d = x_prompt.shape
    n_lat, tg, _ = x_sample.shape
    assert n_req * seq == tg, "token groups must have equal size"
    assert n_lat + 1 <= MOD_ROWS
    depth = ada_w.shape[0]
    n_groups = 1 + n_lat
    m_p = n_req * seq
    dims = (n_req, seq, n_lat, tg, d)

    cvec = jnp.zeros((MOD_ROWS, d), F32).at[0].set(c_ctx).at[1:1 + n_lat].set(c)
    mod = ada_modulation(cvec, ada_w, ada_b)
    mod = mod.reshape(depth, MOD_ROWS, N_MOD, 1, d).transpose(0, 2, 1, 3, 4)

    m_s = n_lat * tg
    x_first, x_second, x_second_row0 = x_prompt.reshape(m_p, d), x_sample.reshape(m_s, d), 0
    h = prenorm(x_first, x_second, norm1_w[0], mod[0, 0], mod[0, 1], tg, BF16)

    new_k, new_v, new_sf, new_sb = [], [], [], []
    y_prompt = y_sample = None
    for layer in range(depth):
        j = layer // 2
        ml = mod[layer]
        if layer % 2 == 0:
            q_p, k_p, v_p = (matmul(h, na_qkv_w, j, F32, rows=m_p, col0=i * d, cols=d)
                             for i in range(3))
            qkv_s = matmul(h, na_qkv_w, j, BF16, row0=m_p)
            new_k.append(k_p.reshape(n_req, seq, N_HEADS, d // N_HEADS))
            new_v.append(v_p.reshape(n_req, seq, N_HEADS, d // N_HEADS))
            att_p = ctx_attention(q_p, k_p, v_p, seq)
            past = cache_k_na.shape[2]
            att_s = na_attention(qkv_s.reshape(n_lat, tg, 3 * d),
                                 cache_k_na[:, j].reshape(n_lat, past, d).astype(BF16),
                                 cache_v_na[:, j].reshape(n_lat, past, d).astype(BF16),
                                 na_bias_table(na_rpb[j], tg // GRID_W))
            x1, hm, logits_t = attn_out(att_p, att_s.reshape(m_s, d), na_out_w, j, x_first, x_second,
                                        x_second_row0, ml[2], norm2_w[layer], ml[3], ml[4],
                                        router_w[layer], tg)
        else:
            n_heads = ssd_d.shape[1]
            d_inner = n_heads * SSD_HEADDIM
            conv_dim = ssd_conv_w.shape[2]
            zs = matmul(h, ssd_in_w, j, BF16, cols=d_inner, silu=True)
            xbc = matmul(h, ssd_in_w, j, BF16, col0=d_inner, cols=conv_dim)
            w_dt = _replicate_heads(ssd_in_w[j, :, d_inner + conv_dim:], n_heads)
            dt2 = matmul(h, w_dt[None], 0, F32)
            xconv = ssd_conv(xbc, ssd_conv_w[j], ssd_conv_b[j], seq, m_p, tg)
            dtb2 = _replicate_heads(ssd_dt_bias[j].reshape(-1), n_heads).reshape(2, 1, LANES)
            alog2 = _replicate_heads(ssd_a_log[j].reshape(-1), n_heads).reshape(2, 1, LANES)
            s0 = jnp.stack([_state_groups(state_ssd_fwd[:, j], n_heads),
                            _state_groups(state_ssd_bwd[:, j], n_heads)])
            seqs = [(r * seq, seq, None, r) for r in range(n_req)]
            seqs += [(m_p + b * tg, tg, b, n_req + b) for b in range(n_lat)]
            d_cols = jnp.repeat(ssd_d[j], SSD_HEADDIM)
            y2, st = ssd_scan(xconv, dt2, dtb2, alog2, d_cols, s0, n_heads, seqs)
            new_sf.append(st[0, :n_req].reshape(n_req, n_heads, SSD_HEADDIM, D_STATE))
            new_sb.append(st[1, :n_req].reshape(n_req, n_heads, SSD_HEADDIM, D_STATE))
            x1, hm, logits_t = ssd_out(y2, zs, ssd_norm_w[j], ssd_out_w, j, x_first,
                                       ml[2], norm2_w[layer], ml[3], ml[4], router_w[layer], tg)
        y_moe = _moe(hm, logits_t, moe_w_gate, moe_w_up, moe_w_down, layer, dims)
        if layer + 1 < depth:
            mn = mod[layer + 1]
            x, h = moe_residual(x1, y_moe, ml[5], norm1_w[layer + 1], mn[0], mn[1], tg)
            x_first, x_second, x_second_row0 = x, x, m_p
        else:
            y_prompt = final_norm(x1, y_moe, ml[5], final_norm_w, tg, 0, m_p)
            y_sample = final_norm(x1, y_moe, ml[5], final_norm_w, tg, m_p, m_s)

    return (y_prompt.reshape(n_req, seq, d), y_sample.reshape(n_lat, tg, d),
            jnp.stack(new_k, axis=1), jnp.stack(new_v, axis=1),
            jnp.stack(new_sf, axis=1), jnp.stack(new_sb, axis=1))
```

```python
import functools

import jax
import jax.numpy as jnp
from jax import lax
from jax.experimental import pallas as pl
from jax.experimental.pallas import tpu as pltpu

N_HEADS = 16
GRID_W = 64
WIN_H = 8
WIN_W = 16
SSD_HEADDIM = 64
N_GROUPS = 4
D_STATE = 128
D_CONV = 5
SSD_CHUNK = 128
CAPACITY_FACTOR = 2
N_MOD = 6
RMS_EPS = 1e-6

LANES = 128
SUBLANES = 8
MOD_ROWS = 8
VMEM_LIMIT = 56 * 1024 * 1024
NEG_INF = -1e30

F32 = jnp.float32
BF16 = jnp.bfloat16
HIGHEST = lax.Precision.HIGHEST
NT_DIMS = (((1,), (1,)), ((), ()))


def _params(*sem):
    return pltpu.CompilerParams(dimension_semantics=sem, vmem_limit_bytes=VMEM_LIMIT)


def _silu(x):
    return x / (1.0 + jnp.exp(-x))


def _rms(x, w):
    ms = jnp.mean(x * x, axis=-1, keepdims=True)
    return x * lax.rsqrt(ms + RMS_EPS) * w


def _ada_kernel(c_ref, w_ref, b_ref, o_ref):
    o_ref[0] = jnp.dot(_silu(c_ref[...]), w_ref[0], preferred_element_type=F32,
                       precision=HIGHEST) + b_ref[0]


def ada_modulation(cvec, ada_w, ada_b):
    n_layers, d, n = ada_w.shape
    tn = n // 4
    return pl.pallas_call(
        _ada_kernel,
        out_shape=jax.ShapeDtypeStruct((n_layers, MOD_ROWS, n), F32),
        grid=(n_layers, n // tn),
        in_specs=[pl.BlockSpec((MOD_ROWS, d), lambda l, j: (0, 0)),
                  pl.BlockSpec((1, d, tn), lambda l, j: (l, 0, j)),
                  pl.BlockSpec((1, 1, tn), lambda l, j: (l, 0, j))],
        out_specs=pl.BlockSpec((1, MOD_ROWS, tn), lambda l, j: (l, 0, j)),
        compiler_params=_params("arbitrary", "arbitrary"),
        name="ada_modulation",
    )(cvec, ada_w, ada_b.reshape(n_layers, 1, n))


def _two_source_specs(tm, width, n_first, off_second):
    first = pl.BlockSpec((tm, width), lambda i: (jnp.minimum(i, n_first - 1), 0))
    second = pl.BlockSpec((tm, width), lambda i: (jnp.maximum(i - n_first, 0) + off_second, 0))
    return [first, second]


def _pick(first_ref, second_ref, n_first):
    return jnp.where(pl.program_id(0) < n_first, first_ref[...], second_ref[...])


def _prenorm_kernel(xa_ref, xb_ref, nw_ref, sh_ref, sc_ref, h_ref, *, n_first):
    x = _pick(xa_ref, xb_ref, n_first)
    h = _rms(x, nw_ref[...]) * (1.0 + sc_ref[0]) + sh_ref[0]
    h_ref[...] = h.astype(h_ref.dtype)


def prenorm(xa, xb, nw, shift, scale, tg, out_dtype, tm=512):
    d = xa.shape[1]
    m = xa.shape[0] + xb.shape[0]
    per = tg // tm
    n_first = xa.shape[0] // tm
    vec = pl.BlockSpec((1, 1, d), lambda i: (i // per, 0, 0))
    return pl.pallas_call(
        functools.partial(_prenorm_kernel, n_first=n_first),
        out_shape=jax.ShapeDtypeStruct((m, d), out_dtype),
        grid=(m // tm,),
        in_specs=_two_source_specs(tm, d, n_first, 0) + [pl.BlockSpec((1, d), lambda i: (0, 0)), vec, vec],
        out_specs=pl.BlockSpec((tm, d), lambda i: (i, 0)),
        compiler_params=_params("arbitrary"),
        name="prenorm",
    )(xa, xb, nw.reshape(1, d), shift, scale)


def _mm_kernel(x_ref, w_ref, o_ref, wb_ref, *, silu):
    @pl.when(pl.program_id(1) == 0)
    def _():
        wb_ref[...] = w_ref[0].astype(BF16)

    y = jnp.dot(x_ref[...], wb_ref[...], preferred_element_type=F32)
    o_ref[...] = (_silu(y) if silu else y).astype(o_ref.dtype)


def matmul(x, w, layer, out_dtype, row0=0, rows=None, col0=0, cols=None, silu=False, tm=1024,
           tn=1024):
    k = x.shape[1]
    rows = x.shape[0] - row0 if rows is None else rows
    cols = w.shape[2] - col0 if cols is None else cols
    tn = max(t for t in range(LANES, min(tn, cols) + 1, LANES) if cols % t == 0 and col0 % t == 0)
    tm = min(tm, rows)
    r_off, c_off = row0 // tm, col0 // tn
    return pl.pallas_call(
        functools.partial(_mm_kernel, silu=silu),
        out_shape=jax.ShapeDtypeStruct((rows, cols), out_dtype),
        grid=(cols // tn, rows // tm),
        in_specs=[pl.BlockSpec((tm, k), lambda j, i: (r_off + i, 0)),
                  pl.BlockSpec((1, k, tn), lambda j, i: (layer, 0, c_off + j))],
        out_specs=pl.BlockSpec((tm, tn), lambda j, i: (i, j)),
        scratch_shapes=[pltpu.VMEM((k, tn), BF16)],
        compiler_params=_params("arbitrary", "arbitrary"),
        name="matmul",
    )(x, w)


LOG2E = 1.4426950408889634


def _softmax_pv(s_list, v_list):
    s = jnp.concatenate(s_list, axis=-1) if len(s_list) > 1 else s_list[0]
    p = jnp.exp2(s - s.max(axis=-1, keepdims=True))
    den = p.sum(axis=-1, keepdims=True)
    p = p.astype(BF16)
    acc = None
    col = 0
    for v in v_list:
        pv = jnp.dot(p[:, col:col + v.shape[0]], v, preferred_element_type=F32)
        acc = pv if acc is None else acc + pv
        col += v.shape[0]
    return acc / den


def _ctx_attn_kernel(q_ref, k_ref, v_ref, o_ref, *, n_heads):
    dh = q_ref.shape[1] // n_heads
    scale = dh ** -0.5 * LOG2E
    seq = q_ref.shape[0]
    first = lax.broadcasted_iota(jnp.int32, (seq, 2 * dh), 1) < dh
    for j in range(n_heads // 2):
        sl = slice(2 * j * dh, 2 * (j + 1) * dh)
        q2 = q_ref[:, sl].astype(BF16)
        qq = jnp.concatenate([jnp.where(first, q2, 0), jnp.where(first, 0, q2)], axis=0)
        k = k_ref[:, sl].astype(BF16)
        v = v_ref[:, sl].astype(BF16)
        s = lax.dot_general(qq, k, NT_DIMS, preferred_element_type=F32) * scale
        o2 = _softmax_pv([s], [v])
        o_ref[:, sl] = jnp.where(first, o2[:seq], o2[seq:]).astype(o_ref.dtype)


def ctx_attention(q, k, v, seq):
    m, d = q.shape
    blk = pl.BlockSpec((seq, d), lambda r: (r, 0))
    return pl.pallas_call(
        functools.partial(_ctx_attn_kernel, n_heads=N_HEADS),
        out_shape=jax.ShapeDtypeStruct((m, d), BF16),
        grid=(m // seq,),
        in_specs=[blk, blk, blk],
        out_specs=blk,
        compiler_params=_params("arbitrary"),
        name="ctx_attention",
    )(q, k, v)


def _na_row_start(r, rows, kh):
    return jnp.clip(r - kh // 2, 0, rows - kh)


def na_bias_table(rpb, rows):
    w, kw = GRID_W, WIN_W
    kh = min(WIN_H, rows)
    col = jnp.arange(w)
    cs = jnp.clip(col - kw // 2, 0, w - kw)
    dc = col[None, :] - col[:, None] + (WIN_W - 1)
    valid = (col[None, :] >= cs[:, None]) & (col[None, :] < cs[:, None] + kw)
    pick = (dc[None] == jnp.arange(2 * WIN_W - 1)[:, None, None]).astype(F32)
    c_full = jnp.einsum('hrd,dqk->hrqk', rpb, pick, precision=HIGHEST)
    c_full = jnp.where(valid[None, None], c_full * LOG2E, NEG_INF)
    n_off = WIN_H
    tab = jnp.stack([jnp.concatenate([c_full[:, off + j] for j in range(kh)], axis=-1)
                     for off in range(n_off)])
    return tab.reshape(n_off, rpb.shape[0] // 2, 2 * w, kh * w)


NA_ROWS_PER_STEP = 2


def _na_kernel(q_ref, k_ref, v_ref, kc_ref, vc_ref, *rest, n_heads, rows, kh, w):
    bias_refs, o_ref = rest[:-1], rest[-1]
    dh = q_ref.shape[2] // n_heads
    scale = dh ** -0.5 * LOG2E
    first = lax.broadcasted_iota(jnp.int32, (w, 2 * dh), 1) < dh
    for i, b_ref in enumerate(bias_refs):
        r0 = _na_row_start(pl.program_id(1) * len(bias_refs) + i, rows, kh)
        start = pl.multiple_of(r0 * w, w)
        qrows = slice(i * w, (i + 1) * w)
        for j in range(n_heads // 2):
            sl = slice(2 * j * dh, 2 * (j + 1) * dh)
            q2 = q_ref[0, qrows, sl]
            qq = jnp.concatenate([jnp.where(first, q2, 0), jnp.where(first, 0, q2)], axis=0)
            kwin = k_ref[0, pl.ds(start, kh * w), sl]
            vwin = v_ref[0, pl.ds(start, kh * w), sl]
            s_win = (lax.dot_general(qq, kwin, NT_DIMS, preferred_element_type=F32) * scale
                     + b_ref[0, j])
            s_ctx = lax.dot_general(qq, kc_ref[0, :, sl], NT_DIMS, preferred_element_type=F32) * scale
            o2 = _softmax_pv([s_win, s_ctx], [vwin, vc_ref[0, :, sl]])
            o_ref[0, qrows, sl] = jnp.where(first, o2[:w], o2[w:]).astype(o_ref.dtype)


def na_attention(qkv, k_ctx, v_ctx, bias):
    b, t, d3 = qkv.shape
    d = d3 // 3
    w = GRID_W
    rows = t // w
    kh = min(WIN_H, rows)
    n_ctx = k_ctx.shape[1]

    rps = NA_ROWS_PER_STEP
    assert rows % rps == 0 and 2 * (d // N_HEADS) == LANES

    def bias_spec(i):
        def bias_map(bi, s):
            r = s * rps + i
            return (_na_row_start(r, rows, kh) - r + WIN_H - 1, 0, 0, 0)
        return pl.BlockSpec((1, N_HEADS // 2, 2 * w, kh * w), bias_map)

    return pl.pallas_call(
        functools.partial(_na_kernel, n_heads=N_HEADS, rows=rows, kh=kh, w=w),
        out_shape=jax.ShapeDtypeStruct((b, t, d), BF16),
        grid=(b, rows // rps),
        in_specs=[pl.BlockSpec((1, rps * w, d), lambda bi, s: (bi, s, 0)),
                  pl.BlockSpec((1, t, d), lambda bi, s: (bi, 0, 1)),
                  pl.BlockSpec((1, t, d), lambda bi, s: (bi, 0, 2)),
                  pl.BlockSpec((1, n_ctx, d), lambda bi, s: (bi, 0, 0)),
                  pl.BlockSpec((1, n_ctx, d), lambda bi, s: (bi, 0, 0))]
        + [bias_spec(i) for i in range(rps)],
        out_specs=pl.BlockSpec((1, rps * w, d), lambda bi, s: (bi, s, 0)),
        compiler_params=_params("arbitrary", "arbitrary"),
        name="na_attention",
    )(qkv, qkv, qkv, k_ctx, v_ctx, *([bias] * rps))


def _residual_router_tail(mix, x, g_ref, nw_ref, sh_ref, sc_ref, rw_ref, x1_ref, h_ref, lg_ref):
    x1 = x + g_ref[0] * mix
    x1_ref[...] = x1
    h = _rms(x1, nw_ref[...]) * (1.0 + sc_ref[0]) + sh_ref[0]
    h_ref[...] = h
    n_exp = rw_ref.shape[0]
    rw = rw_ref[...]
    rw_hi = rw.astype(BF16)
    rw_lo = (rw - rw_hi.astype(F32)).astype(BF16)
    h_hi = h.astype(BF16)
    h_lo = (h - h_hi.astype(F32)).astype(BF16)
    both = lax.dot_general(jnp.concatenate([rw_hi, rw_lo], axis=0), h_hi, NT_DIMS,
                           preferred_element_type=F32)
    lg_ref[...] = (both[:n_exp] + both[n_exp:]
                   + lax.dot_general(rw_hi, h_lo, NT_DIMS, preferred_element_type=F32))


def _attn_out_kernel(aa_ref, ab_ref, w_ref, xa_ref, xb_ref, g_ref, nw_ref, sh_ref, sc_ref, rw_ref,
                     x1_ref, h_ref, lg_ref, wb_ref, *, n_first):
    @pl.when(pl.program_id(0) == 0)
    def _():
        wb_ref[...] = w_ref[0].astype(BF16)

    mix = jnp.dot(_pick(aa_ref, ab_ref, n_first), wb_ref[...], preferred_element_type=F32)
    _residual_router_tail(mix, _pick(xa_ref, xb_ref, n_first), g_ref, nw_ref, sh_ref, sc_ref,
                          rw_ref, x1_ref, h_ref, lg_ref)


def _ssd_out_kernel(yf_ref, yb_ref, zs_ref, snw_ref, w_ref, x_ref, g_ref, nw_ref,
                    sh_ref, sc_ref, rw_ref, x1_ref, h_ref, lg_ref, wb_ref):
    @pl.when(pl.program_id(0) == 0)
    def _():
        wb_ref[...] = w_ref[0].astype(BF16)

    y = yf_ref[0].astype(F32) + yb_ref[0].astype(F32)
    y = _rms(y * zs_ref[...].astype(F32), snw_ref[...])
    mix = jnp.dot(y.astype(BF16), wb_ref[...], preferred_element_type=F32)
    _residual_router_tail(mix, x_ref[...], g_ref, nw_ref, sh_ref, sc_ref, rw_ref, x1_ref, h_ref,
                          lg_ref)


def _tail_specs(d, n_exp, tm, per):
    vec = pl.BlockSpec((1, 1, d), lambda i: (i // per, 0, 0))
    row = pl.BlockSpec((tm, d), lambda i: (i, 0))
    in_specs = [vec, pl.BlockSpec((1, d), lambda i: (0, 0)), vec, vec,
                pl.BlockSpec((n_exp, d), lambda i: (0, 0))]
    out_specs = [row, row, pl.BlockSpec((n_exp, tm), lambda i: (0, i))]
    return in_specs, out_specs


def _tail_out_shape(m, d, n_exp):
    return [jax.ShapeDtypeStruct((m, d), F32), jax.ShapeDtypeStruct((m, d), F32),
            jax.ShapeDtypeStruct((n_exp, m), F32)]


def attn_out(a_first, a_second, w, layer, x_first, x_second, x_second_row0, gate, nw, shift, scale,
             router_w, tg, tm=512):
    k = a_first.shape[1]
    d = w.shape[2]
    n_first = a_first.shape[0] // tm
    m = a_first.shape[0] + a_second.shape[0]
    n_exp = router_w.shape[1]
    tail_in, tail_out = _tail_specs(d, n_exp, tm, tg // tm)
    return pl.pallas_call(
        functools.partial(_attn_out_kernel, n_first=n_first),
        out_shape=_tail_out_shape(m, d, n_exp),
        grid=(m // tm,),
        in_specs=_two_source_specs(tm, k, n_first, 0)
        + [pl.BlockSpec((1, k, d), lambda i: (layer, 0, 0))]
        + _two_source_specs(tm, d, n_first, x_second_row0 // tm) + tail_in,
        out_specs=tail_out,
        scratch_shapes=[pltpu.VMEM((k, d), BF16)],
        compiler_params=_params("arbitrary"),
        name="attn_out",
    )(a_first, a_second, w, x_first, x_second, gate, nw.reshape(1, d), shift, scale, router_w.T)


def ssd_out(y2, zs, snw, w, layer, x, gate, nw, shift, scale, router_w, tg, tm=512):
    _, k, d = w.shape
    m = x.shape[0]
    n_exp = router_w.shape[1]
    tail_in, tail_out = _tail_specs(d, n_exp, tm, tg // tm)
    return pl.pallas_call(
        _ssd_out_kernel,
        out_shape=_tail_out_shape(m, d, n_exp),
        grid=(m // tm,),
        in_specs=[pl.BlockSpec((1, tm, k), lambda i: (0, i, 0)),
                  pl.BlockSpec((1, tm, k), lambda i: (1, i, 0)),
                  pl.BlockSpec((tm, k), lambda i: (i, 0)),
                  pl.BlockSpec((1, k), lambda i: (0, 0)),
                  pl.BlockSpec((1, k, d), lambda i: (layer, 0, 0)),
                  pl.BlockSpec((tm, d), lambda i: (i, 0))] + tail_in,
        out_specs=tail_out,
        scratch_shapes=[pltpu.VMEM((k, d), BF16)],
        compiler_params=_params("arbitrary"),
        name="ssd_out",
    )(y2, y2, zs, snw.reshape(1, k), w, x, gate, nw.reshape(1, d), shift, scale, router_w.T)


def _excl_prefix_lanes(m01):
    e, t = m01.shape
    r = lax.broadcasted_iota(jnp.int32, (LANES, LANES), 0)
    c = lax.broadcasted_iota(jnp.int32, (LANES, LANES), 1)
    upper = jnp.where(r < c, 1.0, 0.0).astype(BF16)
    outs = []
    carry = jnp.zeros((e, 1), F32)
    for j in range(t // LANES):
        blk = m01[:, j * LANES:(j + 1) * LANES]
        outs.append(jnp.dot(blk.astype(BF16), upper, preferred_element_type=F32) + carry)
        carry = carry + blk.sum(axis=1, keepdims=True)
    return jnp.concatenate(outs, axis=1)


def _route_kernel(lg_ref, idx_ref, gate_ref, aff_ref, pos_ref, *, cap, set_len, tchunk):
    n_exp, t = lg_ref.shape
    n_sets = t // set_len
    lg = lg_ref[...]
    ex = jnp.exp(lg - lg.max(axis=0, keepdims=True))
    aff = ex / ex.sum(axis=0, keepdims=True)
    bits = pltpu.bitcast(aff, jnp.int32)

    def set_slices(x):
        return [x[:, s * set_len:(s + 1) * set_len] for s in range(n_sets)]

    def per_set_sum(m01):
        if n_sets == 1:
            return m01.sum(axis=1, keepdims=True)
        return jnp.concatenate([jnp.broadcast_to(blk.sum(axis=1, keepdims=True), (n_exp, set_len))
                                for blk in set_slices(m01)], axis=1)

    def per_set_prefix(m01):
        return jnp.concatenate([_excl_prefix_lanes(blk) for blk in set_slices(m01)], axis=1)

    def search(i, cur):
        cand = cur | jnp.left_shift(jnp.int32(1), 30 - i)
        cnt = per_set_sum(jnp.where(bits >= cand, 1.0, 0.0))
        return jnp.where(cnt >= cap, cand, cur)

    thr = lax.fori_loop(0, 31, search, jnp.zeros((n_exp, 1 if n_sets == 1 else t), jnp.int32))
    gt = bits > thr
    eq = bits == thr
    need = cap - per_set_sum(jnp.where(gt, 1.0, 0.0))
    eq_rank = per_set_prefix(jnp.where(eq, 1.0, 0.0))
    sel = gt | (eq & (eq_rank < need))
    pos = per_set_prefix(jnp.where(sel, 1.0, 0.0))
    aff_ref[...] = aff
    pos_ref[...] = jnp.where(sel, pos, -1.0).astype(jnp.int32)

    tok = lax.broadcasted_iota(jnp.int32, (1, tchunk), 1)
    slot = lax.broadcasted_iota(jnp.int32, (cap, tchunk), 0)
    feat_rows = 2 * SUBLANES
    zero_rows = jnp.zeros((feat_rows - 5, tchunk), F32)

    def per_expert(e, carry):
        for s in range(n_sets):
            res = jnp.zeros((feat_rows, cap), F32)
            for j in range(set_len // tchunk):
                cols = pl.ds(s * set_len + j * tchunk, tchunk)
                a = aff_ref[pl.ds(e, 1), cols]
                a_hi = a.astype(BF16).astype(F32)
                a_mid = (a - a_hi).astype(BF16).astype(F32)
                a_lo = a - a_hi - a_mid
                tj = tok + j * tchunk
                feats = jnp.concatenate(
                    [jnp.right_shift(tj, 6).astype(F32), jnp.bitwise_and(tj, 63).astype(F32),
                     a_hi, a_mid, a_lo, zero_rows], axis=0).astype(BF16)
                onehot = jnp.where(pos_ref[pl.ds(e, 1), cols] == slot, 1.0, 0.0).astype(BF16)
                res = res + lax.dot_general(feats, onehot, NT_DIMS, preferred_element_type=F32)
            idx_ref[s, pl.ds(e, 1), :] = (res[0:1] * 64.0 + res[1:2]).astype(jnp.int32)
            gate_ref[s, pl.ds(e, 1), :] = res[2:3] + res[3:4] + res[4:5]
        return carry

    lax.fori_loop(0, n_exp, per_expert, 0)


def route(logits_t, set_len, col0, n_sets, sets_per_step):
    n_exp = logits_t.shape[0]
    cap = CAPACITY_FACTOR * set_len // n_exp
    width = sets_per_step * set_len
    blk0 = col0 // width
    out_spec = pl.BlockSpec((sets_per_step, n_exp, cap), lambda s: (s, 0, 0))
    return pl.pallas_call(
        functools.partial(_route_kernel, cap=cap, set_len=set_len, tchunk=min(set_len, 1024)),
        out_shape=[jax.ShapeDtypeStruct((n_sets, n_exp, cap), jnp.int32),
                   jax.ShapeDtypeStruct((n_sets, n_exp, cap), F32)],
        grid=(n_sets // sets_per_step,),
        in_specs=[pl.BlockSpec((n_exp, width), lambda s: (0, blk0 + s))],
        out_specs=[out_spec, out_spec],
        scratch_shapes=[pltpu.VMEM((n_exp, width), F32), pltpu.VMEM((n_exp, width), jnp.int32)],
        compiler_params=_params("arbitrary"),
        name="route",
    )(logits_t)


GATHER_ROWS = 16


def _gather_kernel(idx_ref, h_ref, o_ref, *, n_groups, slots):
    g = pl.program_id(0)
    e = pl.program_id(1)
    base = (e * n_groups + g) * slots

    def body(j, carry):
        s0 = pl.multiple_of(j * GATHER_ROWS, GATHER_ROWS)
        rows = [h_ref[0, pl.ds(idx_ref[base + s0 + i], 1), :] for i in range(GATHER_ROWS)]
        o_ref[0, 0, pl.ds(s0, GATHER_ROWS), :] = jnp.concatenate(rows, axis=0).astype(o_ref.dtype)
        return carry

    lax.fori_loop(0, slots // GATHER_ROWS, body, 0)


def moe_gather(idx_flat, h, n_exp, n_groups, slots):
    _, tg, d = h.shape
    return pl.pallas_call(
        functools.partial(_gather_kernel, n_groups=n_groups, slots=slots),
        out_shape=jax.ShapeDtypeStruct((n_exp, n_groups, slots, d), BF16),
        grid_spec=pltpu.PrefetchScalarGridSpec(
            num_scalar_prefetch=1,
            grid=(n_groups, n_exp),
            in_specs=[pl.BlockSpec((1, tg, d), lambda g, e, idx: (g, 0, 0))],
            out_specs=pl.BlockSpec((1, 1, slots, d), lambda g, e, idx: (e, g, 0, 0)),
        ),
        compiler_params=_params("arbitrary", "arbitrary"),
        name="moe_gather",
    )(idx_flat, h)


def _ffn_kernel(xs_ref, wg_ref, wu_ref, wd_ref, gt_ref, o_ref, *, tm):
    f = pl.program_id(1)
    wg = wg_ref[0, 0].astype(BF16)
    wu = wu_ref[0, 0].astype(BF16)
    wd = wd_ref[0, 0].astype(BF16)

    @pl.when(f == 0)
    def _():
        o_ref[...] = jnp.zeros_like(o_ref)

    for i in range(xs_ref.shape[1] // tm):
        rows = slice(i * tm, (i + 1) * tm)
        x = xs_ref[0, rows, :]
        hid = _silu(jnp.dot(x, wg, preferred_element_type=F32)) * jnp.dot(
            x, wu, preferred_element_type=F32)
        o_ref[0, rows, :] += jnp.dot(hid.astype(BF16), wd, preferred_element_type=F32)

    @pl.when(f == pl.num_programs(1) - 1)
    def _():
        for i in range(xs_ref.shape[1] // LANES):
            g_rows = jnp.broadcast_to(gt_ref[0, :, i * LANES:(i + 1) * LANES], (LANES, LANES)).T
            for k in range(o_ref.shape[2] // LANES):
                blk = (0, slice(i * LANES, (i + 1) * LANES), slice(k * LANES, (k + 1) * LANES))
                o_ref[blk] = o_ref[blk] * g_rows


def moe_ffn(xs, w_gate, w_up, w_down, layer, gate_row, tf=512, tm=512):
    n_exp, m, d = xs.shape
    f = w_gate.shape[3]
    tf = min(tf, f)
    return pl.pallas_call(
        functools.partial(_ffn_kernel, tm=min(tm, m)),
        out_shape=jax.ShapeDtypeStruct((n_exp, m, d), F32),
        grid=(n_exp, f // tf),
        in_specs=[pl.BlockSpec((1, m, d), lambda e, j: (e, 0, 0)),
                  pl.BlockSpec((1, 1, d, tf), lambda e, j: (layer, e, 0, j)),
                  pl.BlockSpec((1, 1, d, tf), lambda e, j: (layer, e, 0, j)),
                  pl.BlockSpec((1, 1, tf, d), lambda e, j: (layer, e, j, 0)),
                  pl.BlockSpec((1, 1, m), lambda e, j: (e, 0, 0))],
        out_specs=pl.BlockSpec((1, m, d), lambda e, j: (e, 0, 0)),
        compiler_params=_params("arbitrary", "arbitrary"),
        name="moe_ffn",
    )(xs, w_gate, w_up, w_down, gate_row)


def _rows_to_tiles(tile_ref, rows):
    for k in range(tile_ref.shape[1]):
        tile_ref[:, k, :] = rows[:, k * LANES:(k + 1) * LANES]


def _tiles_to_rows(tile_ref):
    return jnp.concatenate([tile_ref[:, k, :] for k in range(tile_ref.shape[1])], axis=1)


COMBINE_UNROLL = 8


def _combine_kernel(idx_ref, ys_ref, o_ref, stage_ref, *, n_groups, slots):
    g = pl.program_id(0)
    e = pl.program_id(1)
    base = (e * n_groups + g) * slots

    @pl.when(e == 0)
    def _():
        o_ref[...] = jnp.zeros_like(o_ref)

    _rows_to_tiles(stage_ref, ys_ref[0, 0])

    def body(j, carry):
        s0 = pl.multiple_of(j * COMBINE_UNROLL, COMBINE_UNROLL)
        toks = [idx_ref[base + s0 + i] for i in range(COMBINE_UNROLL)]
        sums = [o_ref[0, t] + stage_ref[s0 + i] for i, t in enumerate(toks)]
        for t, v in zip(toks, sums):
            o_ref[0, t] = v
        return carry

    lax.fori_loop(0, slots // COMBINE_UNROLL, body, 0)


def moe_combine(idx_flat, ys, tg):
    n_exp, n_groups, slots, d = ys.shape
    c = d // LANES
    return pl.pallas_call(
        functools.partial(_combine_kernel, n_groups=n_groups, slots=slots),
        out_shape=jax.ShapeDtypeStruct((n_groups, tg, c, LANES), F32),
        grid_spec=pltpu.PrefetchScalarGridSpec(
            num_scalar_prefetch=1,
            grid=(n_groups, n_exp),
            in_specs=[pl.BlockSpec((1, 1, slots, d), lambda g, e, idx: (e, g, 0, 0))],
            out_specs=pl.BlockSpec((1, tg, c, LANES), lambda g, e, idx: (g, 0, 0, 0)),
            scratch_shapes=[pltpu.VMEM((slots, c, LANES), F32)],
        ),
        compiler_params=_params("arbitrary", "arbitrary"),
        name="moe_combine",
    )(idx_flat, ys)


def _moe_res_kernel(x_ref, y_ref, g_ref, nw_ref, sh_ref, sc_ref, x2_ref, h_ref):
    x2 = x_ref[...] + g_ref[0] * _tiles_to_rows(y_ref)
    x2_ref[...] = x2
    h_ref[...] = (_rms(x2, nw_ref[...]) * (1.0 + sc_ref[0]) + sh_ref[0]).astype(h_ref.dtype)


def moe_residual(x, y, gate, nw, shift, scale, tg, tm=512):
    m, d = x.shape
    per = tg // tm
    vec = pl.BlockSpec((1, 1, d), lambda i: (i // per, 0, 0))
    row = pl.BlockSpec((tm, d), lambda i: (i, 0))
    tiles = pl.BlockSpec((tm,) + y.shape[1:], lambda i: (i, 0, 0))
    return pl.pallas_call(
        _moe_res_kernel,
        out_shape=[jax.ShapeDtypeStruct((m, d), F32), jax.ShapeDtypeStruct((m, d), BF16)],
        grid=(m // tm,),
        in_specs=[row, tiles, vec, pl.BlockSpec((1, d), lambda i: (0, 0)), vec, vec],
        out_specs=[row, row],
        compiler_params=_params("arbitrary"),
        name="moe_residual",
    )(x, y, gate, nw.reshape(1, d), shift, scale)


def _final_kernel(x_ref, y_ref, g_ref, nw_ref, o_ref):
    o_ref[...] = _rms(x_ref[...] + g_ref[0] * _tiles_to_rows(y_ref), nw_ref[...])


def final_norm(x, y, gate, nw, tg, row0, rows, tm=512):
    d = x.shape[1]
    per = tg // tm
    off = row0 // tm
    row = pl.BlockSpec((tm, d), lambda i: (off + i, 0))
    return pl.pallas_call(
        _final_kernel,
        out_shape=jax.ShapeDtypeStruct((rows, d), F32),
        grid=(rows // tm,),
        in_specs=[row, pl.BlockSpec((tm,) + y.shape[1:], lambda i: (off + i, 0, 0)),
                  pl.BlockSpec((1, 1, d), lambda i: ((off + i) // per, 0, 0)),
                  pl.BlockSpec((1, d), lambda i: (0, 0))],
        out_specs=pl.BlockSpec((tm, d), lambda i: (i, 0)),
        compiler_params=_params("arbitrary"),
        name="final_norm",
    )(x, y, gate, nw.reshape(1, d))


CONV_HALO = 16


def _conv_kernel(prev_ref, cur_ref, next_ref, w_ref, b_ref, o_ref, ext_ref, *, tm, seq_a, n_a, seq_b):
    i = pl.program_id(0)
    row0 = i * tm
    seq = jnp.where(row0 < n_a, seq_a, seq_b)
    off = jnp.where(row0 < n_a, row0, row0 - n_a)
    first = lax.rem(off, seq) == 0
    last = lax.rem(off + tm, seq) == 0
    half = D_CONV // 2
    pad = SUBLANES
    ext_ref[0:pad] = jnp.where(first, 0.0, prev_ref[...].astype(F32)[CONV_HALO - pad:])
    ext_ref[pad:pad + tm] = cur_ref[...].astype(F32)
    ext_ref[pad + tm:] = jnp.where(last, 0.0, next_ref[...].astype(F32)[:pad])
    acc = b_ref[...] + w_ref[0:1, :] * ext_ref[pad - half:pad - half + tm]
    for k in range(1, D_CONV):
        acc = acc + w_ref[k:k + 1, :] * ext_ref[pad - half + k:pad - half + k + tm]
    o_ref[...] = _silu(acc).astype(o_ref.dtype)


def ssd_conv(xbc, w, b, seq_a, n_a, seq_b, tm=256):
    m, c = xbc.shape
    hb = tm // CONV_HALO
    n_halo = m // CONV_HALO
    return pl.pallas_call(
        functools.partial(_conv_kernel, tm=tm, seq_a=seq_a, n_a=n_a, seq_b=seq_b),
        out_shape=jax.ShapeDtypeStruct((m, c), BF16),
        grid=(m // tm,),
        in_specs=[pl.BlockSpec((CONV_HALO, c), lambda i: (jnp.maximum(i * hb - 1, 0), 0)),
                  pl.BlockSpec((tm, c), lambda i: (i, 0)),
                  pl.BlockSpec((CONV_HALO, c), lambda i: (jnp.minimum((i + 1) * hb, n_halo - 1), 0)),
                  pl.BlockSpec((D_CONV, c), lambda i: (0, 0)),
                  pl.BlockSpec((1, c), lambda i: (0, 0))],
        out_specs=pl.BlockSpec((tm, c), lambda i: (i, 0)),
        scratch_shapes=[pltpu.VMEM((tm + 2 * SUBLANES, c), F32)],
        compiler_params=_params("arbitrary"),
        name="ssd_conv",
    )(xbc, xbc, xbc, w, b.reshape(1, c))


def _softplus(x):
    return jnp.maximum(x, 0.0) + jnp.log(1.0 + jnp.exp(-jnp.abs(x)))


DT_REPLICAS = 3


def _split3(v, lane, n_heads):
    hi = v.astype(BF16)
    r1 = v - hi.astype(F32)
    mid = r1.astype(BF16)
    lo = (r1 - mid.astype(F32)).astype(BF16)
    zero = jnp.zeros_like(lo)
    return jnp.where(lane < n_heads, hi, jnp.where(
        lane < 2 * n_heads, mid, jnp.where(lane < DT_REPLICAS * n_heads, lo, zero)))


def _scan_kernel(rb_ref, dir_ref, first_ref, last_ref, zero_ref, s0i_ref, soi_ref,
                 xbc_ref, dt_ref, dtb_ref, alog_ref, dsk_ref, e64_ref, s0_ref,
                 y_ref, sout_ref, st_ref, *, n_heads):
    i = pl.program_id(0)
    q = xbc_ref.shape[0]
    p, n, g_n = SSD_HEADDIM, D_STATE, N_GROUPS
    hpg = n_heads // g_n
    d_inner = n_heads * p
    fwd = dir_ref[i] == 0

    @pl.when((first_ref[i] == 1) & (zero_ref[i] == 1))
    def _():
        st_ref[...] = jnp.zeros_like(st_ref)

    @pl.when((first_ref[i] == 1) & (zero_ref[i] == 0))
    def _():
        for g in range(g_n):
            st_ref[g] = s0_ref[0, 0, g].T

    dt = _softplus(dt_ref[...] + dtb_ref[0])
    da = dt * (-LOG2E * jnp.exp(alog_ref[0]))
    li = lax.broadcasted_iota(jnp.int32, (q, q), 0)
    si = lax.broadcasted_iota(jnp.int32, (q, q), 1)
    tri = (li - si) * jnp.where(fwd, 1, -1) >= 0
    lane = lax.broadcasted_iota(jnp.int32, (q, LANES), 1)
    parts = jnp.dot(jnp.where(tri, 1.0, 0.0).astype(BF16), _split3(da, lane, n_heads),
                    preferred_element_type=F32)
    total = (parts + pltpu.roll(parts, LANES - n_heads, axis=1)
             + pltpu.roll(parts, LANES - 2 * n_heads, axis=1))
    acs = jnp.where(lane < n_heads, total,
                    jnp.where(lane < 2 * n_heads, pltpu.roll(total, n_heads, axis=1),
                              pltpu.roll(total, 2 * n_heads, axis=1)))
    acs_t = acs.T

    f_dt = _split3(dt, lane, n_heads)
    f_acs = _split3(acs, lane, n_heads)
    dt_all = jnp.dot(f_dt, e64_ref[...], preferred_element_type=F32)
    acs_all = jnp.dot(f_acs, e64_ref[...], preferred_element_type=F32)
    acs_end = jnp.where(fwd, acs_all[q - 1:q, :], acs_all[0:1, :])
    e_in = jnp.exp2(acs_all)
    dec = jnp.exp2(acs_end)
    x = xbc_ref[:, :d_inner].astype(F32)
    xdt = x * dt_all
    skip = x * jnp.where(fwd, dsk_ref[...], 0.0)
    xdt_b = xdt.astype(BF16)
    xd_b = (xdt * jnp.exp2(acs_end - acs_all)).astype(BF16)
    first_head = lax.broadcasted_iota(jnp.int32, (q, 2 * p), 1) < p

    for g in range(g_n):
        gcols = slice(g * hpg * p, (g + 1) * hpg * p)
        bm = xbc_ref[:, d_inner + g * n:d_inner + (g + 1) * n]
        cm = xbc_ref[:, d_inner + g_n * n + g * n:d_inner + g_n * n + (g + 1) * n]
        cb = lax.dot_general(cm, bm, NT_DIMS, preferred_element_type=F32)
        st = st_ref[g]
        y_off = jnp.dot(cm, st.astype(BF16), preferred_element_type=F32) * e_in[:, gcols]
        for e2 in range(hpg // 2):
            h0 = g * hpg + 2 * e2
            pc = slice(h0 * p, (h0 + 2) * p)
            a_pair = acs_all[:, pc]
            a_swap = pltpu.roll(a_pair, p, axis=1)
            a_heads = (jnp.where(first_head, a_pair, a_swap), jnp.where(first_head, a_swap, a_pair))
            ms = []
            for k in range(2):
                seg = a_heads[k] - acs_t[h0 + k:h0 + k + 1, :]
                ms.append((jnp.where(tri, jnp.exp2(seg), 0.0) * cb).astype(BF16))
            yd = jnp.dot(jnp.concatenate(ms, axis=0), xdt_b[:, pc], preferred_element_type=F32)
            y = (jnp.where(first_head, yd[:q], yd[q:]) + y_off[:, 2 * e2 * p:(2 * e2 + 2) * p]
                 + skip[:, pc])
            y_ref[0, :, pc] = y.astype(y_ref.dtype)
        bm_t = bm.astype(F32).T.astype(BF16)
        s_new = jnp.dot(bm_t, xd_b[:, gcols], preferred_element_type=F32)
        st_ref[g] = st * dec[:, gcols] + s_new

    @pl.when(last_ref[i] == 1)
    def _():
        for g in range(g_n):
            sout_ref[0, 0, g] = st_ref[g].T


def _expansion(n_heads, width):
    row_head = jnp.arange(LANES) % n_heads
    row_ok = jnp.arange(LANES) < DT_REPLICAS * n_heads
    col_head = jnp.arange(n_heads * width) // width
    return ((row_head[:, None] == col_head[None, :]) & row_ok[:, None]).astype(BF16)


def ssd_scan(xconv, dt2, dtb2, alog2, d_cols, s0_t, n_heads, seqs):
    m, c = xconv.shape
    q = SSD_CHUNK
    hp = n_heads // N_GROUPS * SSD_HEADDIM
    d_inner = n_heads * SSD_HEADDIM
    assert DT_REPLICAS * n_heads <= LANES and 2 * SSD_HEADDIM == LANES == q
    assert (n_heads // N_GROUPS) % 2 == 0
    tabs = [[] for _ in range(7)]
    for row0, length, s0_slot, out_slot in seqs:
        nc = length // q
        for d in range(2):
            for ci in range(nc):
                vals = (row0 // q + (ci if d == 0 else nc - 1 - ci), d, int(ci == 0),
                        int(ci == nc - 1), int(s0_slot is None), s0_slot or 0, out_slot)
                for t, v in zip(tabs, vals):
                    t.append(v)
    n_steps = len(tabs[0])
    n_out = max(t[3] for t in seqs) + 1
    tabs = [jnp.asarray(t, jnp.int32) for t in tabs]

    st_block = (1, 1, N_GROUPS, hp, D_STATE)
    const = lambda i, *_: (0, 0)
    return pl.pallas_call(
        functools.partial(_scan_kernel, n_heads=n_heads),
        out_shape=[jax.ShapeDtypeStruct((2, m, d_inner), BF16),
                   jax.ShapeDtypeStruct((2, n_out, N_GROUPS, hp, D_STATE), F32)],
        grid_spec=pltpu.PrefetchScalarGridSpec(
            num_scalar_prefetch=7,
            grid=(n_steps,),
            in_specs=[pl.BlockSpec((q, c), lambda i, rb, dr, *_: (rb[i], 0)),
                      pl.BlockSpec((q, LANES), lambda i, rb, dr, *_: (rb[i], dr[i])),
                      pl.BlockSpec((1, 1, LANES), lambda i, rb, dr, *_: (dr[i], 0, 0)),
                      pl.BlockSpec((1, 1, LANES), lambda i, rb, dr, *_: (dr[i], 0, 0)),
                      pl.BlockSpec((1, d_inner), const),
                      pl.BlockSpec((LANES, d_inner), const),
                      pl.BlockSpec(st_block, lambda i, rb, dr, f, l, z, s0i, soi: (dr[i], s0i[i], 0, 0, 0))],
            out_specs=[pl.BlockSpec((1, q, d_inner), lambda i, rb, dr, *_: (dr[i], rb[i], 0)),
                       pl.BlockSpec(st_block, lambda i, rb, dr, f, l, z, s0i, soi: (dr[i], soi[i], 0, 0, 0))],
            scratch_shapes=[pltpu.VMEM((N_GROUPS, D_STATE, hp), F32)],
        ),
        compiler_params=_params("arbitrary"),
        name="ssd_scan",
    )(*tabs, xconv, dt2, dtb2, alog2, d_cols.reshape(1, d_inner), _expansion(n_heads, SSD_HEADDIM),
      s0_t)


def _replicate_heads(v, n_heads):
    lead = v.shape[:-1]
    v = v.astype(F32).reshape(lead + (2, 1, n_heads))
    v = jnp.broadcast_to(v, lead + (2, DT_REPLICAS, n_heads)).reshape(lead + (2, DT_REPLICAS * n_heads))
    pad = [(0, 0)] * (len(lead) + 1) + [(0, LANES - DT_REPLICAS * n_heads)]
    return jnp.pad(v, pad).reshape(lead + (2 * LANES,))


def _state_groups(s, n_heads):
    return s.reshape(s.shape[0], N_GROUPS, n_heads // N_GROUPS * SSD_HEADDIM, D_STATE)


def _moe(h, logits_t, w_gate, w_up, w_down, layer, dims):
    n_req, seq, n_lat, tg, d = dims
    n_groups = 1 + n_lat
    n_exp = logits_t.shape[0]
    idx_p, gate_p = route(logits_t, seq, 0, n_req, n_req)
    idx_s, gate_s = route(logits_t, tg, tg, n_lat, 1)
    slots = idx_s.shape[2]
    idx_p = idx_p + (jnp.arange(n_req, dtype=jnp.int32) * seq)[:, None, None]
    idx = jnp.concatenate([idx_p.transpose(1, 0, 2).reshape(n_exp, 1, slots),
                           idx_s.transpose(1, 0, 2)], axis=1)
    gate = jnp.concatenate([gate_p.transpose(1, 0, 2).reshape(n_exp, 1, slots),
                            gate_s.transpose(1, 0, 2)], axis=1)
    idx_flat = idx.reshape(-1)
    xs = moe_gather(idx_flat, h.reshape(n_groups, tg, d), n_exp, n_groups, slots)
    ys = moe_ffn(xs.reshape(n_exp, n_groups * slots, d), w_gate, w_up, w_down, layer,
                 gate.reshape(n_exp, 1, n_groups * slots))
    out = moe_combine(idx_flat, ys.reshape(n_exp, n_groups, slots, d), tg)
    return out.reshape((n_groups * tg,) + out.shape[2:])


def kernel(x_prompt, x_sample, cache_k_na, cache_v_na, state_ssd_fwd, state_ssd_bwd, c, c_ctx, ada_w, ada_b, norm1_w, norm2_w, final_norm_w, na_qkv_w, na_out_w, na_rpb, ssd_in_w, ssd_conv_w, ssd_conv_b, ssd_dt_bias, ssd_a_log, ssd_d, ssd_norm_w, ssd_out_w, router_w, moe_w_gate, moe_w_up, moe_w_down):
    n_req, seq, d = x_prompt.shape
    n_lat, tg, _ = x_sample.shape
    assert n_req * seq == tg, "token groups must have equal size"
    assert n_lat + 1 <= MOD_ROWS
    depth = ada_w.shape[0]
    n_groups = 1 + n_lat
    m_p = n_req * seq
    dims = (n_req, seq, n_lat, tg, d)

    cvec = jnp.zeros((MOD_ROWS, d), F32).at[0].set(c_ctx).at[1:1 + n_lat].set(c)
    mod = ada_modulation(cvec, ada_w, ada_b)
    mod = mod.reshape(depth, MOD_ROWS, N_MOD, 1, d).transpose(0, 2, 1, 3, 4)

    m_s = n_lat * tg
    x_first, x_second, x_second_row0 = x_prompt.reshape(m_p, d), x_sample.reshape(m_s, d), 0
    h = prenorm(x_first, x_second, norm1_w[0], mod[0, 0], mod[0, 1], tg, BF16)

    new_k, new_v, new_sf, new_sb = [], [], [], []
    y_prompt = y_sample = None
    for layer in range(depth):
        j = layer // 2
        ml = mod[layer]
        if layer % 2 == 0:
            q_p, k_p, v_p = (matmul(h, na_qkv_w, j, F32, rows=m_p, col0=i * d, cols=d)
                             for i in range(3))
            qkv_s = matmul(h, na_qkv_w, j, BF16, row0=m_p)
            new_k.append(k_p.reshape(n_req, seq, N_HEADS, d // N_HEADS))
            new_v.append(v_p.reshape(n_req, seq, N_HEADS, d // N_HEADS))
            att_p = ctx_attention(q_p, k_p, v_p, seq)
            past = cache_k_na.shape[2]
            att_s = na_attention(qkv_s.reshape(n_lat, tg, 3 * d),
                                 cache_k_na[:, j].reshape(n_lat, past, d).astype(BF16),
                                 cache_v_na[:, j].reshape(n_lat, past, d).astype(BF16),
                                 na_bias_table(na_rpb[j], tg // GRID_W))
            x1, hm, logits_t = attn_out(att_p, att_s.reshape(m_s, d), na_out_w, j, x_first, x_second,
                                        x_second_row0, ml[2], norm2_w[layer], ml[3], ml[4],
                                        router_w[layer], tg)
        else:
            n_heads = ssd_d.shape[1]
            d_inner = n_heads * SSD_HEADDIM
            conv_dim = ssd_conv_w.shape[2]
            zs = matmul(h, ssd_in_w, j, BF16, cols=d_inner, silu=True)
            xbc = matmul(h, ssd_in_w, j, BF16, col0=d_inner, cols=conv_dim)
            w_dt = _replicate_heads(ssd_in_w[j, :, d_inner + conv_dim:], n_heads)
            dt2 = matmul(h, w_dt[None], 0, F32)
            xconv = ssd_conv(xbc, ssd_conv_w[j], ssd_conv_b[j], seq, m_p, tg)
            dtb2 = _replicate_heads(ssd_dt_bias[j].reshape(-1), n_heads).reshape(2, 1, LANES)
            alog2 = _replicate_heads(ssd_a_log[j].reshape(-1), n_heads).reshape(2, 1, LANES)
            s0 = jnp.stack([_state_groups(state_ssd_fwd[:, j], n_heads),
                            _state_groups(state_ssd_bwd[:, j], n_heads)])
            seqs = [(r * seq, seq, None, r) for r in range(n_req)]
            seqs += [(m_p + b * tg, tg, b, n_req + b) for b in range(n_lat)]
            d_cols = jnp.repeat(ssd_d[j], SSD_HEADDIM)
            y2, st = ssd_scan(xconv, dt2, dtb2, alog2, d_cols, s0, n_heads, seqs)
            new_sf.append(st[0, :n_req].reshape(n_req, n_heads, SSD_HEADDIM, D_STATE))
            new_sb.append(st[1, :n_req].reshape(n_req, n_heads, SSD_HEADDIM, D_STATE))
            x1, hm, logits_t = ssd_out(y2, zs, ssd_norm_w[j], ssd_out_w, j, x_first,
                                       ml[2], norm2_w[layer], ml[3], ml[4], router_w[layer], tg)
        y_moe = _moe(hm, logits_t, moe_w_gate, moe_w_up, moe_w_down, layer, dims)
        if layer + 1 < depth:
            mn = mod[layer + 1]
            x, h = moe_residual(x1, y_moe, ml[5], norm1_w[layer + 1], mn[0], mn[1], tg)
            x_first, x_second, x_second_row0 = x, x, m_p
        else:
            y_prompt = final_norm(x1, y_moe, ml[5], final_norm_w, tg, 0, m_p)
            y_sample = final_norm(x1, y_moe, ml[5], final_norm_w, tg, m_p, m_s)

    return (y_prompt.reshape(n_req, seq, d), y_sample.reshape(n_lat, tg, d),
            jnp.stack(new_k, axis=1), jnp.stack(new_v, axis=1),
            jnp.stack(new_sf, axis=1), jnp.stack(new_sb, axis=1))
```

```python
import functools

import jax
import jax.numpy as jnp
from jax import lax
from jax.experimental import pallas as pl
from jax.experimental.pallas import tpu as pltpu

N_HEADS = 16
GRID_W = 64
WIN_H = 8
WIN_W = 16
SSD_HEADDIM = 64
N_GROUPS = 4
D_STATE = 128
D_CONV = 5
SSD_CHUNK = 128
CAPACITY_FACTOR = 2
N_MOD = 6
RMS_EPS = 1e-6

LANES = 128
SUBLANES = 8
MOD_ROWS = 8
VMEM_LIMIT = 56 * 1024 * 1024
NEG_INF = -1e30

F32 = jnp.float32
BF16 = jnp.bfloat16
HIGHEST = lax.Precision.HIGHEST
NT_DIMS = (((1,), (1,)), ((), ()))


def _params(*sem):
    return pltpu.CompilerParams(dimension_semantics=sem, vmem_limit_bytes=VMEM_LIMIT)


def _silu(x):
    return x / (1.0 + jnp.exp(-x))


def _rms(x, w):
    ms = jnp.mean(x * x, axis=-1, keepdims=True)
    return x * lax.rsqrt(ms + RMS_EPS) * w


def _ada_kernel(c_ref, w_ref, b_ref, o_ref):
    o_ref[0] = jnp.dot(_silu(c_ref[...]), w_ref[0], preferred_element_type=F32,
                       precision=HIGHEST) + b_ref[0]


def ada_modulation(cvec, ada_w, ada_b):
    n_layers, d, n = ada_w.shape
    tn = n // 4
    return pl.pallas_call(
        _ada_kernel,
        out_shape=jax.ShapeDtypeStruct((n_layers, MOD_ROWS, n), F32),
        grid=(n_layers, n // tn),
        in_specs=[pl.BlockSpec((MOD_ROWS, d), lambda l, j: (0, 0)),
                  pl.BlockSpec((1, d, tn), lambda l, j: (l, 0, j)),
                  pl.BlockSpec((1, 1, tn), lambda l, j: (l, 0, j))],
        out_specs=pl.BlockSpec((1, MOD_ROWS, tn), lambda l, j: (l, 0, j)),
        compiler_params=_params("arbitrary", "arbitrary"),
        name="ada_modulation",
    )(cvec, ada_w, ada_b.reshape(n_layers, 1, n))


def _two_source_specs(tm, width, n_first, off_second):
    first = pl.BlockSpec((tm, width), lambda i: (jnp.minimum(i, n_first - 1), 0))
    second = pl.BlockSpec((tm, width), lambda i: (jnp.maximum(i - n_first, 0) + off_second, 0))
    return [first, second]


def _pick(first_ref, second_ref, n_first):
    return jnp.where(pl.program_id(0) < n_first, first_ref[...], second_ref[...])


def _prenorm_kernel(xa_ref, xb_ref, nw_ref, sh_ref, sc_ref, h_ref, *, n_first):
    x = _pick(xa_ref, xb_ref, n_first)
    h = _rms(x, nw_ref[...]) * (1.0 + sc_ref[0]) + sh_ref[0]
    h_ref[...] = h.astype(h_ref.dtype)


def prenorm(xa, xb, nw, shift, scale, tg, out_dtype, tm=512):
    d = xa.shape[1]
    m = xa.shape[0] + xb.shape[0]
    per = tg // tm
    n_first = xa.shape[0] // tm
    vec = pl.BlockSpec((1, 1, d), lambda i: (i // per, 0, 0))
    return pl.pallas_call(
        functools.partial(_prenorm_kernel, n_first=n_first),
        out_shape=jax.ShapeDtypeStruct((m, d), out_dtype),
        grid=(m // tm,),
        in_specs=_two_source_specs(tm, d, n_first, 0) + [pl.BlockSpec((1, d), lambda i: (0, 0)), vec, vec],
        out_specs=pl.BlockSpec((tm, d), lambda i: (i, 0)),
        compiler_params=_params("arbitrary"),
        name="prenorm",
    )(xa, xb, nw.reshape(1, d), shift, scale)


def _mm_kernel(x_ref, w_ref, o_ref, wb_ref, *, silu):
    @pl.when(pl.program_id(1) == 0)
    def _():
        wb_ref[...] = w_ref[0].astype(BF16)

    y = jnp.dot(x_ref[...], wb_ref[...], preferred_element_type=F32)
    o_ref[...] = (_silu(y) if silu else y).astype(o_ref.dtype)


def matmul(x, w, layer, out_dtype, row0=0, rows=None, col0=0, cols=None, silu=False, tm=1024,
           tn=1024):
    k = x.shape[1]
    rows = x.shape[0] - row0 if rows is None else rows
    cols = w.shape[2] - col0 if cols is None else cols
    tn = max(t for t in range(LANES, min(tn, cols) + 1, LANES) if cols % t == 0 and col0 % t == 0)
    tm = min(tm, rows)
    r_off, c_off = row0 // tm, col0 // tn
    return pl.pallas_call(
        functools.partial(_mm_kernel, silu=silu),
        out_shape=jax.ShapeDtypeStruct((rows, cols), out_dtype),
        grid=(cols // tn, rows // tm),
        in_specs=[pl.BlockSpec((tm, k), lambda j, i: (r_off + i, 0)),
                  pl.BlockSpec((1, k, tn), lambda j, i: (layer, 0, c_off + j))],
        out_specs=pl.BlockSpec((tm, tn), lambda j, i: (i, j)),
        scratch_shapes=[pltpu.VMEM((k, tn), BF16)],
        compiler_params=_params("arbitrary", "arbitrary"),
        name="matmul",
    )(x, w)


LOG2E = 1.4426950408889634


def _softmax_pv(s_list, v_list):
    s = jnp.concatenate(s_list, axis=-1) if len(s_list) > 1 else s_list[0]
    p = jnp.exp2(s - s.max(axis=-1, keepdims=True))
    den = p.sum(axis=-1, keepdims=True)
    p = p.astype(BF16)
    acc = None
    col = 0
    for v in v_list:
        pv = jnp.dot(p[:, col:col + v.shape[0]], v, preferred_element_type=F32)
        acc = pv if acc is None else acc + pv
        col += v.shape[0]
    return acc / den


def _ctx_attn_kernel(q_ref, k_ref, v_ref, o_ref, *, n_heads):
    dh = q_ref.shape[1] // n_heads
    scale = dh ** -0.5 * LOG2E
    seq = q_ref.shape[0]
    first = lax.broadcasted_iota(jnp.int32, (seq, 2 * dh), 1) < dh
    for j in range(n_heads // 2):
        sl = slice(2 * j * dh, 2 * (j + 1) * dh)
        q2 = q_ref[:, sl].astype(BF16)
        qq = jnp.concatenate([jnp.where(first, q2, 0), jnp.where(first, 0, q2)], axis=0)
        k = k_ref[:, sl].astype(BF16)
        v = v_ref[:, sl].astype(BF16)
        s = lax.dot_general(qq, k, NT_DIMS, preferred_element_type=F32) * scale
        o2 = _softmax_pv([s], [v])
        o_ref[:, sl] = jnp.where(first, o2[:seq], o2[seq:]).astype(o_ref.dtype)


def ctx_attention(q, k, v, seq):
    m, d = q.shape
    blk = pl.BlockSpec((seq, d), lambda r: (r, 0))
    return pl.pallas_call(
        functools.partial(_ctx_attn_kernel, n_heads=N_HEADS),
        out_shape=jax.ShapeDtypeStruct((m, d), BF16),
        grid=(m // seq,),
        in_specs=[blk, blk, blk],
        out_specs=blk,
        compiler_params=_params("arbitrary"),
        name="ctx_attention",
    )(q, k, v)


def _na_row_start(r, rows, kh):
    return jnp.clip(r - kh // 2, 0, rows - kh)


def na_bias_table(rpb, rows):
    w, kw = GRID_W, WIN_W
    kh = min(WIN_H, rows)
    col = jnp.arange(w)
    cs = jnp.clip(col - kw // 2, 0, w - kw)
    dc = col[None, :] - col[:, None] + (WIN_W - 1)
    valid = (col[None, :] >= cs[:, None]) & (col[None, :] < cs[:, None] + kw)
    pick = (dc[None] == jnp.arange(2 * WIN_W - 1)[:, None, None]).astype(F32)
    c_full = jnp.einsum('hrd,dqk->hrqk', rpb, pick, precision=HIGHEST)
    c_full = jnp.where(valid[None, None], c_full * LOG2E, NEG_INF)
    n_off = WIN_H
    tab = jnp.stack([jnp.concatenate([c_full[:, off + j] for j in range(kh)], axis=-1)
                     for off in range(n_off)])
    return tab.reshape(n_off, rpb.shape[0] // 2, 2 * w, kh * w)


NA_ROWS_PER_STEP = 2


def _na_kernel(q_ref, k_ref, v_ref, kc_ref, vc_ref, *rest, n_heads, rows, kh, w):
    bias_refs, o_ref = rest[:-1], rest[-1]
    dh = q_ref.shape[2] // n_heads
    scale = dh ** -0.5 * LOG2E
    first = lax.broadcasted_iota(jnp.int32, (w, 2 * dh), 1) < dh
    for i, b_ref in enumerate(bias_refs):
        r0 = _na_row_start(pl.program_id(1) * len(bias_refs) + i, rows, kh)
        start = pl.multiple_of(r0 * w, w)
        qrows = slice(i * w, (i + 1) * w)
        for j in range(n_heads // 2):
            sl = slice(2 * j * dh, 2 * (j + 1) * dh)
            q2 = q_ref[0, qrows, sl]
            qq = jnp.concatenate([jnp.where(first, q2, 0), jnp.where(first, 0, q2)], axis=0)
            kwin = k_ref[0, pl.ds(start, kh * w), sl]
            vwin = v_ref[0, pl.ds(start, kh * w), sl]
            s_win = (lax.dot_general(qq, kwin, NT_DIMS, preferred_element_type=F32) * scale
                     + b_ref[0, j])
            s_ctx = lax.dot_general(qq, kc_ref[0, :, sl], NT_DIMS, preferred_element_type=F32) * scale
            o2 = _softmax_pv([s_win, s_ctx], [vwin, vc_ref[0, :, sl]])
            o_ref[0, qrows, sl] = jnp.where(first, o2[:w], o2[w:]).astype(o_ref.dtype)


def na_attention(qkv, k_ctx, v_ctx, bias):
    b, t, d3 = qkv.shape
    d = d3 // 3
    w = GRID_W
    rows = t // w
    kh = min(WIN_H, rows)
    n_ctx = k_ctx.shape[1]

    rps = NA_ROWS_PER_STEP
    assert rows % rps == 0 and 2 * (d // N_HEADS) == LANES

    def bias_spec(i):
        def bias_map(bi, s):
            r = s * rps + i
            return (_na_row_start(r, rows, kh) - r + WIN_H - 1, 0, 0, 0)
        return pl.BlockSpec((1, N_HEADS // 2, 2 * w, kh * w), bias_map)

    return pl.pallas_call(
        functools.partial(_na_kernel, n_heads=N_HEADS, rows=rows, kh=kh, w=w),
        out_shape=jax.ShapeDtypeStruct((b, t, d), BF16),
        grid=(b, rows // rps),
        in_specs=[pl.BlockSpec((1, rps * w, d), lambda bi, s: (bi, s, 0)),
                  pl.BlockSpec((1, t, d), lambda bi, s: (bi, 0, 1)),
                  pl.BlockSpec((1, t, d), lambda bi, s: (bi, 0, 2)),
                  pl.BlockSpec((1, n_ctx, d), lambda bi, s: (bi, 0, 0)),
                  pl.BlockSpec((1, n_ctx, d), lambda bi, s: (bi, 0, 0))]
        + [bias_spec(i) for i in range(rps)],
        out_specs=pl.BlockSpec((1, rps * w, d), lambda bi, s: (bi, s, 0)),
        compiler_params=_params("arbitrary", "arbitrary"),
        name="na_attention",
    )(qkv, qkv, qkv, k_ctx, v_ctx, *([bias] * rps))


def _residual_router_tail(mix, x, g_ref, nw_ref, sh_ref, sc_ref, rw_ref, x1_ref, h_ref, lg_ref):
    x1 = x + g_ref[0] * mix
    x1_ref[...] = x1
    h = _rms(x1, nw_ref[...]) * (1.0 + sc_ref[0]) + sh_ref[0]
    h_ref[...] = h
    n_exp = rw_ref.shape[0]
    rw = rw_ref[...]
    rw_hi = rw.astype(BF16)
    rw_lo = (rw - rw_hi.astype(F32)).astype(BF16)
    h_hi = h.astype(BF16)
    h_lo = (h - h_hi.astype(F32)).astype(BF16)
    both = lax.dot_general(jnp.concatenate([rw_hi, rw_lo], axis=0), h_hi, NT_DIMS,
                           preferred_element_type=F32)
    lg_ref[...] = (both[:n_exp] + both[n_exp:]
                   + lax.dot_general(rw_hi, h_lo, NT_DIMS, preferred_element_type=F32))


def _attn_out_kernel(aa_ref, ab_ref, w_ref, xa_ref, xb_ref, g_ref, nw_ref, sh_ref, sc_ref, rw_ref,
                     x1_ref, h_ref, lg_ref, wb_ref, *, n_first):
    @pl.when(pl.program_id(0) == 0)
    def _():
        wb_ref[...] = w_ref[0].astype(BF16)

    mix = jnp.dot(_pick(aa_ref, ab_ref, n_first), wb_ref[...], preferred_element_type=F32)
    _residual_router_tail(mix, _pick(xa_ref, xb_ref, n_first), g_ref, nw_ref, sh_ref, sc_ref,
                          rw_ref, x1_ref, h_ref, lg_ref)


def _ssd_out_kernel(yf_ref, yb_ref, zs_ref, snw_ref, w_ref, x_ref, g_ref, nw_ref,
                    sh_ref, sc_ref, rw_ref, x1_ref, h_ref, lg_ref, wb_ref):
    @pl.when(pl.program_id(0) == 0)
    def _():
        wb_ref[...] = w_ref[0].astype(BF16)

    y = yf_ref[0].astype(F32) + yb_ref[0].astype(F32)
    y = _rms(y * zs_ref[...].astype(F32), snw_ref[...])
    mix = jnp.dot(y.astype(BF16), wb_ref[...], preferred_element_type=F32)
    _residual_router_tail(mix, x_ref[...], g_ref, nw_ref, sh_ref, sc_ref, rw_ref, x1_ref, h_ref,
                          lg_ref)


def _tail_specs(d, n_exp, tm, per):
    vec = pl.BlockSpec((1, 1, d), lambda i: (i // per, 0, 0))
    row = pl.BlockSpec((tm, d), lambda i: (i, 0))
    in_specs = [vec, pl.BlockSpec((1, d), lambda i: (0, 0)), vec, vec,
                pl.BlockSpec((n_exp, d), lambda i: (0, 0))]
    out_specs = [row, row, pl.BlockSpec((n_exp, tm), lambda i: (0, i))]
    return in_specs, out_specs


def _tail_out_shape(m, d, n_exp):
    return [jax.ShapeDtypeStruct((m, d), F32), jax.ShapeDtypeStruct((m, d), F32),
            jax.ShapeDtypeStruct((n_exp, m), F32)]


def attn_out(a_first, a_second, w, layer, x_first, x_second, x_second_row0, gate, nw, shift, scale,
             router_w, tg, tm=512):
    k = a_first.shape[1]
    d = w.shape[2]
    n_first = a_first.shape[0] // tm
    m = a_first.shape[0] + a_second.shape[0]
    n_exp = router_w.shape[1]
    tail_in, tail_out = _tail_specs(d, n_exp, tm, tg // tm)
    return pl.pallas_call(
        functools.partial(_attn_out_kernel, n_first=n_first),
        out_shape=_tail_out_shape(m, d, n_exp),
        grid=(m // tm,),
        in_specs=_two_source_specs(tm, k, n_first, 0)
        + [pl.BlockSpec((1, k, d), lambda i: (layer, 0, 0))]
        + _two_source_specs(tm, d, n_first, x_second_row0 // tm) + tail_in,
        out_specs=tail_out,
        scratch_shapes=[pltpu.VMEM((k, d), BF16)],
        compiler_params=_params("arbitrary"),
        name="attn_out",
    )(a_first, a_second, w, x_first, x_second, gate, nw.reshape(1, d), shift, scale, router_w.T)


def ssd_out(y2, zs, snw, w, layer, x, gate, nw, shift, scale, router_w, tg, tm=512):
    _, k, d = w.shape
    m = x.shape[0]
    n_exp = router_w.shape[1]
    tail_in, tail_out = _tail_specs(d, n_exp, tm, tg // tm)
    return pl.pallas_call(
        _ssd_out_kernel,
        out_shape=_tail_out_shape(m, d, n_exp),
        grid=(m // tm,),
        in_specs=[pl.BlockSpec((1, tm, k), lambda i: (0, i, 0)),
                  pl.BlockSpec((1, tm, k), lambda i: (1, i, 0)),
                  pl.BlockSpec((tm, k), lambda i: (i, 0)),
                  pl.BlockSpec((1, k), lambda i: (0, 0)),
                  pl.BlockSpec((1, k, d), lambda i: (layer, 0, 0)),
                  pl.BlockSpec((tm, d), lambda i: (i, 0))] + tail_in,
        out_specs=tail_out,
        scratch_shapes=[pltpu.VMEM((k, d), BF16)],
        compiler_params=_params("arbitrary"),
        name="ssd_out",
    )(y2, y2, zs, snw.reshape(1, k), w, x, gate, nw.reshape(1, d), shift, scale, router_w.T)


def _excl_prefix_lanes(m01):
    e, t = m01.shape
    r = lax.broadcasted_iota(jnp.int32, (LANES, LANES), 0)
    c = lax.broadcasted_iota(jnp.int32, (LANES, LANES), 1)
    upper = jnp.where(r < c, 1.0, 0.0).astype(BF16)
    outs = []
    carry = jnp.zeros((e, 1), F32)
    for j in range(t // LANES):
        blk = m01[:, j * LANES:(j + 1) * LANES]
        outs.append(jnp.dot(blk.astype(BF16), upper, preferred_element_type=F32) + carry)
        carry = carry + blk.sum(axis=1, keepdims=True)
    return jnp.concatenate(outs, axis=1)


def _route_kernel(lg_ref, idx_ref, gate_ref, aff_ref, pos_ref, *, cap, set_len, tchunk):
    n_exp, t = lg_ref.shape
    n_sets = t // set_len
    lg = lg_ref[...]
    ex = jnp.exp(lg - lg.max(axis=0, keepdims=True))
    aff = ex / ex.sum(axis=0, keepdims=True)
    bits = pltpu.bitcast(aff, jnp.int32)

    def set_slices(x):
        return [x[:, s * set_len:(s + 1) * set_len] for s in range(n_sets)]

    def per_set_sum(m01):
        if n_sets == 1:
            return m01.sum(axis=1, keepdims=True)
        return jnp.concatenate([jnp.broadcast_to(blk.sum(axis=1, keepdims=True), (n_exp, set_len))
                                for blk in set_slices(m01)], axis=1)

    def per_set_prefix(m01):
        return jnp.concatenate([_excl_prefix_lanes(blk) for blk in set_slices(m01)], axis=1)

    def search(i, cur):
        cand = cur | jnp.left_shift(jnp.int32(1), 30 - i)
        cnt = per_set_sum(jnp.where(bits >= cand, 1.0, 0.0))
        return jnp.where(cnt >= cap, cand, cur)

    thr = lax.fori_loop(0, 31, search, jnp.zeros((n_exp, 1 if n_sets == 1 else t), jnp.int32))
    gt = bits > thr
    eq = bits == thr
    need = cap - per_set_sum(jnp.where(gt, 1.0, 0.0))
    eq_rank = per_set_prefix(jnp.where(eq, 1.0, 0.0))
    sel = gt | (eq & (eq_rank < need))
    pos = per_set_prefix(jnp.where(sel, 1.0, 0.0))
    aff_ref[...] = aff
    pos_ref[...] = jnp.where(sel, pos, -1.0).astype(jnp.int32)

    tok = lax.broadcasted_iota(jnp.int32, (1, tchunk), 1)
    slot = lax.broadcasted_iota(jnp.int32, (cap, tchunk), 0)
    feat_rows = 2 * SUBLANES
    zero_rows = jnp.zeros((feat_rows - 5, tchunk), F32)

    def per_expert(e, carry):
        for s in range(n_sets):
            res = jnp.zeros((feat_rows, cap), F32)
            for j in range(set_len // tchunk):
                cols = pl.ds(s * set_len + j * tchunk, tchunk)
                a = aff_ref[pl.ds(e, 1), cols]
                a_hi = a.astype(BF16).astype(F32)
                a_mid = (a - a_hi).astype(BF16).astype(F32)
                a_lo = a - a_hi - a_mid
                tj = tok + j * tchunk
                feats = jnp.concatenate(
                    [jnp.right_shift(tj, 6).astype(F32), jnp.bitwise_and(tj, 63).astype(F32),
                     a_hi, a_mid, a_lo, zero_rows], axis=0).astype(BF16)
                onehot = jnp.where(pos_ref[pl.ds(e, 1), cols] == slot, 1.0, 0.0).astype(BF16)
                res = res + lax.dot_general(feats, onehot, NT_DIMS, preferred_element_type=F32)
            idx_ref[s, pl.ds(e, 1), :] = (res[0:1] * 64.0 + res[1:2]).astype(jnp.int32)
            gate_ref[s, pl.ds(e, 1), :] = res[2:3] + res[3:4] + res[4:5]
        return carry

    lax.fori_loop(0, n_exp, per_expert, 0)


def route(logits_t, set_len, col0, n_sets, sets_per_step):
    n_exp = logits_t.shape[0]
    cap = CAPACITY_FACTOR * set_len // n_exp
    width = sets_per_step * set_len
    blk0 = col0 // width
    out_spec = pl.BlockSpec((sets_per_step, n_exp, cap), lambda s: (s, 0, 0))
    return pl.pallas_call(
        functools.partial(_route_kernel, cap=cap, set_len=set_len, tchunk=min(set_len, 1024)),
        out_shape=[jax.ShapeDtypeStruct((n_sets, n_exp, cap), jnp.int32),
                   jax.ShapeDtypeStruct((n_sets, n_exp, cap), F32)],
        grid=(n_sets // sets_per_step,),
        in_specs=[pl.BlockSpec((n_exp, width), lambda s: (0, blk0 + s))],
        out_specs=[out_spec, out_spec],
        scratch_shapes=[pltpu.VMEM((n_exp, width), F32), pltpu.VMEM((n_exp, width), jnp.int32)],
        compiler_params=_params("arbitrary"),
        name="route",
    )(logits_t)


GATHER_ROWS = 16


def _gather_kernel(idx_ref, h_ref, o_ref, *, n_groups, slots):
    g = pl.program_id(0)
    e = pl.program_id(1)
    base = (e * n_groups + g) * slots

    def body(j, carry):
        s0 = pl.multiple_of(j * GATHER_ROWS, GATHER_ROWS)
        rows = [h_ref[0, pl.ds(idx_ref[base + s0 + i], 1), :] for i in range(GATHER_ROWS)]
        o_ref[0, 0, pl.ds(s0, GATHER_ROWS), :] = jnp.concatenate(rows, axis=0).astype(o_ref.dtype)
        return carry

    lax.fori_loop(0, slots // GATHER_ROWS, body, 0)


def moe_gather(idx_flat, h, n_exp, n_groups, slots):
    _, tg, d = h.shape
    return pl.pallas_call(
        functools.partial(_gather_kernel, n_groups=n_groups, slots=slots),
        out_shape=jax.ShapeDtypeStruct((n_exp, n_groups, slots, d), BF16),
        grid_spec=pltpu.PrefetchScalarGridSpec(
            num_scalar_prefetch=1,
            grid=(n_groups, n_exp),
            in_specs=[pl.BlockSpec((1, tg, d), lambda g, e, idx: (g, 0, 0))],
            out_specs=pl.BlockSpec((1, 1, slots, d), lambda g, e, idx: (e, g, 0, 0)),
        ),
        compiler_params=_params("arbitrary", "arbitrary"),
        name="moe_gather",
    )(idx_flat, h)


def _ffn_kernel(xs_ref, wg_ref, wu_ref, wd_ref, gt_ref, o_ref, *, tm):
    f = pl.program_id(1)
    wg = wg_ref[0, 0].astype(BF16)
    wu = wu_ref[0, 0].astype(BF16)
    wd = wd_ref[0, 0].astype(BF16)

    @pl.when(f == 0)
    def _():
        o_ref[...] = jnp.zeros_like(o_ref)

    for i in range(xs_ref.shape[1] // tm):
        rows = slice(i * tm, (i + 1) * tm)
        x = xs_ref[0, rows, :]
        hid = _silu(jnp.dot(x, wg, preferred_element_type=F32)) * jnp.dot(
            x, wu, preferred_element_type=F32)
        o_ref[0, rows, :] += jnp.dot(hid.astype(BF16), wd, preferred_element_type=F32)

    @pl.when(f == pl.num_programs(1) - 1)
    def _():
        for i in range(xs_ref.shape[1] // LANES):
            g_rows = jnp.broadcast_to(gt_ref[0, :, i * LANES:(i + 1) * LANES], (LANES, LANES)).T
            for k in range(o_ref.shape[2] // LANES):
                blk = (0, slice(i * LANES, (i + 1) * LANES), slice(k * LANES, (k + 1) * LANES))
                o_ref[blk] = o_ref[blk] * g_rows


def moe_ffn(xs, w_gate, w_up, w_down, layer, gate_row, tf=512, tm=512):
    n_exp, m, d = xs.shape
    f = w_gate.shape[3]
    tf = min(tf, f)
    return pl.pallas_call(
        functools.partial(_ffn_kernel, tm=min(tm, m)),
        out_shape=jax.ShapeDtypeStruct((n_exp, m, d), F32),
        grid=(n_exp, f // tf),
        in_specs=[pl.BlockSpec((1, m, d), lambda e, j: (e, 0, 0)),
                  pl.BlockSpec((1, 1, d, tf), lambda e, j: (layer, e, 0, j)),
                  pl.BlockSpec((1, 1, d, tf), lambda e, j: (layer, e, 0, j)),
                  pl.BlockSpec((1, 1, tf, d), lambda e, j: (layer, e, j, 0)),
                  pl.BlockSpec((1, 1, m), lambda e, j: (e, 0, 0))],
        out_specs=pl.BlockSpec((1, m, d), lambda e, j: (e, 0, 0)),
        compiler_params=_params("arbitrary", "arbitrary"),
        name="moe_ffn",
    )(xs, w_gate, w_up, w_down, gate_row)


def _rows_to_tiles(tile_ref, rows):
    for k in range(tile_ref.shape[1]):
        tile_ref[:, k, :] = rows[:, k * LANES:(k + 1) * LANES]


def _tiles_to_rows(tile_ref):
    return jnp.concatenate([tile_ref[:, k, :] for k in range(tile_ref.shape[1])], axis=1)


COMBINE_UNROLL = 8


def _combine_kernel(idx_ref, ys_ref, o_ref, stage_ref, *, n_groups, slots):
    g = pl.program_id(0)
    e = pl.program_id(1)
    base = (e * n_groups + g) * slots

    @pl.when(e == 0)
    def _():
        o_ref[...] = jnp.zeros_like(o_ref)

    _rows_to_tiles(stage_ref, ys_ref[0, 0])

    def body(j, carry):
        s0 = pl.multiple_of(j * COMBINE_UNROLL, COMBINE_UNROLL)
        toks = [idx_ref[base + s0 + i] for i in range(COMBINE_UNROLL)]
        sums = [o_ref[0, t] + stage_ref[s0 + i] for i, t in enumerate(toks)]
        for t, v in zip(toks, sums):
            o_ref[0, t] = v
        return carry

    lax.fori_loop(0, slots // COMBINE_UNROLL, body, 0)


def moe_combine(idx_flat, ys, tg):
    n_exp, n_groups, slots, d = ys.shape
    c = d // LANES
    return pl.pallas_call(
        functools.partial(_combine_kernel, n_groups=n_groups, slots=slots),
        out_shape=jax.ShapeDtypeStruct((n_groups, tg, c, LANES), F32),
        grid_spec=pltpu.PrefetchScalarGridSpec(
            num_scalar_prefetch=1,
            grid=(n_groups, n_exp),
            in_specs=[pl.BlockSpec((1, 1, slots, d), lambda g, e, idx: (e, g, 0, 0))],
            out_specs=pl.BlockSpec((1, tg, c, LANES), lambda g, e, idx: (g, 0, 0, 0)),
            scratch_shapes=[pltpu.VMEM((slots, c, LANES), F32)],
        ),
        compiler_params=_params("arbitrary", "arbitrary"),
        name="moe_combine",
    )(idx_flat, ys)


def _moe_res_kernel(x_ref, y_ref, g_ref, nw_ref, sh_ref, sc_ref, x2_ref, h_ref):
    x2 = x_ref[...] + g_ref[0] * _tiles_to_rows(y_ref)
    x2_ref[...] = x2
    h_ref[...] = (_rms(x2, nw_ref[...]) * (1.0 + sc_ref[0]) + sh_ref[0]).astype(h_ref.dtype)


def moe_residual(x, y, gate, nw, shift, scale, tg, tm=512):
    m, d = x.shape
    per = tg // tm
    vec = pl.BlockSpec((1, 1, d), lambda i: (i // per, 0, 0))
    row = pl.BlockSpec((tm, d), lambda i: (i, 0))
    tiles = pl.BlockSpec((tm,) + y.shape[1:], lambda i: (i, 0, 0))
    return pl.pallas_call(
        _moe_res_kernel,
        out_shape=[jax.ShapeDtypeStruct((m, d), F32), jax.ShapeDtypeStruct((m, d), BF16)],
        grid=(m // tm,),
        in_specs=[row, tiles, vec, pl.BlockSpec((1, d), lambda i: (0, 0)), vec, vec],
        out_specs=[row, row],
        compiler_params=_params("arbitrary"),
        name="moe_residual",
    )(x, y, gate, nw.reshape(1, d), shift, scale)


def _final_kernel(x_ref, y_ref, g_ref, nw_ref, o_ref):
    o_ref[...] = _rms(x_ref[...] + g_ref[0] * _tiles_to_rows(y_ref), nw_ref[...])


def final_norm(x, y, gate, nw, tg, row0, rows, tm=512):
    d = x.shape[1]
    per = tg // tm
    off = row0 // tm
    row = pl.BlockSpec((tm, d), lambda i: (off + i, 0))
    return pl.pallas_call(
        _final_kernel,
        out_shape=jax.ShapeDtypeStruct((rows, d), F32),
        grid=(rows // tm,),
        in_specs=[row, pl.BlockSpec((tm,) + y.shape[1:], lambda i: (off + i, 0, 0)),
                  pl.BlockSpec((1, 1, d), lambda i: ((off + i) // per, 0, 0)),
                  pl.BlockSpec((1, d), lambda i: (0, 0))],
        out_specs=pl.BlockSpec((tm, d), lambda i: (i, 0)),
        compiler_params=_params("arbitrary"),
        name="final_norm",
    )(x, y, gate, nw.reshape(1, d))


CONV_HALO = 16
CONV_SUB = 256


def _shift_rows(ext3, o):
    rot = pltpu.roll(ext3, (SUBLANES - o) % SUBLANES, axis=1)
    r = lax.broadcasted_iota(jnp.int32, (ext3.shape[0] - 2,) + ext3.shape[1:], 1)
    if o > 0:
        return jnp.where(r < SUBLANES - o, rot[1:-1], rot[2:])
    return jnp.where(r >= -o, rot[1:-1], rot[:-2])


def _proj_conv_kernel(xp_ref, x_ref, xn_ref, w_ref, cw_ref, cb_ref, o_ref, wb_ref, *,
                      tm, seq_a, n_a, seq_b):
    i = pl.program_id(1)

    @pl.when(i == 0)
    def _():
        wb_ref[...] = w_ref[0].astype(BF16)

    wb = wb_ref[...]
    y = jnp.dot(x_ref[...], wb, preferred_element_type=F32)
    y_prev = jnp.dot(xp_ref[...], wb, preferred_element_type=F32)[CONV_HALO - SUBLANES:]
    y_next = jnp.dot(xn_ref[...], wb, preferred_element_type=F32)[:SUBLANES]
    row0 = i * tm
    seq = jnp.where(row0 < n_a, seq_a, seq_b)
    off = jnp.where(row0 < n_a, row0, row0 - n_a)
    half = D_CONV // 2
    sub = CONV_SUB
    n_sub = tm // sub
    for s in range(n_sub):
        cur = y[s * sub:(s + 1) * sub]
        prev = y_prev if s == 0 else y[s * sub - SUBLANES:s * sub]
        nxt = y_next if s == n_sub - 1 else y[(s + 1) * sub:(s + 1) * sub + SUBLANES]
        prev = jnp.where(lax.rem(off + s * sub, seq) == 0, 0.0, prev)
        nxt = jnp.where(lax.rem(off + (s + 1) * sub, seq) == 0, 0.0, nxt)
        ext3 = jnp.concatenate([prev, cur, nxt], axis=0).reshape(sub // SUBLANES + 2, SUBLANES, -1)
        acc = cb_ref[...] + cw_ref[half:half + 1, :] * cur
        for o in range(-half, half + 1):
            if o != 0:
                acc = acc + cw_ref[half + o:half + o + 1, :] * _shift_rows(ext3, o).reshape(sub, -1)
        o_ref[s * sub:(s + 1) * sub, :] = _silu(acc).astype(o_ref.dtype)


def proj_conv(x, w, layer, col0, cols, conv_w, conv_b, seq_a, n_a, seq_b, tm=1024, tn=1024):
    m, k = x.shape
    tn = max(t for t in range(LANES, min(tn, cols) + 1, LANES) if cols % t == 0 and col0 % t == 0)
    tm = min(tm, n_a)
    assert n_a % tm == 0 and m % tm == 0 and tm % CONV_SUB == 0
    assert seq_a % CONV_SUB == 0 and seq_b % CONV_SUB == 0
    c_off = col0 // tn
    hb = tm // CONV_HALO
    n_halo = m // CONV_HALO
    return pl.pallas_call(
        functools.partial(_proj_conv_kernel, tm=tm, seq_a=seq_a, n_a=n_a, seq_b=seq_b),
        out_shape=jax.ShapeDtypeStruct((m, cols), BF16),
        grid=(cols // tn, m // tm),
        in_specs=[pl.BlockSpec((CONV_HALO, k), lambda j, i: (jnp.maximum(i * hb - 1, 0), 0)),
                  pl.BlockSpec((tm, k), lambda j, i: (i, 0)),
                  pl.BlockSpec((CONV_HALO, k), lambda j, i: (jnp.minimum((i + 1) * hb, n_halo - 1), 0)),
                  pl.BlockSpec((1, k, tn), lambda j, i: (layer, 0, c_off + j)),
                  pl.BlockSpec((D_CONV, tn), lambda j, i: (0, j)),
                  pl.BlockSpec((1, tn), lambda j, i: (0, j))],
        out_specs=pl.BlockSpec((tm, tn), lambda j, i: (i, j)),
        scratch_shapes=[pltpu.VMEM((k, tn), BF16)],
        compiler_params=_params("arbitrary", "arbitrary"),
        name="proj_conv",
    )(x, x, x, w, conv_w, conv_b.reshape(1, cols))


def _softplus(x):
    return jnp.maximum(x, 0.0) + jnp.log(1.0 + jnp.exp(-jnp.abs(x)))


DT_REPLICAS = 3


def _split3(v, lane, n_heads):
    hi = v.astype(BF16)
    r1 = v - hi.astype(F32)
    mid = r1.astype(BF16)
    lo = (r1 - mid.astype(F32)).astype(BF16)
    zero = jnp.zeros_like(lo)
    return jnp.where(lane < n_heads, hi, jnp.where(
        lane < 2 * n_heads, mid, jnp.where(lane < DT_REPLICAS * n_heads, lo, zero)))


def _scan_kernel(rb_ref, dir_ref, first_ref, last_ref, zero_ref, s0i_ref, soi_ref,
                 xbc_ref, dt_ref, dtb_ref, alog_ref, dsk_ref, e64_ref, e128_ref, s0_ref,
                 y_ref, sout_ref, st_ref, *, n_heads):
    i = pl.program_id(0)
    q = xbc_ref.shape[0]
    p, n, g_n = SSD_HEADDIM, D_STATE, N_GROUPS
    hpg = n_heads // g_n
    d_inner = n_heads * p
    fwd = dir_ref[i] == 0

    @pl.when((first_ref[i] == 1) & (zero_ref[i] == 1))
    def _():
        st_ref[...] = jnp.zeros_like(st_ref)

    @pl.when((first_ref[i] == 1) & (zero_ref[i] == 0))
    def _():
        for g in range(g_n):
            st_ref[g] = s0_ref[0, 0, g].T

    dt = _softplus(dt_ref[...] + dtb_ref[0])
    da = dt * (-LOG2E * jnp.exp(alog_ref[0]))
    li = lax.broadcasted_iota(jnp.int32, (q, q), 0)
    si = lax.broadcasted_iota(jnp.int32, (q, q), 1)
    tri = (li - si) * jnp.where(fwd, 1, -1) >= 0
    lane = lax.broadcasted_iota(jnp.int32, (q, LANES), 1)
    parts = jnp.dot(jnp.where(tri, 1.0, 0.0).astype(BF16), _split3(da, lane, n_heads),
                    preferred_element_type=F32)
    total = (parts + pltpu.roll(parts, LANES - n_heads, axis=1)
             + pltpu.roll(parts, LANES - 2 * n_heads, axis=1))
    acs = jnp.where(lane < n_heads, total,
                    jnp.where(lane < 2 * n_heads, pltpu.roll(total, n_heads, axis=1),
                              pltpu.roll(total, 2 * n_heads, axis=1)))
    acs_t = acs.T

    f_dt = _split3(dt, lane, n_heads)
    f_acs = _split3(acs, lane, n_heads)
    dt_all = jnp.dot(f_dt, e64_ref[...], preferred_element_type=F32)
    acs_all = jnp.dot(f_acs, e64_ref[...], preferred_element_type=F32)
    acs_end = jnp.where(fwd, acs_all[q - 1:q, :], acs_all[0:1, :])
    e_in = jnp.exp2(acs_all)
    dec = jnp.exp2(acs_end)
    x = xbc_ref[:, :d_inner].astype(F32)
    xdt = x * dt_all
    skip = x * jnp.where(fwd, dsk_ref[...], 0.0)
    xdt_b = xdt.astype(BF16)
    xd_b = (xdt * jnp.exp2(acs_end - acs_all)).astype(BF16)
    first_head = lax.broadcasted_iota(jnp.int32, (q, 2 * p), 1) < p

    for g in range(g_n):
        gcols = slice(g * hpg * p, (g + 1) * hpg * p)
        bm = xbc_ref[:, d_inner + g * n:d_inner + (g + 1) * n]
        cm = xbc_ref[:, d_inner + g_n * n + g * n:d_inner + g_n * n + (g + 1) * n]
        cb = lax.dot_general(cm, bm, NT_DIMS, preferred_element_type=F32)
        st = st_ref[g]
        y_off = jnp.dot(cm, st.astype(BF16), preferred_element_type=F32) * e_in[:, gcols]
        a_g = jnp.dot(f_acs, e128_ref[:, g * hpg * q:(g + 1) * hpg * q],
                      preferred_element_type=F32)
        for e2 in range(hpg // 2):
            h0 = g * hpg + 2 * e2
            pc = slice(h0 * p, (h0 + 2) * p)
            ms = []
            for k in range(2):
                e = 2 * e2 + k
                seg = a_g[:, e * q:(e + 1) * q] - acs_t[h0 + k:h0 + k + 1, :]
                ms.append((jnp.where(tri, jnp.exp2(seg), 0.0) * cb).astype(BF16))
            yd = jnp.dot(jnp.concatenate(ms, axis=0), xdt_b[:, pc], preferred_element_type=F32)
            y = (jnp.where(first_head, yd[:q], yd[q:]) + y_off[:, 2 * e2 * p:(2 * e2 + 2) * p]
                 + skip[:, pc])
            y_ref[0, :, pc] = y.astype(y_ref.dtype)
        bm_t = bm.astype(F32).T.astype(BF16)
        s_new = jnp.dot(bm_t, xd_b[:, gcols], preferred_element_type=F32)
        st_ref[g] = st * dec[:, gcols] + s_new

    @pl.when(last_ref[i] == 1)
    def _():
        for g in range(g_n):
            sout_ref[0, 0, g] = st_ref[g].T


def _expansion(n_heads, width):
    row_head = jnp.arange(LANES) % n_heads
    row_ok = jnp.arange(LANES) < DT_REPLICAS * n_heads
    col_head = jnp.arange(n_heads * width) // width
    return ((row_head[:, None] == col_head[None, :]) & row_ok[:, None]).astype(BF16)


def ssd_scan(xconv, dt2, dtb2, alog2, d_cols, s0_t, n_heads, seqs):
    m, c = xconv.shape
    q = SSD_CHUNK
    hp = n_heads // N_GROUPS * SSD_HEADDIM
    d_inner = n_heads * SSD_HEADDIM
    assert DT_REPLICAS * n_heads <= LANES and 2 * SSD_HEADDIM == LANES == q
    assert (n_heads // N_GROUPS) % 2 == 0
    tabs = [[] for _ in range(7)]
    for row0, length, s0_slot, out_slot in seqs:
        nc = length // q
        for d in range(2):
            for ci in range(nc):
                vals = (row0 // q + (ci if d == 0 else nc - 1 - ci), d, int(ci == 0),
                        int(ci == nc - 1), int(s0_slot is None), s0_slot or 0, out_slot)
                for t, v in zip(tabs, vals):
                    t.append(v)
    n_steps = len(tabs[0])
    n_out = max(t[3] for t in seqs) + 1
    tabs = [jnp.asarray(t, jnp.int32) for t in tabs]

    st_block = (1, 1, N_GROUPS, hp, D_STATE)
    const = lambda i, *_: (0, 0)
    return pl.pallas_call(
        functools.partial(_scan_kernel, n_heads=n_heads),
        out_shape=[jax.ShapeDtypeStruct((2, m, d_inner), BF16),
                   jax.ShapeDtypeStruct((2, n_out, N_GROUPS, hp, D_STATE), F32)],
        grid_spec=pltpu.PrefetchScalarGridSpec(
            num_scalar_prefetch=7,
            grid=(n_steps,),
            in_specs=[pl.BlockSpec((q, c), lambda i, rb, dr, *_: (rb[i], 0)),
                      pl.BlockSpec((q, LANES), lambda i, rb, dr, *_: (rb[i], dr[i])),
                      pl.BlockSpec((1, 1, LANES), lambda i, rb, dr, *_: (dr[i], 0, 0)),
                      pl.BlockSpec((1, 1, LANES), lambda i, rb, dr, *_: (dr[i], 0, 0)),
                      pl.BlockSpec((1, d_inner), const),
                      pl.BlockSpec((LANES, d_inner), const),
                      pl.BlockSpec((LANES, n_heads * q), const),
                      pl.BlockSpec(st_block, lambda i, rb, dr, f, l, z, s0i, soi: (dr[i], s0i[i], 0, 0, 0))],
            out_specs=[pl.BlockSpec((1, q, d_inner), lambda i, rb, dr, *_: (dr[i], rb[i], 0)),
                       pl.BlockSpec(st_block, lambda i, rb, dr, f, l, z, s0i, soi: (dr[i], soi[i], 0, 0, 0))],
            scratch_shapes=[pltpu.VMEM((N_GROUPS, D_STATE, hp), F32)],
        ),
        compiler_params=_params("arbitrary"),
        name="ssd_scan",
    )(*tabs, xconv, dt2, dtb2, alog2, d_cols.reshape(1, d_inner), _expansion(n_heads, SSD_HEADDIM),
      _expansion(n_heads, q), s0_t)


def _replicate_heads(v, n_heads):
    lead = v.shape[:-1]
    v = v.astype(F32).reshape(lead + (2, 1, n_heads))
    v = jnp.broadcast_to(v, lead + (2, DT_REPLICAS, n_heads)).reshape(lead + (2, DT_REPLICAS * n_heads))
    pad = [(0, 0)] * (len(lead) + 1) + [(0, LANES - DT_REPLICAS * n_heads)]
    return jnp.pad(v, pad).reshape(lead + (2 * LANES,))


def _state_groups(s, n_heads):
    return s.reshape(s.shape[0], N_GROUPS, n_heads // N_GROUPS * SSD_HEADDIM, D_STATE)


def _moe(h, logits_t, w_gate, w_up, w_down, layer, dims):
    n_req, seq, n_lat, tg, d = dims
    n_groups = 1 + n_lat
    n_exp = logits_t.shape[0]
    idx_p, gate_p = route(logits_t, seq, 0, n_req, n_req)
    idx_s, gate_s = route(logits_t, tg, tg, n_lat, 1)
    slots = idx_s.shape[2]
    idx_p = idx_p + (jnp.arange(n_req, dtype=jnp.int32) * seq)[:, None, None]
    idx = jnp.concatenate([idx_p.transpose(1, 0, 2).reshape(n_exp, 1, slots),
                           idx_s.transpose(1, 0, 2)], axis=1)
    gate = jnp.concatenate([gate_p.transpose(1, 0, 2).reshape(n_exp, 1, slots),
                            gate_s.transpose(1, 0, 2)], axis=1)
    idx_flat = idx.reshape(-1)
    xs = moe_gather(idx_flat, h.reshape(n_groups, tg, d), n_exp, n_groups, slots)
    ys = moe_ffn(xs.reshape(n_exp, n_groups * slots, d), w_gate, w_up, w_down, layer,
                 gate.reshape(n_exp, 1, n_groups * slots))
    out = moe_combine(idx_flat, ys.reshape(n_exp, n_groups, slots, d), tg)
    return out.reshape((n_groups * tg,) + out.shape[2:])


def kernel(x_prompt, x_sample, cache_k_na, cache_v_na, state_ssd_fwd, state_ssd_bwd, c, c_ctx, ada_w, ada_b, norm1_w, norm2_w, final_norm_w, na_qkv_w, na_out_w, na_rpb, ssd_in_w, ssd_conv_w, ssd_conv_b, ssd_dt_bias, ssd_a_log, ssd_d, ssd_norm_w, ssd_out_w, router_w, moe_w_gate, moe_w_up, moe_w_down):
    n_req, seq, d = x_prompt.shape
    n_lat, tg, _ = x_sample.shape
    assert n_req * seq == tg, "token groups must have equal size"
    assert n_lat + 1 <= MOD_ROWS
    depth = ada_w.shape[0]
    n_groups = 1 + n_lat
    m_p = n_req * seq
    dims = (n_req, seq, n_lat, tg, d)

    cvec = jnp.zeros((MOD_ROWS, d), F32).at[0].set(c_ctx).at[1:1 + n_lat].set(c)
    mod = ada_modulation(cvec, ada_w, ada_b)
    mod = mod.reshape(depth, MOD_ROWS, N_MOD, 1, d).transpose(0, 2, 1, 3, 4)

    m_s = n_lat * tg
    x_first, x_second, x_second_row0 = x_prompt.reshape(m_p, d), x_sample.reshape(m_s, d), 0
    h = prenorm(x_first, x_second, norm1_w[0], mod[0, 0], mod[0, 1], tg, BF16)

    new_k, new_v, new_sf, new_sb = [], [], [], []
    y_prompt = y_sample = None
    for layer in range(depth):
        j = layer // 2
        ml = mod[layer]
        if layer % 2 == 0:
            q_p, k_p, v_p = (matmul(h, na_qkv_w, j, F32, rows=m_p, col0=i * d, cols=d)
                             for i in range(3))
            qkv_s = matmul(h, na_qkv_w, j, BF16, row0=m_p)
            new_k.append(k_p.reshape(n_req, seq, N_HEADS, d // N_HEADS))
            new_v.append(v_p.reshape(n_req, seq, N_HEADS, d // N_HEADS))
            att_p = ctx_attention(q_p, k_p, v_p, seq)
            past = cache_k_na.shape[2]
            att_s = na_attention(qkv_s.reshape(n_lat, tg, 3 * d),
                                 cache_k_na[:, j].reshape(n_lat, past, d).astype(BF16),
                                 cache_v_na[:, j].reshape(n_lat, past, d).astype(BF16),
                                 na_bias_table(na_rpb[j], tg // GRID_W))
            x1, hm, logits_t = attn_out(att_p, att_s.reshape(m_s, d), na_out_w, j, x_first, x_second,
                                        x_second_row0, ml[2], norm2_w[layer], ml[3], ml[4],
                                        router_w[layer], tg)
        else:
            n_heads = ssd_d.shape[1]
            d_inner = n_heads * SSD_HEADDIM
            conv_dim = ssd_conv_w.shape[2]
            zs = matmul(h, ssd_in_w, j, BF16, cols=d_inner, silu=True)
            xconv = proj_conv(h, ssd_in_w, j, d_inner, conv_dim, ssd_conv_w[j], ssd_conv_b[j],
                              seq, m_p, tg)
            w_dt = _replicate_heads(ssd_in_w[j, :, d_inner + conv_dim:], n_heads)
            dt2 = matmul(h, w_dt[None], 0, F32)
            dtb2 = _replicate_heads(ssd_dt_bias[j].reshape(-1), n_heads).reshape(2, 1, LANES)
            alog2 = _replicate_heads(ssd_a_log[j].reshape(-1), n_heads).reshape(2, 1, LANES)
            s0 = jnp.stack([_state_groups(state_ssd_fwd[:, j], n_heads),
                            _state_groups(state_ssd_bwd[:, j], n_heads)])
            seqs = [(r * seq, seq, None, r) for r in range(n_req)]
            seqs += [(m_p + b * tg, tg, b, n_req + b) for b in range(n_lat)]
            d_cols = jnp.repeat(ssd_d[j], SSD_HEADDIM)
            y2, st = ssd_scan(xconv, dt2, dtb2, alog2, d_cols, s0, n_heads, seqs)
            new_sf.append(st[0, :n_req].reshape(n_req, n_heads, SSD_HEADDIM, D_STATE))
            new_sb.append(st[1, :n_req].reshape(n_req, n_heads, SSD_HEADDIM, D_STATE))
            x1, hm, logits_t = ssd_out(y2, zs, ssd_norm_w[j], ssd_out_w, j, x_first,
                                       ml[2], norm2_w[layer], ml[3], ml[4], router_w[layer], tg)
        y_moe = _moe(hm, logits_t, moe_w_gate, moe_w_up, moe_w_down, layer, dims)
        if layer + 1 < depth:
            mn = mod[layer + 1]
            x, h = moe_residual(x1, y_moe, ml[5], norm1_w[layer + 1], mn[0], mn[1], tg)
            x_first, x_second, x_second_row0 = x, x, m_p
        else:
            y_prompt = final_norm(x1, y_moe, ml[5], final_norm_w, tg, 0, m_p)
            y_sample = final_norm(x1, y_moe, ml[5], final_norm_w, tg, m_p, m_s)

    return (y_prompt.reshape(n_req, seq, d), y_sample.reshape(n_lat, tg, d),
            jnp.stack(new_k, axis=1), jnp.stack(new_v, axis=1),
            jnp.stack(new_sf, axis=1), jnp.stack(new_sb, axis=1))
```

```python
import functools

import jax
import jax.numpy as jnp
from jax import lax
from jax.experimental import pallas as pl
from jax.experimental.pallas import tpu as pltpu

N_HEADS = 16
GRID_W = 64
WIN_H = 8
WIN_W = 16
SSD_HEADDIM = 64
N_GROUPS = 4
D_STATE = 128
D_CONV = 5
SSD_CHUNK = 128
CAPACITY_FACTOR = 2
N_MOD = 6
RMS_EPS = 1e-6

LANES = 128
SUBLANES = 8
MOD_ROWS = 8
VMEM_LIMIT = 56 * 1024 * 1024
NEG_INF = -1e30

F32 = jnp.float32
BF16 = jnp.bfloat16
HIGHEST = lax.Precision.HIGHEST
NT_DIMS = (((1,), (1,)), ((), ()))


def _params(*sem):
    return pltpu.CompilerParams(dimension_semantics=sem, vmem_limit_bytes=VMEM_LIMIT)


def _silu(x):
    return x / (1.0 + jnp.exp(-x))


def _rms(x, w):
    ms = jnp.mean(x * x, axis=-1, keepdims=True)
    return x * lax.rsqrt(ms + RMS_EPS) * w


def _ada_kernel(c_ref, w_ref, b_ref, o_ref):
    o_ref[0] = jnp.dot(_silu(c_ref[...]), w_ref[0], preferred_element_type=F32,
                       precision=HIGHEST) + b_ref[0]


def ada_modulation(cvec, ada_w, ada_b):
    n_layers, d, n = ada_w.shape
    tn = n // 4
    return pl.pallas_call(
        _ada_kernel,
        out_shape=jax.ShapeDtypeStruct((n_layers, MOD_ROWS, n), F32),
        grid=(n_layers, n // tn),
        in_specs=[pl.BlockSpec((MOD_ROWS, d), lambda l, j: (0, 0)),
                  pl.BlockSpec((1, d, tn), lambda l, j: (l, 0, j)),
                  pl.BlockSpec((1, 1, tn), lambda l, j: (l, 0, j))],
        out_specs=pl.BlockSpec((1, MOD_ROWS, tn), lambda l, j: (l, 0, j)),
        compiler_params=_params("arbitrary", "arbitrary"),
        name="ada_modulation",
    )(cvec, ada_w, ada_b.reshape(n_layers, 1, n))


def _two_source_specs(tm, width, n_first, off_second):
    first = pl.BlockSpec((tm, width), lambda i: (jnp.minimum(i, n_first - 1), 0))
    second = pl.BlockSpec((tm, width), lambda i: (jnp.maximum(i - n_first, 0) + off_second, 0))
    return [first, second]


def _pick(first_ref, second_ref, n_first):
    return jnp.where(pl.program_id(0) < n_first, first_ref[...], second_ref[...])


def _prenorm_kernel(xa_ref, xb_ref, nw_ref, sh_ref, sc_ref, h_ref, *, n_first):
    x = _pick(xa_ref, xb_ref, n_first)
    h = _rms(x, nw_ref[...]) * (1.0 + sc_ref[0]) + sh_ref[0]
    h_ref[...] = h.astype(h_ref.dtype)


def prenorm(xa, xb, nw, shift, scale, tg, out_dtype, tm=512):
    d = xa.shape[1]
    m = xa.shape[0] + xb.shape[0]
    per = tg // tm
    n_first = xa.shape[0] // tm
    vec = pl.BlockSpec((1, 1, d), lambda i: (i // per, 0, 0))
    return pl.pallas_call(
        functools.partial(_prenorm_kernel, n_first=n_first),
        out_shape=jax.ShapeDtypeStruct((m, d), out_dtype),
        grid=(m // tm,),
        in_specs=_two_source_specs(tm, d, n_first, 0) + [pl.BlockSpec((1, d), lambda i: (0, 0)), vec, vec],
        out_specs=pl.BlockSpec((tm, d), lambda i: (i, 0)),
        compiler_params=_params("arbitrary"),
        name="prenorm",
    )(xa, xb, nw.reshape(1, d), shift, scale)


def _mm_kernel(x_ref, w_ref, o_ref, wb_ref, *, silu):
    @pl.when(pl.program_id(1) == 0)
    def _():
        wb_ref[...] = w_ref[0].astype(BF16)

    y = jnp.dot(x_ref[...], wb_ref[...], preferred_element_type=F32)
    o_ref[...] = (_silu(y) if silu else y).astype(o_ref.dtype)


def matmul(x, w, layer, out_dtype, row0=0, rows=None, col0=0, cols=None, silu=False, tm=1024,
           tn=1024):
    k = x.shape[1]
    rows = x.shape[0] - row0 if rows is None else rows
    cols = w.shape[2] - col0 if cols is None else cols
    tn = max(t for t in range(LANES, min(tn, cols) + 1, LANES) if cols % t == 0 and col0 % t == 0)
    tm = min(tm, rows)
    r_off, c_off = row0 // tm, col0 // tn
    return pl.pallas_call(
        functools.partial(_mm_kernel, silu=silu),
        out_shape=jax.ShapeDtypeStruct((rows, cols), out_dtype),
        grid=(cols // tn, rows // tm),
        in_specs=[pl.BlockSpec((tm, k), lambda j, i: (r_off + i, 0)),
                  pl.BlockSpec((1, k, tn), lambda j, i: (layer, 0, c_off + j))],
        out_specs=pl.BlockSpec((tm, tn), lambda j, i: (i, j)),
        scratch_shapes=[pltpu.VMEM((k, tn), BF16)],
        compiler_params=_params("arbitrary", "arbitrary"),
        name="matmul",
    )(x, w)


LOG2E = 1.4426950408889634


def _softmax_pv(s_list, v_list):
    s = jnp.concatenate(s_list, axis=-1) if len(s_list) > 1 else s_list[0]
    p = jnp.exp2(s - s.max(axis=-1, keepdims=True))
    den = p.sum(axis=-1, keepdims=True)
    p = p.astype(BF16)
    acc = None
    col = 0
    for v in v_list:
        pv = jnp.dot(p[:, col:col + v.shape[0]], v, preferred_element_type=F32)
        acc = pv if acc is None else acc + pv
        col += v.shape[0]
    return acc / den


def _ctx_attn_kernel(q_ref, k_ref, v_ref, o_ref, *, n_heads):
    dh = q_ref.shape[1] // n_heads
    scale = dh ** -0.5 * LOG2E
    seq = q_ref.shape[0]
    first = lax.broadcasted_iota(jnp.int32, (seq, 2 * dh), 1) < dh
    for j in range(n_heads // 2):
        sl = slice(2 * j * dh, 2 * (j + 1) * dh)
        q2 = q_ref[:, sl].astype(BF16)
        qq = jnp.concatenate([jnp.where(first, q2, 0), jnp.where(first, 0, q2)], axis=0)
        k = k_ref[:, sl].astype(BF16)
        v = v_ref[:, sl].astype(BF16)
        s = lax.dot_general(qq, k, NT_DIMS, preferred_element_type=F32) * scale
        o2 = _softmax_pv([s], [v])
        o_ref[:, sl] = jnp.where(first, o2[:seq], o2[seq:]).astype(o_ref.dtype)


def ctx_attention(q, k, v, seq):
    m, d = q.shape
    blk = pl.BlockSpec((seq, d), lambda r: (r, 0))
    return pl.pallas_call(
        functools.partial(_ctx_attn_kernel, n_heads=N_HEADS),
        out_shape=jax.ShapeDtypeStruct((m, d), BF16),
        grid=(m // seq,),
        in_specs=[blk, blk, blk],
        out_specs=blk,
        compiler_params=_params("arbitrary"),
        name="ctx_attention",
    )(q, k, v)


def _na_row_start(r, rows, kh):
    return jnp.clip(r - kh // 2, 0, rows - kh)


def na_bias_table(rpb, rows):
    w, kw = GRID_W, WIN_W
    kh = min(WIN_H, rows)
    col = jnp.arange(w)
    cs = jnp.clip(col - kw // 2, 0, w - kw)
    dc = col[None, :] - col[:, None] + (WIN_W - 1)
    valid = (col[None, :] >= cs[:, None]) & (col[None, :] < cs[:, None] + kw)
    pick = (dc[None] == jnp.arange(2 * WIN_W - 1)[:, None, None]).astype(F32)
    c_full = jnp.einsum('hrd,dqk->hrqk', rpb, pick, precision=HIGHEST)
    c_full = jnp.where(valid[None, None], c_full * LOG2E, NEG_INF)
    n_off = WIN_H
    tab = jnp.stack([jnp.concatenate([c_full[:, off + j] for j in range(kh)], axis=-1)
                     for off in range(n_off)])
    return tab.reshape(n_off, rpb.shape[0] // 2, 2 * w, kh * w)


NA_ROWS_PER_STEP = 4


def _na_kernel(q_ref, k_ref, v_ref, kc_ref, vc_ref, *rest, n_heads, rows, kh, w):
    bias_refs, o_ref = rest[:-1], rest[-1]
    dh = q_ref.shape[2] // n_heads
    scale = dh ** -0.5 * LOG2E
    first = lax.broadcasted_iota(jnp.int32, (w, 2 * dh), 1) < dh
    for i, b_ref in enumerate(bias_refs):
        r0 = _na_row_start(pl.program_id(1) * len(bias_refs) + i, rows, kh)
        start = pl.multiple_of(r0 * w, w)
        qrows = slice(i * w, (i + 1) * w)
        for j in range(n_heads // 2):
            sl = slice(2 * j * dh, 2 * (j + 1) * dh)
            q2 = q_ref[0, qrows, sl]
            qq = jnp.concatenate([jnp.where(first, q2, 0), jnp.where(first, 0, q2)], axis=0)
            kwin = k_ref[0, pl.ds(start, kh * w), sl]
            vwin = v_ref[0, pl.ds(start, kh * w), sl]
            s_win = (lax.dot_general(qq, kwin, NT_DIMS, preferred_element_type=F32) * scale
                     + b_ref[0, j])
            s_ctx = lax.dot_general(qq, kc_ref[0, :, sl], NT_DIMS, preferred_element_type=F32) * scale
            o2 = _softmax_pv([s_win, s_ctx], [vwin, vc_ref[0, :, sl]])
            o_ref[0, qrows, sl] = jnp.where(first, o2[:w], o2[w:]).astype(o_ref.dtype)


def na_attention(qkv, k_ctx, v_ctx, bias):
    b, t, d3 = qkv.shape
    d = d3 // 3
    w = GRID_W
    rows = t // w
    kh = min(WIN_H, rows)
    n_ctx = k_ctx.shape[1]

    rps = NA_ROWS_PER_STEP
    assert rows % rps == 0 and 2 * (d // N_HEADS) == LANES

    def bias_spec(i):
        def bias_map(bi, s):
            r = s * rps + i
            return (_na_row_start(r, rows, kh) - r + WIN_H - 1, 0, 0, 0)
        return pl.BlockSpec((1, N_HEADS // 2, 2 * w, kh * w), bias_map)

    return pl.pallas_call(
        functools.partial(_na_kernel, n_heads=N_HEADS, rows=rows, kh=kh, w=w),
        out_shape=jax.ShapeDtypeStruct((b, t, d), BF16),
        grid=(b, rows // rps),
        in_specs=[pl.BlockSpec((1, rps * w, d), lambda bi, s: (bi, s, 0)),
                  pl.BlockSpec((1, t, d), lambda bi, s: (bi, 0, 1), pipeline_mode=pl.Buffered(1)),
                  pl.BlockSpec((1, t, d), lambda bi, s: (bi, 0, 2), pipeline_mode=pl.Buffered(1)),
                  pl.BlockSpec((1, n_ctx, d), lambda bi, s: (bi, 0, 0)),
                  pl.BlockSpec((1, n_ctx, d), lambda bi, s: (bi, 0, 0))]
        + [bias_spec(i) for i in range(rps)],
        out_specs=pl.BlockSpec((1, rps * w, d), lambda bi, s: (bi, s, 0)),
        compiler_params=_params("arbitrary", "arbitrary"),
        name="na_attention",
    )(qkv, qkv, qkv, k_ctx, v_ctx, *([bias] * rps))


def _residual_router_tail(mix, x, g_ref, nw_ref, sh_ref, sc_ref, rw_ref, x1_ref, h_ref, lg_ref):
    x1 = x + g_ref[0] * mix
    x1_ref[...] = x1
    h = _rms(x1, nw_ref[...]) * (1.0 + sc_ref[0]) + sh_ref[0]
    h_ref[...] = h
    n_exp = rw_ref.shape[0]
    rw = rw_ref[...]
    rw_hi = rw.astype(BF16)
    rw_lo = (rw - rw_hi.astype(F32)).astype(BF16)
    h_hi = h.astype(BF16)
    h_lo = (h - h_hi.astype(F32)).astype(BF16)
    both = lax.dot_general(jnp.concatenate([rw_hi, rw_lo], axis=0), h_hi, NT_DIMS,
                           preferred_element_type=F32)
    lg_ref[...] = (both[:n_exp] + both[n_exp:]
                   + lax.dot_general(rw_hi, h_lo, NT_DIMS, preferred_element_type=F32))


def _attn_out_kernel(aa_ref, ab_ref, w_ref, xa_ref, xb_ref, g_ref, nw_ref, sh_ref, sc_ref, rw_ref,
                     x1_ref, h_ref, lg_ref, wb_ref, *, n_first):
    @pl.when(pl.program_id(0) == 0)
    def _():
        wb_ref[...] = w_ref[0].astype(BF16)

    mix = jnp.dot(_pick(aa_ref, ab_ref, n_first), wb_ref[...], preferred_element_type=F32)
    _residual_router_tail(mix, _pick(xa_ref, xb_ref, n_first), g_ref, nw_ref, sh_ref, sc_ref,
                          rw_ref, x1_ref, h_ref, lg_ref)


def _ssd_out_kernel(yf_ref, yb_ref, zs_ref, snw_ref, w_ref, x_ref, g_ref, nw_ref,
                    sh_ref, sc_ref, rw_ref, x1_ref, h_ref, lg_ref, wb_ref):
    @pl.when(pl.program_id(0) == 0)
    def _():
        wb_ref[...] = w_ref[0].astype(BF16)

    y = yf_ref[0].astype(F32) + yb_ref[0].astype(F32)
    y = _rms(y * zs_ref[...].astype(F32), snw_ref[...])
    mix = jnp.dot(y.astype(BF16), wb_ref[...], preferred_element_type=F32)
    _residual_router_tail(mix, x_ref[...], g_ref, nw_ref, sh_ref, sc_ref, rw_ref, x1_ref, h_ref,
                          lg_ref)


def _tail_specs(d, n_exp, tm, per):
    vec = pl.BlockSpec((1, 1, d), lambda i: (i // per, 0, 0))
    row = pl.BlockSpec((tm, d), lambda i: (i, 0))
    in_specs = [vec, pl.BlockSpec((1, d), lambda i: (0, 0)), vec, vec,
                pl.BlockSpec((n_exp, d), lambda i: (0, 0))]
    out_specs = [row, row, pl.BlockSpec((n_exp, tm), lambda i: (0, i))]
    return in_specs, out_specs


def _tail_out_shape(m, d, n_exp):
    return [jax.ShapeDtypeStruct((m, d), F32), jax.ShapeDtypeStruct((m, d), F32),
            jax.ShapeDtypeStruct((n_exp, m), F32)]


def attn_out(a_first, a_second, w, layer, x_first, x_second, x_second_row0, gate, nw, shift, scale,
             router_w, tg, tm=512):
    k = a_first.shape[1]
    d = w.shape[2]
    n_first = a_first.shape[0] // tm
    m = a_first.shape[0] + a_second.shape[0]
    n_exp = router_w.shape[1]
    tail_in, tail_out = _tail_specs(d, n_exp, tm, tg // tm)
    return pl.pallas_call(
        functools.partial(_attn_out_kernel, n_first=n_first),
        out_shape=_tail_out_shape(m, d, n_exp),
        grid=(m // tm,),
        in_specs=_two_source_specs(tm, k, n_first, 0)
        + [pl.BlockSpec((1, k, d), lambda i: (layer, 0, 0))]
        + _two_source_specs(tm, d, n_first, x_second_row0 // tm) + tail_in,
        out_specs=tail_out,
        scratch_shapes=[pltpu.VMEM((k, d), BF16)],
        compiler_params=_params("arbitrary"),
        name="attn_out",
    )(a_first, a_second, w, x_first, x_second, gate, nw.reshape(1, d), shift, scale, router_w.T)


def ssd_out(y2, zs, snw, w, layer, x, gate, nw, shift, scale, router_w, tg, tm=512):
    _, k, d = w.shape
    m = x.shape[0]
    n_exp = router_w.shape[1]
    tail_in, tail_out = _tail_specs(d, n_exp, tm, tg // tm)
    return pl.pallas_call(
        _ssd_out_kernel,
        out_shape=_tail_out_shape(m, d, n_exp),
        grid=(m // tm,),
        in_specs=[pl.BlockSpec((1, tm, k), lambda i: (0, i, 0)),
                  pl.BlockSpec((1, tm, k), lambda i: (1, i, 0)),
                  pl.BlockSpec((tm, k), lambda i: (i, 0)),
                  pl.BlockSpec((1, k), lambda i: (0, 0)),
                  pl.BlockSpec((1, k, d), lambda i: (layer, 0, 0)),
                  pl.BlockSpec((tm, d), lambda i: (i, 0))] + tail_in,
        out_specs=tail_out,
        scratch_shapes=[pltpu.VMEM((k, d), BF16)],
        compiler_params=_params("arbitrary"),
        name="ssd_out",
    )(y2, y2, zs, snw.reshape(1, k), w, x, gate, nw.reshape(1, d), shift, scale, router_w.T)


def _excl_prefix_lanes(m01):
    e, t = m01.shape
    r = lax.broadcasted_iota(jnp.int32, (LANES, LANES), 0)
    c = lax.broadcasted_iota(jnp.int32, (LANES, LANES), 1)
    upper = jnp.where(r < c, 1.0, 0.0).astype(BF16)
    outs = []
    carry = jnp.zeros((e, 1), F32)
    for j in range(t // LANES):
        blk = m01[:, j * LANES:(j + 1) * LANES]
        outs.append(jnp.dot(blk.astype(BF16), upper, preferred_element_type=F32) + carry)
        carry = carry + blk.sum(axis=1, keepdims=True)
    return jnp.concatenate(outs, axis=1)


def _route_kernel(lg_ref, idx_ref, gate_ref, aff_ref, pos_ref, *, cap, set_len, tchunk):
    n_exp, t = lg_ref.shape
    n_sets = t // set_len
    lg = lg_ref[...]
    ex = jnp.exp(lg - lg.max(axis=0, keepdims=True))
    aff = ex / ex.sum(axis=0, keepdims=True)
    bits = pltpu.bitcast(aff, jnp.int32)

    def set_slices(x):
        return [x[:, s * set_len:(s + 1) * set_len] for s in range(n_sets)]

    def per_set_sum(m01):
        if n_sets == 1:
            return m01.sum(axis=1, keepdims=True)
        return jnp.concatenate([jnp.broadcast_to(blk.sum(axis=1, keepdims=True), (n_exp, set_len))
                                for blk in set_slices(m01)], axis=1)

    def per_set_prefix(m01):
        return jnp.concatenate([_excl_prefix_lanes(blk) for blk in set_slices(m01)], axis=1)

    def search(i, cur):
        cand = cur | jnp.left_shift(jnp.int32(1), 30 - i)
        cnt = per_set_sum(jnp.where(bits >= cand, 1.0, 0.0))
        return jnp.where(cnt >= cap, cand, cur)

    thr = lax.fori_loop(0, 31, search, jnp.zeros((n_exp, 1 if n_sets == 1 else t), jnp.int32))
    gt = bits > thr
    eq = bits == thr
    need = cap - per_set_sum(jnp.where(gt, 1.0, 0.0))
    eq_rank = per_set_prefix(jnp.where(eq, 1.0, 0.0))
    sel = gt | (eq & (eq_rank < need))
    pos = per_set_prefix(jnp.where(sel, 1.0, 0.0))
    aff_ref[...] = aff
    pos_ref[...] = jnp.where(sel, pos, -1.0).astype(jnp.int32)

    tok = lax.broadcasted_iota(jnp.int32, (1, tchunk), 1)
    slot = lax.broadcasted_iota(jnp.int32, (cap, tchunk), 0)
    feat_rows = 2 * SUBLANES
    zero_rows = jnp.zeros((feat_rows - 5, tchunk), F32)

    def per_expert(e, carry):
        for s in range(n_sets):
            res = jnp.zeros((feat_rows, cap), F32)
            for j in range(set_len // tchunk):
                cols = pl.ds(s * set_len + j * tchunk, tchunk)
                a = aff_ref[pl.ds(e, 1), cols]
                a_hi = a.astype(BF16).astype(F32)
                a_mid = (a - a_hi).astype(BF16).astype(F32)
                a_lo = a - a_hi - a_mid
                tj = tok + j * tchunk
                feats = jnp.concatenate(
                    [jnp.right_shift(tj, 6).astype(F32), jnp.bitwise_and(tj, 63).astype(F32),
                     a_hi, a_mid, a_lo, zero_rows], axis=0).astype(BF16)
                onehot = jnp.where(pos_ref[pl.ds(e, 1), cols] == slot, 1.0, 0.0).astype(BF16)
                res = res + lax.dot_general(feats, onehot, NT_DIMS, preferred_element_type=F32)
            idx_ref[s, pl.ds(e, 1), :] = (res[0:1] * 64.0 + res[1:2]).astype(jnp.int32)
            gate_ref[s, pl.ds(e, 1), :] = res[2:3] + res[3:4] + res[4:5]
        return carry

    lax.fori_loop(0, n_exp, per_expert, 0)


def route(logits_t, set_len, col0, n_sets, sets_per_step):
    n_exp = logits_t.shape[0]
    cap = CAPACITY_FACTOR * set_len // n_exp
    width = sets_per_step * set_len
    blk0 = col0 // width
    out_spec = pl.BlockSpec((sets_per_step, n_exp, cap), lambda s: (s, 0, 0))
    return pl.pallas_call(
        functools.partial(_route_kernel, cap=cap, set_len=set_len, tchunk=min(set_len, 1024)),
        out_shape=[jax.ShapeDtypeStruct((n_sets, n_exp, cap), jnp.int32),
                   jax.ShapeDtypeStruct((n_sets, n_exp, cap), F32)],
        grid=(n_sets // sets_per_step,),
        in_specs=[pl.BlockSpec((n_exp, width), lambda s: (0, blk0 + s))],
        out_specs=[out_spec, out_spec],
        scratch_shapes=[pltpu.VMEM((n_exp, width), F32), pltpu.VMEM((n_exp, width), jnp.int32)],
        compiler_params=_params("arbitrary"),
        name="route",
    )(logits_t)


GATHER_ROWS = 16


def _gather_kernel(idx_ref, h_ref, o_ref, *, n_groups, slots):
    g = pl.program_id(0)
    e = pl.program_id(1)
    base = (e * n_groups + g) * slots

    def body(j, carry):
        s0 = pl.multiple_of(j * GATHER_ROWS, GATHER_ROWS)
        rows = [h_ref[0, pl.ds(idx_ref[base + s0 + i], 1), :] for i in range(GATHER_ROWS)]
        o_ref[0, 0, pl.ds(s0, GATHER_ROWS), :] = jnp.concatenate(rows, axis=0).astype(o_ref.dtype)
        return carry

    lax.fori_loop(0, slots // GATHER_ROWS, body, 0)


def moe_gather(idx_flat, h, n_exp, n_groups, slots):
    _, tg, d = h.shape
    return pl.pallas_call(
        functools.partial(_gather_kernel, n_groups=n_groups, slots=slots),
        out_shape=jax.ShapeDtypeStruct((n_exp, n_groups, slots, d), BF16),
        grid_spec=pltpu.PrefetchScalarGridSpec(
            num_scalar_prefetch=1,
            grid=(n_groups, n_exp),
            in_specs=[pl.BlockSpec((1, tg, d), lambda g, e, idx: (g, 0, 0))],
            out_specs=pl.BlockSpec((1, 1, slots, d), lambda g, e, idx: (e, g, 0, 0)),
        ),
        compiler_params=_params("arbitrary", "arbitrary"),
        name="moe_gather",
    )(idx_flat, h)


def _ffn_kernel(xs_ref, wg_ref, wu_ref, wd_ref, gt_ref, o_ref, *, tm):
    f = pl.program_id(1)
    wg = wg_ref[0, 0].astype(BF16)
    wu = wu_ref[0, 0].astype(BF16)
    wd = wd_ref[0, 0].astype(BF16)

    @pl.when(f == 0)
    def _():
        o_ref[...] = jnp.zeros_like(o_ref)

    for i in range(xs_ref.shape[1] // tm):
        rows = slice(i * tm, (i + 1) * tm)
        x = xs_ref[0, rows, :]
        hid = _silu(jnp.dot(x, wg, preferred_element_type=F32)) * jnp.dot(
            x, wu, preferred_element_type=F32)
        o_ref[0, rows, :] += jnp.dot(hid.astype(BF16), wd, preferred_element_type=F32)

    @pl.when(f == pl.num_programs(1) - 1)
    def _():
        for i in range(xs_ref.shape[1] // LANES):
            g_rows = jnp.broadcast_to(gt_ref[0, :, i * LANES:(i + 1) * LANES], (LANES, LANES)).T
            for k in range(o_ref.shape[2] // LANES):
                blk = (0, slice(i * LANES, (i + 1) * LANES), slice(k * LANES, (k + 1) * LANES))
                o_ref[blk] = o_ref[blk] * g_rows


def moe_ffn(xs, w_gate, w_up, w_down, layer, gate_row, tf=512, tm=512):
    n_exp, m, d = xs.shape
    f = w_gate.shape[3]
    tf = min(tf, f)
    return pl.pallas_call(
        functools.partial(_ffn_kernel, tm=min(tm, m)),
        out_shape=jax.ShapeDtypeStruct((n_exp, m, d), F32),
        grid=(n_exp, f // tf),
        in_specs=[pl.BlockSpec((1, m, d), lambda e, j: (e, 0, 0)),
                  pl.BlockSpec((1, 1, d, tf), lambda e, j: (layer, e, 0, j)),
                  pl.BlockSpec((1, 1, d, tf), lambda e, j: (layer, e, 0, j)),
                  pl.BlockSpec((1, 1, tf, d), lambda e, j: (layer, e, j, 0)),
                  pl.BlockSpec((1, 1, m), lambda e, j: (e, 0, 0))],
        out_specs=pl.BlockSpec((1, m, d), lambda e, j: (e, 0, 0)),
        compiler_params=_params("arbitrary", "arbitrary"),
        name="moe_ffn",
    )(xs, w_gate, w_up, w_down, gate_row)


def _rows_to_tiles(tile_ref, rows):
    for k in range(tile_ref.shape[1]):
        tile_ref[:, k, :] = rows[:, k * LANES:(k + 1) * LANES]


def _tiles_to_rows(tile_ref):
    return jnp.concatenate([tile_ref[:, k, :] for k in range(tile_ref.shape[1])], axis=1)


COMBINE_UNROLL = 8


def _combine_kernel(idx_ref, ys_ref, o_ref, stage_ref, *, n_groups, slots):
    g = pl.program_id(0)
    e = pl.program_id(1)
    base = (e * n_groups + g) * slots

    @pl.when(e == 0)
    def _():
        o_ref[...] = jnp.zeros_like(o_ref)

    _rows_to_tiles(stage_ref, ys_ref[0, 0])

    def body(j, carry):
        s0 = pl.multiple_of(j * COMBINE_UNROLL, COMBINE_UNROLL)
        toks = [idx_ref[base + s0 + i] for i in range(COMBINE_UNROLL)]
        sums = [o_ref[0, t] + stage_ref[s0 + i] for i, t in enumerate(toks)]
        for t, v in zip(toks, sums):
            o_ref[0, t] = v
        return carry

    lax.fori_loop(0, slots // COMBINE_UNROLL, body, 0)


def moe_combine(idx_flat, ys, tg):
    n_exp, n_groups, slots, d = ys.shape
    c = d // LANES
    return pl.pallas_call(
        functools.partial(_combine_kernel, n_groups=n_groups, slots=slots),
        out_shape=jax.ShapeDtypeStruct((n_groups, tg, c, LANES), F32),
        grid_spec=pltpu.PrefetchScalarGridSpec(
            num_scalar_prefetch=1,
            grid=(n_groups, n_exp),
            in_specs=[pl.BlockSpec((1, 1, slots, d), lambda g, e, idx: (e, g, 0, 0))],
            out_specs=pl.BlockSpec((1, tg, c, LANES), lambda g, e, idx: (g, 0, 0, 0)),
            scratch_shapes=[pltpu.VMEM((slots, c, LANES), F32)],
        ),
        compiler_params=_params("arbitrary", "arbitrary"),
        name="moe_combine",
    )(idx_flat, ys)


def _moe_res_kernel(x_ref, y_ref, g_ref, nw_ref, sh_ref, sc_ref, x2_ref, h_ref):
    x2 = x_ref[...] + g_ref[0] * _tiles_to_rows(y_ref)
    x2_ref[...] = x2
    h_ref[...] = (_rms(x2, nw_ref[...]) * (1.0 + sc_ref[0]) + sh_ref[0]).astype(h_ref.dtype)


def moe_residual(x, y, gate, nw, shift, scale, tg, tm=512):
    m, d = x.shape
    per = tg // tm
    vec = pl.BlockSpec((1, 1, d), lambda i: (i // per, 0, 0))
    row = pl.BlockSpec((tm, d), lambda i: (i, 0))
    tiles = pl.BlockSpec((tm,) + y.shape[1:], lambda i: (i, 0, 0))
    return pl.pallas_call(
        _moe_res_kernel,
        out_shape=[jax.ShapeDtypeStruct((m, d), F32), jax.ShapeDtypeStruct((m, d), BF16)],
        grid=(m // tm,),
        in_specs=[row, tiles, vec, pl.BlockSpec((1, d), lambda i: (0, 0)), vec, vec],
        out_specs=[row, row],
        compiler_params=_params("arbitrary"),
        name="moe_residual",
    )(x, y, gate, nw.reshape(1, d), shift, scale)


def _final_kernel(x_ref, y_ref, g_ref, nw_ref, o_ref):
    o_ref[...] = _rms(x_ref[...] + g_ref[0] * _tiles_to_rows(y_ref), nw_ref[...])


def final_norm(x, y, gate, nw, tg, row0, rows, tm=512):
    d = x.shape[1]
    per = tg // tm
    off = row0 // tm
    row = pl.BlockSpec((tm, d), lambda i: (off + i, 0))
    return pl.pallas_call(
        _final_kernel,
        out_shape=jax.ShapeDtypeStruct((rows, d), F32),
        grid=(rows // tm,),
        in_specs=[row, pl.BlockSpec((tm,) + y.shape[1:], lambda i: (off + i, 0, 0)),
                  pl.BlockSpec((1, 1, d), lambda i: ((off + i) // per, 0, 0)),
                  pl.BlockSpec((1, d), lambda i: (0, 0))],
        out_specs=pl.BlockSpec((tm, d), lambda i: (i, 0)),
        compiler_params=_params("arbitrary"),
        name="final_norm",
    )(x, y, gate, nw.reshape(1, d))


CONV_HALO = 16
CONV_SUB = 256


def _shift_rows(ext3, o):
    rot = pltpu.roll(ext3, (SUBLANES - o) % SUBLANES, axis=1)
    r = lax.broadcasted_iota(jnp.int32, (ext3.shape[0] - 2,) + ext3.shape[1:], 1)
    if o > 0:
        return jnp.where(r < SUBLANES - o, rot[1:-1], rot[2:])
    return jnp.where(r >= -o, rot[1:-1], rot[:-2])


def _proj_conv_kernel(xp_ref, x_ref, xn_ref, w_ref, cw_ref, cb_ref, o_ref, wb_ref, *,
                      tm, seq_a, n_a, seq_b):
    i = pl.program_id(1)

    @pl.when(i == 0)
    def _():
        wb_ref[...] = w_ref[0].astype(BF16)

    wb = wb_ref[...]
    y = jnp.dot(x_ref[...], wb, preferred_element_type=F32)
    y_prev = jnp.dot(xp_ref[...], wb, preferred_element_type=F32)[CONV_HALO - SUBLANES:]
    y_next = jnp.dot(xn_ref[...], wb, preferred_element_type=F32)[:SUBLANES]
    row0 = i * tm
    seq = jnp.where(row0 < n_a, seq_a, seq_b)
    off = jnp.where(row0 < n_a, row0, row0 - n_a)
    half = D_CONV // 2
    sub = CONV_SUB
    n_sub = tm // sub
    for s in range(n_sub):
        cur = y[s * sub:(s + 1) * sub]
        prev = y_prev if s == 0 else y[s * sub - SUBLANES:s * sub]
        nxt = y_next if s == n_sub - 1 else y[(s + 1) * sub:(s + 1) * sub + SUBLANES]
        prev = jnp.where(lax.rem(off + s * sub, seq) == 0, 0.0, prev)
        nxt = jnp.where(lax.rem(off + (s + 1) * sub, seq) == 0, 0.0, nxt)
        ext3 = jnp.concatenate([prev, cur, nxt], axis=0).reshape(sub // SUBLANES + 2, SUBLANES, -1)
        acc = cb_ref[...] + cw_ref[half:half + 1, :] * cur
        for o in range(-half, half + 1):
            if o != 0:
                acc = acc + cw_ref[half + o:half + o + 1, :] * _shift_rows(ext3, o).reshape(sub, -1)
        o_ref[s * sub:(s + 1) * sub, :] = _silu(acc).astype(o_ref.dtype)


def proj_conv(x, w, layer, col0, cols, conv_w, conv_b, seq_a, n_a, seq_b, tm=1024, tn=1024):
    m, k = x.shape
    tn = max(t for t in range(LANES, min(tn, cols) + 1, LANES) if cols % t == 0 and col0 % t == 0)
    tm = min(tm, n_a)
    assert n_a % tm == 0 and m % tm == 0 and tm % CONV_SUB == 0
    assert seq_a % CONV_SUB == 0 and seq_b % CONV_SUB == 0
    c_off = col0 // tn
    hb = tm // CONV_HALO
    n_halo = m // CONV_HALO
    return pl.pallas_call(
        functools.partial(_proj_conv_kernel, tm=tm, seq_a=seq_a, n_a=n_a, seq_b=seq_b),
        out_shape=jax.ShapeDtypeStruct((m, cols), BF16),
        grid=(cols // tn, m // tm),
        in_specs=[pl.BlockSpec((CONV_HALO, k), lambda j, i: (jnp.maximum(i * hb - 1, 0), 0)),
                  pl.BlockSpec((tm, k), lambda j, i: (i, 0)),
                  pl.BlockSpec((CONV_HALO, k), lambda j, i: (jnp.minimum((i + 1) * hb, n_halo - 1), 0)),
                  pl.BlockSpec((1, k, tn), lambda j, i: (layer, 0, c_off + j)),
                  pl.BlockSpec((D_CONV, tn), lambda j, i: (0, j)),
                  pl.BlockSpec((1, tn), lambda j, i: (0, j))],
        out_specs=pl.BlockSpec((tm, tn), lambda j, i: (i, j)),
        scratch_shapes=[pltpu.VMEM((k, tn), BF16)],
        compiler_params=_params("arbitrary", "arbitrary"),
        name="proj_conv",
    )(x, x, x, w, conv_w, conv_b.reshape(1, cols))


def _softplus(x):
    return jnp.maximum(x, 0.0) + jnp.log(1.0 + jnp.exp(-jnp.abs(x)))


DT_REPLICAS = 3


def _split3(v, lane, n_heads):
    hi = v.astype(BF16)
    r1 = v - hi.astype(F32)
    mid = r1.astype(BF16)
    lo = (r1 - mid.astype(F32)).astype(BF16)
    zero = jnp.zeros_like(lo)
    return jnp.where(lane < n_heads, hi, jnp.where(
        lane < 2 * n_heads, mid, jnp.where(lane < DT_REPLICAS * n_heads, lo, zero)))


def _scan_kernel(rb_ref, dir_ref, first_ref, last_ref, zero_ref, s0i_ref, soi_ref,
                 xbc_ref, dt_ref, dtb_ref, alog_ref, dsk_ref, e64_ref, e128_ref, s0_ref,
                 y_ref, sout_ref, st_ref, *, n_heads):
    i = pl.program_id(0)
    q = xbc_ref.shape[0]
    p, n, g_n = SSD_HEADDIM, D_STATE, N_GROUPS
    hpg = n_heads // g_n
    d_inner = n_heads * p
    fwd = dir_ref[i] == 0

    @pl.when((first_ref[i] == 1) & (zero_ref[i] == 1))
    def _():
        st_ref[...] = jnp.zeros_like(st_ref)

    @pl.when((first_ref[i] == 1) & (zero_ref[i] == 0))
    def _():
        for g in range(g_n):
            st_ref[g] = s0_ref[0, 0, g].T

    dt = _softplus(dt_ref[...] + dtb_ref[0])
    da = dt * (-LOG2E * jnp.exp(alog_ref[0]))
    li = lax.broadcasted_iota(jnp.int32, (q, q), 0)
    si = lax.broadcasted_iota(jnp.int32, (q, q), 1)
    tri = (li - si) * jnp.where(fwd, 1, -1) >= 0
    lane = lax.broadcasted_iota(jnp.int32, (q, LANES), 1)
    parts = jnp.dot(jnp.where(tri, 1.0, 0.0).astype(BF16), _split3(da, lane, n_heads),
                    preferred_element_type=F32)
    total = (parts + pltpu.roll(parts, LANES - n_heads, axis=1)
             + pltpu.roll(parts, LANES - 2 * n_heads, axis=1))
    acs = jnp.where(lane < n_heads, total,
                    jnp.where(lane < 2 * n_heads, pltpu.roll(total, n_heads, axis=1),
                              pltpu.roll(total, 2 * n_heads, axis=1)))
    acs_t = acs.T

    f_dt = _split3(dt, lane, n_heads)
    f_acs = _split3(acs, lane, n_heads)
    dt_all = jnp.dot(f_dt, e64_ref[...], preferred_element_type=F32)
    acs_all = jnp.dot(f_acs, e64_ref[...], preferred_element_type=F32)
    acs_end = jnp.where(fwd, acs_all[q - 1:q, :], acs_all[0:1, :])
    e_in = jnp.exp2(acs_all)
    dec = jnp.exp2(acs_end)
    x = xbc_ref[:, :d_inner].astype(F32)
    xdt = x * dt_all
    skip = x * jnp.where(fwd, dsk_ref[...], 0.0)
    xdt_b = xdt.astype(BF16)
    xd_b = (xdt * jnp.exp2(acs_end - acs_all)).astype(BF16)
    first_head = lax.broadcasted_iota(jnp.int32, (q, 2 * p), 1) < p

    for g in range(g_n):
        gcols = slice(g * hpg * p, (g + 1) * hpg * p)
        bm = xbc_ref[:, d_inner + g * n:d_inner + (g + 1) * n]
        cm = xbc_ref[:, d_inner + g_n * n + g * n:d_inner + g_n * n + (g + 1) * n]
        cb = lax.dot_general(cm, bm, NT_DIMS, preferred_element_type=F32)
        st = st_ref[g]
        y_off = jnp.dot(cm, st.astype(BF16), preferred_element_type=F32) * e_in[:, gcols]
        a_g = jnp.dot(f_acs, e128_ref[:, g * hpg * q:(g + 1) * hpg * q],
                      preferred_element_type=F32)
        for e2 in range(hpg // 2):
            h0 = g * hpg + 2 * e2
            pc = slice(h0 * p, (h0 + 2) * p)
            ms = []
            for k in range(2):
                e = 2 * e2 + k
                seg = a_g[:, e * q:(e + 1) * q] - acs_t[h0 + k:h0 + k + 1, :]
                ms.append((jnp.where(tri, jnp.exp2(seg), 0.0) * cb).astype(BF16))
            yd = jnp.dot(jnp.concatenate(ms, axis=0), xdt_b[:, pc], preferred_element_type=F32)
            y = (jnp.where(first_head, yd[:q], yd[q:]) + y_off[:, 2 * e2 * p:(2 * e2 + 2) * p]
                 + skip[:, pc])
            y_ref[0, :, pc] = y.astype(y_ref.dtype)
        bm_t = bm.astype(F32).T.astype(BF16)
        s_new = jnp.dot(bm_t, xd_b[:, gcols], preferred_element_type=F32)
        st_ref[g] = st * dec[:, gcols] + s_new

    @pl.when(last_ref[i] == 1)
    def _():
        for g in range(g_n):
            sout_ref[0, 0, g] = st_ref[g].T


def _expansion(n_heads, width):
    row_head = jnp.arange(LANES) % n_heads
    row_ok = jnp.arange(LANES) < DT_REPLICAS * n_heads
    col_head = jnp.arange(n_heads * width) // width
    return ((row_head[:, None] == col_head[None, :]) & row_ok[:, None]).astype(BF16)


def ssd_scan(xconv, dt2, dtb2, alog2, d_cols, s0_t, n_heads, seqs):
    m, c = xconv.shape
    q = SSD_CHUNK
    hp = n_heads // N_GROUPS * SSD_HEADDIM
    d_inner = n_heads * SSD_HEADDIM
    assert DT_REPLICAS * n_heads <= LANES and 2 * SSD_HEADDIM == LANES == q
    assert (n_heads // N_GROUPS) % 2 == 0
    tabs = [[] for _ in range(7)]
    for row0, length, s0_slot, out_slot in seqs:
        nc = length // q
        for d in range(2):
            for ci in range(nc):
                vals = (row0 // q + (ci if d == 0 else nc - 1 - ci), d, int(ci == 0),
                        int(ci == nc - 1), int(s0_slot is None), s0_slot or 0, out_slot)
                for t, v in zip(tabs, vals):
                    t.append(v)
    n_steps = len(tabs[0])
    n_out = max(t[3] for t in seqs) + 1
    tabs = [jnp.asarray(t, jnp.int32) for t in tabs]

    st_block = (1, 1, N_GROUPS, hp, D_STATE)
    const = lambda i, *_: (0, 0)
    return pl.pallas_call(
        functools.partial(_scan_kernel, n_heads=n_heads),
        out_shape=[jax.ShapeDtypeStruct((2, m, d_inner), BF16),
                   jax.ShapeDtypeStruct((2, n_out, N_GROUPS, hp, D_STATE), F32)],
        grid_spec=pltpu.PrefetchScalarGridSpec(
            num_scalar_prefetch=7,
            grid=(n_steps,),
            in_specs=[pl.BlockSpec((q, c), lambda i, rb, dr, *_: (rb[i], 0)),
                      pl.BlockSpec((q, LANES), lambda i, rb, dr, *_: (rb[i], dr[i])),
                      pl.BlockSpec((1, 1, LANES), lambda i, rb, dr, *_: (dr[i], 0, 0)),
                      pl.BlockSpec((1, 1, LANES), lambda i, rb, dr, *_: (dr[i], 0, 0)),
                      pl.BlockSpec((1, d_inner), const),
                      pl.BlockSpec((LANES, d_inner), const),
                      pl.BlockSpec((LANES, n_heads * q), const),
                      pl.BlockSpec(st_block, lambda i, rb, dr, f, l, z, s0i, soi: (dr[i], s0i[i], 0, 0, 0))],
            out_specs=[pl.BlockSpec((1, q, d_inner), lambda i, rb, dr, *_: (dr[i], rb[i], 0)),
                       pl.BlockSpec(st_block, lambda i, rb, dr, f, l, z, s0i, soi: (dr[i], soi[i], 0, 0, 0))],
            scratch_shapes=[pltpu.VMEM((N_GROUPS, D_STATE, hp), F32)],
        ),
        compiler_params=_params("arbitrary"),
        name="ssd_scan",
    )(*tabs, xconv, dt2, dtb2, alog2, d_cols.reshape(1, d_inner), _expansion(n_heads, SSD_HEADDIM),
      _expansion(n_heads, q), s0_t)


def _replicate_heads(v, n_heads):
    lead = v.shape[:-1]
    v = v.astype(F32).reshape(lead + (2, 1, n_heads))
    v = jnp.broadcast_to(v, lead + (2, DT_REPLICAS, n_heads)).reshape(lead + (2, DT_REPLICAS * n_heads))
    pad = [(0, 0)] * (len(lead) + 1) + [(0, LANES - DT_REPLICAS * n_heads)]
    return jnp.pad(v, pad).reshape(lead + (2 * LANES,))


def _state_groups(s, n_heads):
    return s.reshape(s.shape[0], N_GROUPS, n_heads // N_GROUPS * SSD_HEADDIM, D_STATE)


def _moe(h, logits_t, w_gate, w_up, w_down, layer, dims):
    n_req, seq, n_lat, tg, d = dims
    n_groups = 1 + n_lat
    n_exp = logits_t.shape[0]
    idx_p, gate_p = route(logits_t, seq, 0, n_req, n_req)
    idx_s, gate_s = route(logits_t, tg, tg, n_lat, 1)
    slots = idx_s.shape[2]
    idx_p = idx_p + (jnp.arange(n_req, dtype=jnp.int32) * seq)[:, None, None]
    idx = jnp.concatenate([idx_p.transpose(1, 0, 2).reshape(n_exp, 1, slots),
                           idx_s.transpose(1, 0, 2)], axis=1)
    gate = jnp.concatenate([gate_p.transpose(1, 0, 2).reshape(n_exp, 1, slots),
                            gate_s.transpose(1, 0, 2)], axis=1)
    idx_flat = idx.reshape(-1)
    xs = moe_gather(idx_flat, h.reshape(n_groups, tg, d), n_exp, n_groups, slots)
    ys = moe_ffn(xs.reshape(n_exp, n_groups * slots, d), w_gate, w_up, w_down, layer,
                 gate.reshape(n_exp, 1, n_groups * slots))
    out = moe_combine(idx_flat, ys.reshape(n_exp, n_groups, slots, d), tg)
    return out.reshape((n_groups * tg,) + out.shape[2:])


def kernel(x_prompt, x_sample, cache_k_na, cache_v_na, state_ssd_fwd, state_ssd_bwd, c, c_ctx, ada_w, ada_b, norm1_w, norm2_w, final_norm_w, na_qkv_w, na_out_w, na_rpb, ssd_in_w, ssd_conv_w, ssd_conv_b, ssd_dt_bias, ssd_a_log, ssd_d, ssd_norm_w, ssd_out_w, router_w, moe_w_gate, moe_w_up, moe_w_down):
    n_req, seq, d = x_prompt.shape
    n_lat, tg, _ = x_sample.shape
    assert n_req * seq == tg, "token groups must have equal size"
    assert n_lat + 1 <= MOD_ROWS
    depth = ada_w.shape[0]
    n_groups = 1 + n_lat
    m_p = n_req * seq
    dims = (n_req, seq, n_lat, tg, d)

    cvec = jnp.zeros((MOD_ROWS, d), F32).at[0].set(c_ctx).at[1:1 + n_lat].set(c)
    mod = ada_modulation(cvec, ada_w, ada_b)
    mod = mod.reshape(depth, MOD_ROWS, N_MOD, 1, d).transpose(0, 2, 1, 3, 4)

    m_s = n_lat * tg
    x_first, x_second, x_second_row0 = x_prompt.reshape(m_p, d), x_sample.reshape(m_s, d), 0
    h = prenorm(x_first, x_second, norm1_w[0], mod[0, 0], mod[0, 1], tg, BF16)

    new_k, new_v, new_sf, new_sb = [], [], [], []
    y_prompt = y_sample = None
    for layer in range(depth):
        j = layer // 2
        ml = mod[layer]
        if layer % 2 == 0:
            q_p, k_p, v_p = (matmul(h, na_qkv_w, j, F32, rows=m_p, col0=i * d, cols=d)
                             for i in range(3))
            qkv_s = matmul(h, na_qkv_w, j, BF16, row0=m_p)
            new_k.append(k_p.reshape(n_req, seq, N_HEADS, d // N_HEADS))
            new_v.append(v_p.reshape(n_req, seq, N_HEADS, d // N_HEADS))
            att_p = ctx_attention(q_p, k_p, v_p, seq)
            past = cache_k_na.shape[2]
            att_s = na_attention(qkv_s.reshape(n_lat, tg, 3 * d),
                                 cache_k_na[:, j].reshape(n_lat, past, d).astype(BF16),
                                 cache_v_na[:, j].reshape(n_lat, past, d).astype(BF16),
                                 na_bias_table(na_rpb[j], tg // GRID_W))
            x1, hm, logits_t = attn_out(att_p, att_s.reshape(m_s, d), na_out_w, j, x_first, x_second,
                                        x_second_row0, ml[2], norm2_w[layer], ml[3], ml[4],
                                        router_w[layer], tg)
        else:
            n_heads = ssd_d.shape[1]
            d_inner = n_heads * SSD_HEADDIM
            conv_dim = ssd_conv_w.shape[2]
            zs = matmul(h, ssd_in_w, j, BF16, cols=d_inner, silu=True)
            xconv = proj_conv(h, ssd_in_w, j, d_inner, conv_dim, ssd_conv_w[j], ssd_conv_b[j],
                              seq, m_p, tg)
            w_dt = _replicate_heads(ssd_in_w[j, :, d_inner + conv_dim:], n_heads)
            dt2 = matmul(h, w_dt[None], 0, F32)
            dtb2 = _replicate_heads(ssd_dt_bias[j].reshape(-1), n_heads).reshape(2, 1, LANES)
            alog2 = _replicate_heads(ssd_a_log[j].reshape(-1), n_heads).reshape(2, 1, LANES)
            s0 = jnp.stack([_state_groups(state_ssd_fwd[:, j], n_heads),
                            _state_groups(state_ssd_bwd[:, j], n_heads)])
            seqs = [(r * seq, seq, None, r) for r in range(n_req)]
            seqs += [(m_p + b * tg, tg, b, n_req + b) for b in range(n_lat)]
            d_cols = jnp.repeat(ssd_d[j], SSD_HEADDIM)
            y2, st = ssd_scan(xconv, dt2, dtb2, alog2, d_cols, s0, n_heads, seqs)
            new_sf.append(st[0, :n_req].reshape(n_req, n_heads, SSD_HEADDIM, D_STATE))
            new_sb.append(st[1, :n_req].reshape(n_req, n_heads, SSD_HEADDIM, D_STATE))
            x1, hm, logits_t = ssd_out(y2, zs, ssd_norm_w[j], ssd_out_w, j, x_first,
                                       ml[2], norm2_w[layer], ml[3], ml[4], router_w[layer], tg)
        y_moe = _moe(hm, logits_t, moe_w_gate, moe_w_up, moe_w_down, layer, dims)
        if layer + 1 < depth:
            mn = mod[layer + 1]
            x, h = moe_residual(x1, y_moe, ml[5], norm1_w[layer + 1], mn[0], mn[1], tg)
            x_first, x_second, x_second_row0 = x, x, m_p
        else:
            y_prompt = final_norm(x1, y_moe, ml[5], final_norm_w, tg, 0, m_p)
            y_sample = final_norm(x1, y_moe, ml[5], final_norm_w, tg, m_p, m_s)

    return (y_prompt.reshape(n_req, seq, d), y_sample.reshape(n_lat, tg, d),
            jnp.stack(new_k, axis=1), jnp.stack(new_v, axis=1),
            jnp.stack(new_sf, axis=1), jnp.stack(new_sb, axis=1))
```

```python
import functools

import jax
import jax.numpy as jnp
from jax import lax
from jax.experimental import pallas as pl
from jax.experimental.pallas import tpu as pltpu

N_HEADS = 16
GRID_W = 64
WIN_H = 8
WIN_W = 16
SSD_HEADDIM = 64
N_GROUPS = 4
D_STATE = 128
D_CONV = 5
SSD_CHUNK = 128
CAPACITY_FACTOR = 2
N_MOD = 6
RMS_EPS = 1e-6

LANES = 128
SUBLANES = 8
MOD_ROWS = 8
VMEM_LIMIT = 56 * 1024 * 1024
NEG_INF = -1e30

F32 = jnp.float32
BF16 = jnp.bfloat16
HIGHEST = lax.Precision.HIGHEST
NT_DIMS = (((1,), (1,)), ((), ()))


def _params(*sem):
    return pltpu.CompilerParams(dimension_semantics=sem, vmem_limit_bytes=VMEM_LIMIT)


def _silu(x):
    return x / (1.0 + jnp.exp(-x))


def _rms(x, w):
    ms = jnp.mean(x * x, axis=-1, keepdims=True)
    return x * lax.rsqrt(ms + RMS_EPS) * w


def _ada_kernel(c_ref, w_ref, b_ref, o_ref):
    o_ref[0] = jnp.dot(_silu(c_ref[...]), w_ref[0], preferred_element_type=F32,
                       precision=HIGHEST) + b_ref[0]


def ada_modulation(cvec, ada_w, ada_b):
    n_layers, d, n = ada_w.shape
    tn = n // 4
    return pl.pallas_call(
        _ada_kernel,
        out_shape=jax.ShapeDtypeStruct((n_layers, MOD_ROWS, n), F32),
        grid=(n_layers, n // tn),
        in_specs=[pl.BlockSpec((MOD_ROWS, d), lambda l, j: (0, 0)),
                  pl.BlockSpec((1, d, tn), lambda l, j: (l, 0, j)),
                  pl.BlockSpec((1, 1, tn), lambda l, j: (l, 0, j))],
        out_specs=pl.BlockSpec((1, MOD_ROWS, tn), lambda l, j: (l, 0, j)),
        compiler_params=_params("arbitrary", "arbitrary"),
        name="ada_modulation",
    )(cvec, ada_w, ada_b.reshape(n_layers, 1, n))


def _two_source_specs(tm, width, n_first, off_second):
    first = pl.BlockSpec((tm, width), lambda i: (jnp.minimum(i, n_first - 1), 0))
    second = pl.BlockSpec((tm, width), lambda i: (jnp.maximum(i - n_first, 0) + off_second, 0))
    return [first, second]


def _pick(first_ref, second_ref, n_first):
    return jnp.where(pl.program_id(0) < n_first, first_ref[...], second_ref[...])


def _prenorm_kernel(xa_ref, xb_ref, nw_ref, sh_ref, sc_ref, h_ref, *, n_first):
    x = _pick(xa_ref, xb_ref, n_first)
    h = _rms(x, nw_ref[...]) * (1.0 + sc_ref[0]) + sh_ref[0]
    h_ref[...] = h.astype(h_ref.dtype)


def prenorm(xa, xb, nw, shift, scale, tg, out_dtype, tm=512):
    d = xa.shape[1]
    m = xa.shape[0] + xb.shape[0]
    per = tg // tm
    n_first = xa.shape[0] // tm
    vec = pl.BlockSpec((1, 1, d), lambda i: (i // per, 0, 0))
    return pl.pallas_call(
        functools.partial(_prenorm_kernel, n_first=n_first),
        out_shape=jax.ShapeDtypeStruct((m, d), out_dtype),
        grid=(m // tm,),
        in_specs=_two_source_specs(tm, d, n_first, 0) + [pl.BlockSpec((1, d), lambda i: (0, 0)), vec, vec],
        out_specs=pl.BlockSpec((tm, d), lambda i: (i, 0)),
        compiler_params=_params("arbitrary"),
        name="prenorm",
    )(xa, xb, nw.reshape(1, d), shift, scale)


def _mm_kernel(x_ref, w_ref, o_ref, wb_ref, *, silu):
    @pl.when(pl.program_id(1) == 0)
    def _():
        wb_ref[...] = w_ref[0].astype(BF16)

    y = jnp.dot(x_ref[...], wb_ref[...], preferred_element_type=F32)
    o_ref[...] = (_silu(y) if silu else y).astype(o_ref.dtype)


def matmul(x, w, layer, out_dtype, row0=0, rows=None, col0=0, cols=None, silu=False, tm=1024,
           tn=1024):
    k = x.shape[1]
    rows = x.shape[0] - row0 if rows is None else rows
    cols = w.shape[2] - col0 if cols is None else cols
    tn = max(t for t in range(LANES, min(tn, cols) + 1, LANES) if cols % t == 0 and col0 % t == 0)
    tm = min(tm, rows)
    r_off, c_off = row0 // tm, col0 // tn
    return pl.pallas_call(
        functools.partial(_mm_kernel, silu=silu),
        out_shape=jax.ShapeDtypeStruct((rows, cols), out_dtype),
        grid=(cols // tn, rows // tm),
        in_specs=[pl.BlockSpec((tm, k), lambda j, i: (r_off + i, 0)),
                  pl.BlockSpec((1, k, tn), lambda j, i: (layer, 0, c_off + j))],
        out_specs=pl.BlockSpec((tm, tn), lambda j, i: (i, j)),
        scratch_shapes=[pltpu.VMEM((k, tn), BF16)],
        compiler_params=_params("arbitrary", "arbitrary"),
        name="matmul",
    )(x, w)


LOG2E = 1.4426950408889634


def _softmax_pv(s_list, v_list):
    s = jnp.concatenate(s_list, axis=-1) if len(s_list) > 1 else s_list[0]
    p = jnp.exp2(s - s.max(axis=-1, keepdims=True))
    den = p.sum(axis=-1, keepdims=True)
    p = p.astype(BF16)
    acc = None
    col = 0
    for v in v_list:
        pv = jnp.dot(p[:, col:col + v.shape[0]], v, preferred_element_type=F32)
        acc = pv if acc is None else acc + pv
        col += v.shape[0]
    return acc / den


def _ctx_attn_kernel(q_ref, k_ref, v_ref, o_ref, *, n_heads):
    dh = q_ref.shape[1] // n_heads
    scale = dh ** -0.5 * LOG2E
    seq = q_ref.shape[0]
    first = lax.broadcasted_iota(jnp.int32, (seq, 2 * dh), 1) < dh
    for j in range(n_heads // 2):
        sl = slice(2 * j * dh, 2 * (j + 1) * dh)
        q2 = q_ref[:, sl].astype(BF16)
        qq = jnp.concatenate([jnp.where(first, q2, 0), jnp.where(first, 0, q2)], axis=0)
        k = k_ref[:, sl].astype(BF16)
        v = v_ref[:, sl].astype(BF16)
        s = lax.dot_general(qq, k, NT_DIMS, preferred_element_type=F32) * scale
        o2 = _softmax_pv([s], [v])
        o_ref[:, sl] = jnp.where(first, o2[:seq], o2[seq:]).astype(o_ref.dtype)


def ctx_attention(q, k, v, seq):
    m, d = q.shape
    blk = pl.BlockSpec((seq, d), lambda r: (r, 0))
    return pl.pallas_call(
        functools.partial(_ctx_attn_kernel, n_heads=N_HEADS),
        out_shape=jax.ShapeDtypeStruct((m, d), BF16),
        grid=(m // seq,),
        in_specs=[blk, blk, blk],
        out_specs=blk,
        compiler_params=_params("arbitrary"),
        name="ctx_attention",
    )(q, k, v)


def _na_row_start(r, rows, kh):
    return jnp.clip(r - kh // 2, 0, rows - kh)


def na_bias_table(rpb, rows):
    w, kw = GRID_W, WIN_W
    kh = min(WIN_H, rows)
    col = jnp.arange(w)
    cs = jnp.clip(col - kw // 2, 0, w - kw)
    dc = col[None, :] - col[:, None] + (WIN_W - 1)
    valid = (col[None, :] >= cs[:, None]) & (col[None, :] < cs[:, None] + kw)
    pick = (dc[None] == jnp.arange(2 * WIN_W - 1)[:, None, None]).astype(F32)
    c_full = jnp.einsum('hrd,dqk->hrqk', rpb, pick, precision=HIGHEST)
    c_full = jnp.where(valid[None, None], c_full * LOG2E, NEG_INF)
    n_off = WIN_H
    tab = jnp.stack([jnp.concatenate([c_full[:, off + j] for j in range(kh)], axis=-1)
                     for off in range(n_off)])
    return tab.reshape(n_off, rpb.shape[0] // 2, 2 * w, kh * w)


NA_ROWS_PER_STEP = 2


def _na_kernel(q_ref, k_ref, v_ref, kc_ref, vc_ref, *rest, n_heads, rows, kh, w):
    bias_refs, o_ref = rest[:-1], rest[-1]
    dh = q_ref.shape[2] // n_heads
    scale = dh ** -0.5 * LOG2E
    first = lax.broadcasted_iota(jnp.int32, (w, 2 * dh), 1) < dh
    for i, b_ref in enumerate(bias_refs):
        r0 = _na_row_start(pl.program_id(1) * len(bias_refs) + i, rows, kh)
        start = pl.multiple_of(r0 * w, w)
        qrows = slice(i * w, (i + 1) * w)
        for j in range(n_heads // 2):
            sl = slice(2 * j * dh, 2 * (j + 1) * dh)
            q2 = q_ref[0, qrows, sl]
            qq = jnp.concatenate([jnp.where(first, q2, 0), jnp.where(first, 0, q2)], axis=0)
            kwin = k_ref[0, pl.ds(start, kh * w), sl]
            vwin = v_ref[0, pl.ds(start, kh * w), sl]
            s_win = (lax.dot_general(qq, kwin, NT_DIMS, preferred_element_type=F32) * scale
                     + b_ref[0, j])
            s_ctx = lax.dot_general(qq, kc_ref[0, :, sl], NT_DIMS, preferred_element_type=F32) * scale
            o2 = _softmax_pv([s_win, s_ctx], [vwin, vc_ref[0, :, sl]])
            o_ref[0, qrows, sl] = jnp.where(first, o2[:w], o2[w:]).astype(o_ref.dtype)


def na_attention(qkv, k_ctx, v_ctx, bias):
    b, t, d3 = qkv.shape
    d = d3 // 3
    w = GRID_W
    rows = t // w
    kh = min(WIN_H, rows)
    n_ctx = k_ctx.shape[1]

    rps = NA_ROWS_PER_STEP
    assert rows % rps == 0 and 2 * (d // N_HEADS) == LANES

    def bias_spec(i):
        def bias_map(bi, s):
            r = s * rps + i
            return (_na_row_start(r, rows, kh) - r + WIN_H - 1, 0, 0, 0)
        return pl.BlockSpec((1, N_HEADS // 2, 2 * w, kh * w), bias_map)

    return pl.pallas_call(
        functools.partial(_na_kernel, n_heads=N_HEADS, rows=rows, kh=kh, w=w),
        out_shape=jax.ShapeDtypeStruct((b, t, d), BF16),
        grid=(b, rows // rps),
        in_specs=[pl.BlockSpec((1, rps * w, d), lambda bi, s: (bi, s, 0)),
                  pl.BlockSpec((1, t, d), lambda bi, s: (bi, 0, 1)),
                  pl.BlockSpec((1, t, d), lambda bi, s: (bi, 0, 2)),
                  pl.BlockSpec((1, n_ctx, d), lambda bi, s: (bi, 0, 0)),
                  pl.BlockSpec((1, n_ctx, d), lambda bi, s: (bi, 0, 0))]
        + [bias_spec(i) for i in range(rps)],
        out_specs=pl.BlockSpec((1, rps * w, d), lambda bi, s: (bi, s, 0)),
        compiler_params=_params("arbitrary", "arbitrary"),
        name="na_attention",
    )(qkv, qkv, qkv, k_ctx, v_ctx, *([bias] * rps))


def _residual_router_tail(mix, x, g_ref, nw_ref, sh_ref, sc_ref, rw_ref, x1_ref, h_ref, lg_ref):
    x1 = x + g_ref[0] * mix
    x1_ref[...] = x1
    h = _rms(x1, nw_ref[...]) * (1.0 + sc_ref[0]) + sh_ref[0]
    h_ref[...] = h
    n_exp = rw_ref.shape[0]
    rw = rw_ref[...]
    rw_hi = rw.astype(BF16)
    rw_lo = (rw - rw_hi.astype(F32)).astype(BF16)
    h_hi = h.astype(BF16)
    h_lo = (h - h_hi.astype(F32)).astype(BF16)
    both = lax.dot_general(jnp.concatenate([rw_hi, rw_lo], axis=0), h_hi, NT_DIMS,
                           preferred_element_type=F32)
    lg_ref[...] = (both[:n_exp] + both[n_exp:]
                   + lax.dot_general(rw_hi, h_lo, NT_DIMS, preferred_element_type=F32))


def _attn_out_kernel(aa_ref, ab_ref, w_ref, xa_ref, xb_ref, g_ref, nw_ref, sh_ref, sc_ref, rw_ref,
                     x1_ref, h_ref, lg_ref, wb_ref, *, n_first):
    @pl.when(pl.program_id(0) == 0)
    def _():
        wb_ref[...] = w_ref[0].astype(BF16)

    mix = jnp.dot(_pick(aa_ref, ab_ref, n_first), wb_ref[...], preferred_element_type=F32)
    _residual_router_tail(mix, _pick(xa_ref, xb_ref, n_first), g_ref, nw_ref, sh_ref, sc_ref,
                          rw_ref, x1_ref, h_ref, lg_ref)


def _ssd_out_kernel(yf_ref, yb_ref, zs_ref, snw_ref, w_ref, x_ref, g_ref, nw_ref,
                    sh_ref, sc_ref, rw_ref, x1_ref, h_ref, lg_ref, wb_ref):
    @pl.when(pl.program_id(0) == 0)
    def _():
        wb_ref[...] = w_ref[0].astype(BF16)

    y = yf_ref[0].astype(F32) + yb_ref[0].astype(F32)
    y = _rms(y * zs_ref[...].astype(F32), snw_ref[...])
    mix = jnp.dot(y.astype(BF16), wb_ref[...], preferred_element_type=F32)
    _residual_router_tail(mix, x_ref[...], g_ref, nw_ref, sh_ref, sc_ref, rw_ref, x1_ref, h_ref,
                          lg_ref)


def _tail_specs(d, n_exp, tm, per):
    vec = pl.BlockSpec((1, 1, d), lambda i: (i // per, 0, 0))
    row = pl.BlockSpec((tm, d), lambda i: (i, 0))
    in_specs = [vec, pl.BlockSpec((1, d), lambda i: (0, 0)), vec, vec,
                pl.BlockSpec((n_exp, d), lambda i: (0, 0))]
    out_specs = [row, row, pl.BlockSpec((n_exp, tm), lambda i: (0, i))]
    return in_specs, out_specs


def _tail_out_shape(m, d, n_exp):
    return [jax.ShapeDtypeStruct((m, d), F32), jax.ShapeDtypeStruct((m, d), F32),
            jax.ShapeDtypeStruct((n_exp, m), F32)]


def attn_out(a_first, a_second, w, layer, x_first, x_second, x_second_row0, gate, nw, shift, scale,
             router_w, tg, tm=512):
    k = a_first.shape[1]
    d = w.shape[2]
    n_first = a_first.shape[0] // tm
    m = a_first.shape[0] + a_second.shape[0]
    n_exp = router_w.shape[1]
    tail_in, tail_out = _tail_specs(d, n_exp, tm, tg // tm)
    return pl.pallas_call(
        functools.partial(_attn_out_kernel, n_first=n_first),
        out_shape=_tail_out_shape(m, d, n_exp),
        grid=(m // tm,),
        in_specs=_two_source_specs(tm, k, n_first, 0)
        + [pl.BlockSpec((1, k, d), lambda i: (layer, 0, 0))]
        + _two_source_specs(tm, d, n_first, x_second_row0 // tm) + tail_in,
        out_specs=tail_out,
        scratch_shapes=[pltpu.VMEM((k, d), BF16)],
        compiler_params=_params("arbitrary"),
        name="attn_out",
    )(a_first, a_second, w, x_first, x_second, gate, nw.reshape(1, d), shift, scale, router_w.T)


def ssd_out(y2, zs, snw, w, layer, x, gate, nw, shift, scale, router_w, tg, tm=512):
    _, k, d = w.shape
    m = x.shape[0]
    n_exp = router_w.shape[1]
    tail_in, tail_out = _tail_specs(d, n_exp, tm, tg // tm)
    return pl.pallas_call(
        _ssd_out_kernel,
        out_shape=_tail_out_shape(m, d, n_exp),
        grid=(m // tm,),
        in_specs=[pl.BlockSpec((1, tm, k), lambda i: (0, i, 0)),
                  pl.BlockSpec((1, tm, k), lambda i: (1, i, 0)),
                  pl.BlockSpec((tm, k), lambda i: (i, 0)),
                  pl.BlockSpec((1, k), lambda i: (0, 0)),
                  pl.BlockSpec((1, k, d), lambda i: (layer, 0, 0)),
                  pl.BlockSpec((tm, d), lambda i: (i, 0))] + tail_in,
        out_specs=tail_out,
        scratch_shapes=[pltpu.VMEM((k, d), BF16)],
        compiler_params=_params("arbitrary"),
        name="ssd_out",
    )(y2, y2, zs, snw.reshape(1, k), w, x, gate, nw.reshape(1, d), shift, scale, router_w.T)


def _excl_prefix_lanes(m01):
    e, t = m01.shape
    r = lax.broadcasted_iota(jnp.int32, (LANES, LANES), 0)
    c = lax.broadcasted_iota(jnp.int32, (LANES, LANES), 1)
    upper = jnp.where(r < c, 1.0, 0.0).astype(BF16)
    outs = []
    carry = jnp.zeros((e, 1), F32)
    for j in range(t // LANES):
        blk = m01[:, j * LANES:(j + 1) * LANES]
        outs.append(jnp.dot(blk.astype(BF16), upper, preferred_element_type=F32) + carry)
        carry = carry + blk.sum(axis=1, keepdims=True)
    return jnp.concatenate(outs, axis=1)


def _route_kernel(lg_ref, idx_ref, gate_ref, aff_ref, pos_ref, *, cap, set_len, tchunk):
    n_exp, t = lg_ref.shape
    n_sets = t // set_len
    lg = lg_ref[...]
    ex = jnp.exp(lg - lg.max(axis=0, keepdims=True))
    aff = ex / ex.sum(axis=0, keepdims=True)
    bits = pltpu.bitcast(aff, jnp.int32)

    def set_slices(x):
        return [x[:, s * set_len:(s + 1) * set_len] for s in range(n_sets)]

    def per_set_sum(m01):
        if n_sets == 1:
            return m01.sum(axis=1, keepdims=True)
        return jnp.concatenate([jnp.broadcast_to(blk.sum(axis=1, keepdims=True), (n_exp, set_len))
                                for blk in set_slices(m01)], axis=1)

    def per_set_prefix(m01):
        return jnp.concatenate([_excl_prefix_lanes(blk) for blk in set_slices(m01)], axis=1)

    def search(i, cur):
        cand = cur | jnp.left_shift(jnp.int32(1), 30 - i)
        cnt = per_set_sum(jnp.where(bits >= cand, 1.0, 0.0))
        return jnp.where(cnt >= cap, cand, cur)

    thr = lax.fori_loop(0, 31, search, jnp.zeros((n_exp, 1 if n_sets == 1 else t), jnp.int32))
    gt = bits > thr
    eq = bits == thr
    need = cap - per_set_sum(jnp.where(gt, 1.0, 0.0))
    eq_rank = per_set_prefix(jnp.where(eq, 1.0, 0.0))
    sel = gt | (eq & (eq_rank < need))
    pos = per_set_prefix(jnp.where(sel, 1.0, 0.0))
    aff_ref[...] = aff
    pos_ref[...] = jnp.where(sel, pos, -1.0).astype(jnp.int32)

    tok = lax.broadcasted_iota(jnp.int32, (1, tchunk), 1)
    slot = lax.broadcasted_iota(jnp.int32, (cap, tchunk), 0)
    feat_rows = 2 * SUBLANES
    zero_rows = jnp.zeros((feat_rows - 5, tchunk), F32)

    def per_expert(e, carry):
        for s in range(n_sets):
            res = jnp.zeros((feat_rows, cap), F32)
            for j in range(set_len // tchunk):
                cols = pl.ds(s * set_len + j * tchunk, tchunk)
                a = aff_ref[pl.ds(e, 1), cols]
                a_hi = a.astype(BF16).astype(F32)
                a_mid = (a - a_hi).astype(BF16).astype(F32)
                a_lo = a - a_hi - a_mid
                tj = tok + j * tchunk
                feats = jnp.concatenate(
                    [jnp.right_shift(tj, 6).astype(F32), jnp.bitwise_and(tj, 63).astype(F32),
                     a_hi, a_mid, a_lo, zero_rows], axis=0).astype(BF16)
                onehot = jnp.where(pos_ref[pl.ds(e, 1), cols] == slot, 1.0, 0.0).astype(BF16)
                res = res + lax.dot_general(feats, onehot, NT_DIMS, preferred_element_type=F32)
            idx_ref[s, pl.ds(e, 1), :] = (res[0:1] * 64.0 + res[1:2]).astype(jnp.int32)
            gate_ref[s, pl.ds(e, 1), :] = res[2:3] + res[3:4] + res[4:5]
        return carry

    lax.fori_loop(0, n_exp, per_expert, 0)


def route(logits_t, set_len, col0, n_sets, sets_per_step):
    n_exp = logits_t.shape[0]
    cap = CAPACITY_FACTOR * set_len // n_exp
    width = sets_per_step * set_len
    blk0 = col0 // width
    out_spec = pl.BlockSpec((sets_per_step, n_exp, cap), lambda s: (s, 0, 0))
    return pl.pallas_call(
        functools.partial(_route_kernel, cap=cap, set_len=set_len, tchunk=min(set_len, 1024)),
        out_shape=[jax.ShapeDtypeStruct((n_sets, n_exp, cap), jnp.int32),
                   jax.ShapeDtypeStruct((n_sets, n_exp, cap), F32)],
        grid=(n_sets // sets_per_step,),
        in_specs=[pl.BlockSpec((n_exp, width), lambda s: (0, blk0 + s))],
        out_specs=[out_spec, out_spec],
        scratch_shapes=[pltpu.VMEM((n_exp, width), F32), pltpu.VMEM((n_exp, width), jnp.int32)],
        compiler_params=_params("arbitrary"),
        name="route",
    )(logits_t)


GATHER_ROWS = 16


def _gather_kernel(idx_ref, h_ref, o_ref, *, n_groups, slots):
    g = pl.program_id(0)
    e = pl.program_id(1)
    base = (e * n_groups + g) * slots

    def body(j, carry):
        s0 = pl.multiple_of(j * GATHER_ROWS, GATHER_ROWS)
        rows = [h_ref[0, pl.ds(idx_ref[base + s0 + i], 1), :] for i in range(GATHER_ROWS)]
        o_ref[0, 0, pl.ds(s0, GATHER_ROWS), :] = jnp.concatenate(rows, axis=0).astype(o_ref.dtype)
        return carry

    lax.fori_loop(0, slots // GATHER_ROWS, body, 0)


def moe_gather(idx_flat, h, n_exp, n_groups, slots):
    _, tg, d = h.shape
    return pl.pallas_call(
        functools.partial(_gather_kernel, n_groups=n_groups, slots=slots),
        out_shape=jax.ShapeDtypeStruct((n_exp, n_groups, slots, d), BF16),
        grid_spec=pltpu.PrefetchScalarGridSpec(
            num_scalar_prefetch=1,
            grid=(n_groups, n_exp),
            in_specs=[pl.BlockSpec((1, tg, d), lambda g, e, idx: (g, 0, 0))],
            out_specs=pl.BlockSpec((1, 1, slots, d), lambda g, e, idx: (e, g, 0, 0)),
        ),
        compiler_params=_params("arbitrary", "arbitrary"),
        name="moe_gather",
    )(idx_flat, h)


def _ffn_kernel(xs_ref, wg_ref, wu_ref, wd_ref, gt_ref, o_ref, *, tm):
    f = pl.program_id(1)
    wg = wg_ref[0, 0].astype(BF16)
    wu = wu_ref[0, 0].astype(BF16)
    wd = wd_ref[0, 0].astype(BF16)

    @pl.when(f == 0)
    def _():
        o_ref[...] = jnp.zeros_like(o_ref)

    for i in range(xs_ref.shape[1] // tm):
        rows = slice(i * tm, (i + 1) * tm)
        x = xs_ref[0, rows, :]
        hid = _silu(jnp.dot(x, wg, preferred_element_type=F32)) * jnp.dot(
            x, wu, preferred_element_type=F32)
        o_ref[0, rows, :] += jnp.dot(hid.astype(BF16), wd, preferred_element_type=F32)

    @pl.when(f == pl.num_programs(1) - 1)
    def _():
        for i in range(xs_ref.shape[1] // LANES):
            g_rows = jnp.broadcast_to(gt_ref[0, :, i * LANES:(i + 1) * LANES], (LANES, LANES)).T
            for k in range(o_ref.shape[2] // LANES):
                blk = (0, slice(i * LANES, (i + 1) * LANES), slice(k * LANES, (k + 1) * LANES))
                o_ref[blk] = o_ref[blk] * g_rows


def moe_ffn(xs, w_gate, w_up, w_down, layer, gate_row, tf=512, tm=512):
    n_exp, m, d = xs.shape
    f = w_gate.shape[3]
    tf = min(tf, f)
    return pl.pallas_call(
        functools.partial(_ffn_kernel, tm=min(tm, m)),
        out_shape=jax.ShapeDtypeStruct((n_exp, m, d), F32),
        grid=(n_exp, f // tf),
        in_specs=[pl.BlockSpec((1, m, d), lambda e, j: (e, 0, 0)),
                  pl.BlockSpec((1, 1, d, tf), lambda e, j: (layer, e, 0, j)),
                  pl.BlockSpec((1, 1, d, tf), lambda e, j: (layer, e, 0, j)),
                  pl.BlockSpec((1, 1, tf, d), lambda e, j: (layer, e, j, 0)),
                  pl.BlockSpec((1, 1, m), lambda e, j: (e, 0, 0))],
        out_specs=pl.BlockSpec((1, m, d), lambda e, j: (e, 0, 0)),
        compiler_params=_params("arbitrary", "arbitrary"),
        name="moe_ffn",
    )(xs, w_gate, w_up, w_down, gate_row)


def _rows_to_tiles(tile_ref, rows):
    for k in range(tile_ref.shape[1]):
        tile_ref[:, k, :] = rows[:, k * LANES:(k + 1) * LANES]


def _tiles_to_rows(tile_ref):
    return jnp.concatenate([tile_ref[:, k, :] for k in range(tile_ref.shape[1])], axis=1)


COMBINE_UNROLL = 8


def _combine_kernel(idx_ref, ys_ref, o_ref, stage_ref, *, n_groups, slots):
    g = pl.program_id(0)
    e = pl.program_id(1)
    base = (e * n_groups + g) * slots

    @pl.when(e == 0)
    def _():
        o_ref[...] = jnp.zeros_like(o_ref)

    _rows_to_tiles(stage_ref, ys_ref[0, 0])

    def body(j, carry):
        s0 = pl.multiple_of(j * COMBINE_UNROLL, COMBINE_UNROLL)
        toks = [idx_ref[base + s0 + i] for i in range(COMBINE_UNROLL)]
        sums = [o_ref[0, t] + stage_ref[s0 + i] for i, t in enumerate(toks)]
        for t, v in zip(toks, sums):
            o_ref[0, t] = v
        return carry

    lax.fori_loop(0, slots // COMBINE_UNROLL, body, 0)


def moe_combine(idx_flat, ys, tg):
    n_exp, n_groups, slots, d = ys.shape
    c = d // LANES
    return pl.pallas_call(
        functools.partial(_combine_kernel, n_groups=n_groups, slots=slots),
        out_shape=jax.ShapeDtypeStruct((n_groups, tg, c, LANES), F32),
        grid_spec=pltpu.PrefetchScalarGridSpec(
            num_scalar_prefetch=1,
            grid=(n_groups, n_exp),
            in_specs=[pl.BlockSpec((1, 1, slots, d), lambda g, e, idx: (e, g, 0, 0))],
            out_specs=pl.BlockSpec((1, tg, c, LANES), lambda g, e, idx: (g, 0, 0, 0)),
            scratch_shapes=[pltpu.VMEM((slots, c, LANES), F32)],
        ),
        compiler_params=_params("arbitrary", "arbitrary"),
        name="moe_combine",
    )(idx_flat, ys)


def _moe_res_kernel(x_ref, y_ref, g_ref, nw_ref, sh_ref, sc_ref, x2_ref, h_ref):
    x2 = x_ref[...] + g_ref[0] * _tiles_to_rows(y_ref)
    x2_ref[...] = x2
    h_ref[...] = (_rms(x2, nw_ref[...]) * (1.0 + sc_ref[0]) + sh_ref[0]).astype(h_ref.dtype)


def moe_residual(x, y, gate, nw, shift, scale, tg, tm=512):
    m, d = x.shape
    per = tg // tm
    vec = pl.BlockSpec((1, 1, d), lambda i: (i // per, 0, 0))
    row = pl.BlockSpec((tm, d), lambda i: (i, 0))
    tiles = pl.BlockSpec((tm,) + y.shape[1:], lambda i: (i, 0, 0))
    return pl.pallas_call(
        _moe_res_kernel,
        out_shape=[jax.ShapeDtypeStruct((m, d), F32), jax.ShapeDtypeStruct((m, d), BF16)],
        grid=(m // tm,),
        in_specs=[row, tiles, vec, pl.BlockSpec((1, d), lambda i: (0, 0)), vec, vec],
        out_specs=[row, row],
        compiler_params=_params("arbitrary"),
        name="moe_residual",
    )(x, y, gate, nw.reshape(1, d), shift, scale)


def _final_kernel(x_ref, y_ref, g_ref, nw_ref, o_ref):
    o_ref[...] = _rms(x_ref[...] + g_ref[0] * _tiles_to_rows(y_ref), nw_ref[...])


def final_norm(x, y, gate, nw, tg, row0, rows, tm=512):
    d = x.shape[1]
    per = tg // tm
    off = row0 // tm
    row = pl.BlockSpec((tm, d), lambda i: (off + i, 0))
    return pl.pallas_call(
        _final_kernel,
        out_shape=jax.ShapeDtypeStruct((rows, d), F32),
        grid=(rows // tm,),
        in_specs=[row, pl.BlockSpec((tm,) + y.shape[1:], lambda i: (off + i, 0, 0)),
                  pl.BlockSpec((1, 1, d), lambda i: ((off + i) // per, 0, 0)),
                  pl.BlockSpec((1, d), lambda i: (0, 0))],
        out_specs=pl.BlockSpec((tm, d), lambda i: (i, 0)),
        compiler_params=_params("arbitrary"),
        name="final_norm",
    )(x, y, gate, nw.reshape(1, d))


CONV_HALO = 16
CONV_SUB = 256


def _shift_rows(ext3, o):
    rot = pltpu.roll(ext3, (SUBLANES - o) % SUBLANES, axis=1)
    r = lax.broadcasted_iota(jnp.int32, (ext3.shape[0] - 2,) + ext3.shape[1:], 1)
    if o > 0:
        return jnp.where(r < SUBLANES - o, rot[1:-1], rot[2:])
    return jnp.where(r >= -o, rot[1:-1], rot[:-2])


def _proj_conv_kernel(xp_ref, x_ref, xn_ref, w_ref, cw_ref, cb_ref, o_ref, wb_ref, *,
                      tm, seq_a, n_a, seq_b):
    i = pl.program_id(1)

    @pl.when(i == 0)
    def _():
        wb_ref[...] = w_ref[0].astype(BF16)

    wb = wb_ref[...]
    y = jnp.dot(x_ref[...], wb, preferred_element_type=F32)
    y_prev = jnp.dot(xp_ref[...], wb, preferred_element_type=F32)[CONV_HALO - SUBLANES:]
    y_next = jnp.dot(xn_ref[...], wb, preferred_element_type=F32)[:SUBLANES]
    row0 = i * tm
    seq = jnp.where(row0 < n_a, seq_a, seq_b)
    off = jnp.where(row0 < n_a, row0, row0 - n_a)
    half = D_CONV // 2
    sub = CONV_SUB
    n_sub = tm // sub
    for s in range(n_sub):
        cur = y[s * sub:(s + 1) * sub]
        prev = y_prev if s == 0 else y[s * sub - SUBLANES:s * sub]
        nxt = y_next if s == n_sub - 1 else y[(s + 1) * sub:(s + 1) * sub + SUBLANES]
        prev = jnp.where(lax.rem(off + s * sub, seq) == 0, 0.0, prev)
        nxt = jnp.where(lax.rem(off + (s + 1) * sub, seq) == 0, 0.0, nxt)
        ext3 = jnp.concatenate([prev, cur, nxt], axis=0).reshape(sub // SUBLANES + 2, SUBLANES, -1)
        acc = cb_ref[...] + cw_ref[half:half + 1, :] * cur
        for o in range(-half, half + 1):
            if o != 0:
                acc = acc + cw_ref[half + o:half + o + 1, :] * _shift_rows(ext3, o).reshape(sub, -1)
        o_ref[s * sub:(s + 1) * sub, :] = _silu(acc).astype(o_ref.dtype)


def proj_conv(x, w, layer, col0, cols, conv_w, conv_b, seq_a, n_a, seq_b, tm=1024, tn=1024):
    m, k = x.shape
    tn = max(t for t in range(LANES, min(tn, cols) + 1, LANES) if cols % t == 0 and col0 % t == 0)
    tm = min(tm, n_a)
    assert n_a % tm == 0 and m % tm == 0 and tm % CONV_SUB == 0
    assert seq_a % CONV_SUB == 0 and seq_b % CONV_SUB == 0
    c_off = col0 // tn
    hb = tm // CONV_HALO
    n_halo = m // CONV_HALO
    return pl.pallas_call(
        functools.partial(_proj_conv_kernel, tm=tm, seq_a=seq_a, n_a=n_a, seq_b=seq_b),
        out_shape=jax.ShapeDtypeStruct((m, cols), BF16),
        grid=(cols // tn, m // tm),
        in_specs=[pl.BlockSpec((CONV_HALO, k), lambda j, i: (jnp.maximum(i * hb - 1, 0), 0)),
                  pl.BlockSpec((tm, k), lambda j, i: (i, 0)),
                  pl.BlockSpec((CONV_HALO, k), lambda j, i: (jnp.minimum((i + 1) * hb, n_halo - 1), 0)),
                  pl.BlockSpec((1, k, tn), lambda j, i: (layer, 0, c_off + j)),
                  pl.BlockSpec((D_CONV, tn), lambda j, i: (0, j)),
                  pl.BlockSpec((1, tn), lambda j, i: (0, j))],
        out_specs=pl.BlockSpec((tm, tn), lambda j, i: (i, j)),
        scratch_shapes=[pltpu.VMEM((k, tn), BF16)],
        compiler_params=_params("arbitrary", "arbitrary"),
        name="proj_conv",
    )(x, x, x, w, conv_w, conv_b.reshape(1, cols))


def _softplus(x):
    return jnp.maximum(x, 0.0) + jnp.log(1.0 + jnp.exp(-jnp.abs(x)))


DT_REPLICAS = 3


def _split3(v, lane, n_heads):
    hi = v.astype(BF16)
    r1 = v - hi.astype(F32)
    mid = r1.astype(BF16)
    lo = (r1 - mid.astype(F32)).astype(BF16)
    zero = jnp.zeros_like(lo)
    return jnp.where(lane < n_heads, hi, jnp.where(
        lane < 2 * n_heads, mid, jnp.where(lane < DT_REPLICAS * n_heads, lo, zero)))


def _scan_kernel(rb_ref, dir_ref, first_ref, last_ref, zero_ref, s0i_ref, soi_ref,
                 xbc_ref, dt_ref, dtb_ref, alog_ref, dsk_ref, e64_ref, e128_ref, s0_ref,
                 y_ref, sout_ref, st_ref, *, n_heads):
    i = pl.program_id(0)
    q = xbc_ref.shape[0]
    p, n, g_n = SSD_HEADDIM, D_STATE, N_GROUPS
    hpg = n_heads // g_n
    d_inner = n_heads * p
    fwd = dir_ref[i] == 0

    @pl.when((first_ref[i] == 1) & (zero_ref[i] == 1))
    def _():
        st_ref[...] = jnp.zeros_like(st_ref)

    @pl.when((first_ref[i] == 1) & (zero_ref[i] == 0))
    def _():
        for g in range(g_n):
            st_ref[g] = s0_ref[0, 0, g].T

    dt = _softplus(dt_ref[...] + dtb_ref[0])
    da = dt * (-LOG2E * jnp.exp(alog_ref[0]))
    li = lax.broadcasted_iota(jnp.int32, (q, q), 0)
    si = lax.broadcasted_iota(jnp.int32, (q, q), 1)
    tri = (li - si) * jnp.where(fwd, 1, -1) >= 0
    lane = lax.broadcasted_iota(jnp.int32, (q, LANES), 1)
    acs = jnp.dot(jnp.where(tri, 1.0, 0.0), da, preferred_element_type=F32, precision=HIGHEST)
    acs_t = acs.T

    f_dt = _split3(dt, lane, n_heads)
    f_acs = _split3(acs, lane, n_heads)
    dt_all = jnp.dot(f_dt, e64_ref[...], preferred_element_type=F32)
    acs_all = jnp.dot(f_acs, e64_ref[...], preferred_element_type=F32)
    acs_end = jnp.where(fwd, acs_all[q - 1:q, :], acs_all[0:1, :])
    e_in = jnp.exp2(acs_all)
    dec = jnp.exp2(acs_end)
    x = xbc_ref[:, :d_inner].astype(F32)
    xdt = x * dt_all
    skip = x * jnp.where(fwd, dsk_ref[...], 0.0)
    xdt_b = xdt.astype(BF16)
    xd_b = (xdt * jnp.exp2(acs_end - acs_all)).astype(BF16)
    first_head = lax.broadcasted_iota(jnp.int32, (q, 2 * p), 1) < p

    for g in range(g_n):
        gcols = slice(g * hpg * p, (g + 1) * hpg * p)
        bm = xbc_ref[:, d_inner + g * n:d_inner + (g + 1) * n]
        cm = xbc_ref[:, d_inner + g_n * n + g * n:d_inner + g_n * n + (g + 1) * n]
        cb = lax.dot_general(cm, bm, NT_DIMS, preferred_element_type=F32)
        st = st_ref[g]
        y_off = jnp.dot(cm, st.astype(BF16), preferred_element_type=F32) * e_in[:, gcols]
        a_g = jnp.dot(f_acs, e128_ref[:, g * hpg * q:(g + 1) * hpg * q],
                      preferred_element_type=F32)
        for e2 in range(hpg // 2):
            h0 = g * hpg + 2 * e2
            pc = slice(h0 * p, (h0 + 2) * p)
            ms = []
            for k in range(2):
                e = 2 * e2 + k
                seg = a_g[:, e * q:(e + 1) * q] - acs_t[h0 + k:h0 + k + 1, :]
                ms.append((jnp.where(tri, jnp.exp2(seg), 0.0) * cb).astype(BF16))
            yd = jnp.dot(jnp.concatenate(ms, axis=0), xdt_b[:, pc], preferred_element_type=F32)
            y = (jnp.where(first_head, yd[:q], yd[q:]) + y_off[:, 2 * e2 * p:(2 * e2 + 2) * p]
                 + skip[:, pc])
            y_ref[0, :, pc] = y.astype(y_ref.dtype)
        bm_t = bm.astype(F32).T.astype(BF16)
        s_new = jnp.dot(bm_t, xd_b[:, gcols], preferred_element_type=F32)
        st_ref[g] = st * dec[:, gcols] + s_new

    @pl.when(last_ref[i] == 1)
    def _():
        for g in range(g_n):
            sout_ref[0, 0, g] = st_ref[g].T


def _expansion(n_heads, width):
    row_head = jnp.arange(LANES) % n_heads
    row_ok = jnp.arange(LANES) < DT_REPLICAS * n_heads
    col_head = jnp.arange(n_heads * width) // width
    return ((row_head[:, None] == col_head[None, :]) & row_ok[:, None]).astype(BF16)


def ssd_scan(xconv, dt2, dtb2, alog2, d_cols, s0_t, n_heads, seqs):
    m, c = xconv.shape
    q = SSD_CHUNK
    hp = n_heads // N_GROUPS * SSD_HEADDIM
    d_inner = n_heads * SSD_HEADDIM
    assert DT_REPLICAS * n_heads <= LANES and 2 * SSD_HEADDIM == LANES == q
    assert (n_heads // N_GROUPS) % 2 == 0
    tabs = [[] for _ in range(7)]
    for row0, length, s0_slot, out_slot in seqs:
        nc = length // q
        for d in range(2):
            for ci in range(nc):
                vals = (row0 // q + (ci if d == 0 else nc - 1 - ci), d, int(ci == 0),
                        int(ci == nc - 1), int(s0_slot is None), s0_slot or 0, out_slot)
                for t, v in zip(tabs, vals):
                    t.append(v)
    n_steps = len(tabs[0])
    n_out = max(t[3] for t in seqs) + 1
    tabs = [jnp.asarray(t, jnp.int32) for t in tabs]

    st_block = (1, 1, N_GROUPS, hp, D_STATE)
    const = lambda i, *_: (0, 0)
    return pl.pallas_call(
        functools.partial(_scan_kernel, n_heads=n_heads),
        out_shape=[jax.ShapeDtypeStruct((2, m, d_inner), BF16),
                   jax.ShapeDtypeStruct((2, n_out, N_GROUPS, hp, D_STATE), F32)],
        grid_spec=pltpu.PrefetchScalarGridSpec(
            num_scalar_prefetch=7,
            grid=(n_steps,),
            in_specs=[pl.BlockSpec((q, c), lambda i, rb, dr, *_: (rb[i], 0)),
                      pl.BlockSpec((q, LANES), lambda i, rb, dr, *_: (rb[i], dr[i])),
                      pl.BlockSpec((1, 1, LANES), lambda i, rb, dr, *_: (dr[i], 0, 0)),
                      pl.BlockSpec((1, 1, LANES), lambda i, rb, dr, *_: (dr[i], 0, 0)),
                      pl.BlockSpec((1, d_inner), const),
                      pl.BlockSpec((LANES, d_inner), const),
                      pl.BlockSpec((LANES, n_heads * q), const),
                      pl.BlockSpec(st_block, lambda i, rb, dr, f, l, z, s0i, soi: (dr[i], s0i[i], 0, 0, 0))],
            out_specs=[pl.BlockSpec((1, q, d_inner), lambda i, rb, dr, *_: (dr[i], rb[i], 0)),
                       pl.BlockSpec(st_block, lambda i, rb, dr, f, l, z, s0i, soi: (dr[i], soi[i], 0, 0, 0))],
            scratch_shapes=[pltpu.VMEM((N_GROUPS, D_STATE, hp), F32)],
        ),
        compiler_params=_params("arbitrary"),
        name="ssd_scan",
    )(*tabs, xconv, dt2, dtb2, alog2, d_cols.reshape(1, d_inner), _expansion(n_heads, SSD_HEADDIM),
      _expansion(n_heads, q), s0_t)


def _replicate_heads(v, n_heads):
    lead = v.shape[:-1]
    v = v.astype(F32).reshape(lead + (2, 1, n_heads))
    v = jnp.broadcast_to(v, lead + (2, DT_REPLICAS, n_heads)).reshape(lead + (2, DT_REPLICAS * n_heads))
    pad = [(0, 0)] * (len(lead) + 1) + [(0, LANES - DT_REPLICAS * n_heads)]
    return jnp.pad(v, pad).reshape(lead + (2 * LANES,))


def _state_groups(s, n_heads):
    return s.reshape(s.shape[0], N_GROUPS, n_heads // N_GROUPS * SSD_HEADDIM, D_STATE)


def _moe(h, logits_t, w_gate, w_up, w_down, layer, dims):
    n_req, seq, n_lat, tg, d = dims
    n_groups = 1 + n_lat
    n_exp = logits_t.shape[0]
    idx_p, gate_p = route(logits_t, seq, 0, n_req, n_req)
    idx_s, gate_s = route(logits_t, tg, tg, n_lat, 1)
    slots = idx_s.shape[2]
    idx_p = idx_p + (jnp.arange(n_req, dtype=jnp.int32) * seq)[:, None, None]
    idx = jnp.concatenate([idx_p.transpose(1, 0, 2).reshape(n_exp, 1, slots),
                           idx_s.transpose(1, 0, 2)], axis=1)
    gate = jnp.concatenate([gate_p.transpose(1, 0, 2).reshape(n_exp, 1, slots),
                            gate_s.transpose(1, 0, 2)], axis=1)
    idx_flat = idx.reshape(-1)
    xs = moe_gather(idx_flat, h.reshape(n_groups, tg, d), n_exp, n_groups, slots)
    ys = moe_ffn(xs.reshape(n_exp, n_groups * slots, d), w_gate, w_up, w_down, layer,
                 gate.reshape(n_exp, 1, n_groups * slots))
    out = moe_combine(idx_flat, ys.reshape(n_exp, n_groups, slots, d), tg)
    return out.reshape((n_groups * tg,) + out.shape[2:])


def kernel(x_prompt, x_sample, cache_k_na, cache_v_na, state_ssd_fwd, state_ssd_bwd, c, c_ctx, ada_w, ada_b, norm1_w, norm2_w, final_norm_w, na_qkv_w, na_out_w, na_rpb, ssd_in_w, ssd_conv_w, ssd_conv_b, ssd_dt_bias, ssd_a_log, ssd_d, ssd_norm_w, ssd_out_w, router_w, moe_w_gate, moe_w_up, moe_w_down):
    n_req, seq, d = x_prompt.shape
    n_lat, tg, _ = x_sample.shape
    assert n_req * seq == tg, "token groups must have equal size"
    assert n_lat + 1 <= MOD_ROWS
    depth = ada_w.shape[0]
    n_groups = 1 + n_lat
    m_p = n_req * seq
    dims = (n_req, seq, n_lat, tg, d)

    cvec = jnp.zeros((MOD_ROWS, d), F32).at[0].set(c_ctx).at[1:1 + n_lat].set(c)
    mod = ada_modulation(cvec, ada_w, ada_b)
    mod = mod.reshape(depth, MOD_ROWS, N_MOD, 1, d).transpose(0, 2, 1, 3, 4)

    m_s = n_lat * tg
    x_first, x_second, x_second_row0 = x_prompt.reshape(m_p, d), x_sample.reshape(m_s, d), 0
    h = prenorm(x_first, x_second, norm1_w[0], mod[0, 0], mod[0, 1], tg, BF16)

    new_k, new_v, new_sf, new_sb = [], [], [], []
    y_prompt = y_sample = None
    for layer in range(depth):
        j = layer // 2
        ml = mod[layer]
        if layer % 2 == 0:
            q_p, k_p, v_p = (matmul(h, na_qkv_w, j, F32, rows=m_p, col0=i * d, cols=d)
                             for i in range(3))
            qkv_s = matmul(h, na_qkv_w, j, BF16, row0=m_p)
            new_k.append(k_p.reshape(n_req, seq, N_HEADS, d // N_HEADS))
            new_v.append(v_p.reshape(n_req, seq, N_HEADS, d // N_HEADS))
            att_p = ctx_attention(q_p, k_p, v_p, seq)
            past = cache_k_na.shape[2]
            att_s = na_attention(qkv_s.reshape(n_lat, tg, 3 * d),
                                 cache_k_na[:, j].reshape(n_lat, past, d).astype(BF16),
                                 cache_v_na[:, j].reshape(n_lat, past, d).astype(BF16),
                                 na_bias_table(na_rpb[j], tg // GRID_W))
            x1, hm, logits_t = attn_out(att_p, att_s.reshape(m_s, d), na_out_w, j, x_first, x_second,
                                        x_second_row0, ml[2], norm2_w[layer], ml[3], ml[4],
                                        router_w[layer], tg)
        else:
            n_heads = ssd_d.shape[1]
            d_inner = n_heads * SSD_HEADDIM
            conv_dim = ssd_conv_w.shape[2]
            zs = matmul(h, ssd_in_w, j, BF16, cols=d_inner, silu=True)
            xconv = proj_conv(h, ssd_in_w, j, d_inner, conv_dim, ssd_conv_w[j], ssd_conv_b[j],
                              seq, m_p, tg)
            w_dt = _replicate_heads(ssd_in_w[j, :, d_inner + conv_dim:], n_heads)
            dt2 = matmul(h, w_dt[None], 0, F32)
            dtb2 = _replicate_heads(ssd_dt_bias[j].reshape(-1), n_heads).reshape(2, 1, LANES)
            alog2 = _replicate_heads(ssd_a_log[j].reshape(-1), n_heads).reshape(2, 1, LANES)
            s0 = jnp.stack([_state_groups(state_ssd_fwd[:, j], n_heads),
                            _state_groups(state_ssd_bwd[:, j], n_heads)])
            seqs = [(r * seq, seq, None, r) for r in range(n_req)]
            seqs += [(m_p + b * tg, tg, b, n_req + b) for b in range(n_lat)]
            d_cols = jnp.repeat(ssd_d[j], SSD_HEADDIM)
            y2, st = ssd_scan(xconv, dt2, dtb2, alog2, d_cols, s0, n_heads, seqs)
            new_sf.append(st[0, :n_req].reshape(n_req, n_heads, SSD_HEADDIM, D_STATE))
            new_sb.append(st[1, :n_req].reshape(n_req, n_heads, SSD_HEADDIM, D_STATE))
            x1, hm, logits_t = ssd_out(y2, zs, ssd_norm_w[j], ssd_out_w, j, x_first,
                                       ml[2], norm2_w[layer], ml[3], ml[4], router_w[layer], tg)
        y_moe = _moe(hm, logits_t, moe_w_gate, moe_w_up, moe_w_down, layer, dims)
        if layer + 1 < depth:
            mn = mod[layer + 1]
            x, h = moe_residual(x1, y_moe, ml[5], norm1_w[layer + 1], mn[0], mn[1], tg)
            x_first, x_second, x_second_row0 = x, x, m_p
        else:
            y_prompt = final_norm(x1, y_moe, ml[5], final_norm_w, tg, 0, m_p)
            y_sample = final_norm(x1, y_moe, ml[5], final_norm_w, tg, m_p, m_s)

    return (y_prompt.reshape(n_req, seq, d), y_sample.reshape(n_lat, tg, d),
            jnp.stack(new_k, axis=1), jnp.stack(new_v, axis=1),
            jnp.stack(new_sf, axis=1), jnp.stack(new_sb, axis=1))
```

```python
import functools

import jax
import jax.numpy as jnp
from jax import lax
from jax.experimental import pallas as pl
from jax.experimental.pallas import tpu as pltpu

N_HEADS = 16
GRID_W = 64
WIN_H = 8
WIN_W = 16
SSD_HEADDIM = 64
N_GROUPS = 4
D_STATE = 128
D_CONV = 5
SSD_CHUNK = 128
CAPACITY_FACTOR = 2
N_MOD = 6
RMS_EPS = 1e-6

LANES = 128
SUBLANES = 8
MOD_ROWS = 8
VMEM_LIMIT = 56 * 1024 * 1024
NEG_INF = -1e30

F32 = jnp.float32
BF16 = jnp.bfloat16
HIGHEST = lax.Precision.HIGHEST
NT_DIMS = (((1,), (1,)), ((), ()))


def _params(*sem):
    return pltpu.CompilerParams(dimension_semantics=sem, vmem_limit_bytes=VMEM_LIMIT)


def _silu(x):
    return x / (1.0 + jnp.exp(-x))


def _rms(x, w):
    ms = jnp.mean(x * x, axis=-1, keepdims=True)
    return x * lax.rsqrt(ms + RMS_EPS) * w


def _ada_kernel(c_ref, w_ref, b_ref, o_ref):
    o_ref[0] = jnp.dot(_silu(c_ref[...]), w_ref[0], preferred_element_type=F32,
                       precision=HIGHEST) + b_ref[0]


def ada_modulation(cvec, ada_w, ada_b):
    n_layers, d, n = ada_w.shape
    tn = n // 4
    return pl.pallas_call(
        _ada_kernel,
        out_shape=jax.ShapeDtypeStruct((n_layers, MOD_ROWS, n), F32),
        grid=(n_layers, n // tn),
        in_specs=[pl.BlockSpec((MOD_ROWS, d), lambda l, j: (0, 0)),
                  pl.BlockSpec((1, d, tn), lambda l, j: (l, 0, j)),
                  pl.BlockSpec((1, 1, tn), lambda l, j: (l, 0, j))],
        out_specs=pl.BlockSpec((1, MOD_ROWS, tn), lambda l, j: (l, 0, j)),
        compiler_params=_params("arbitrary", "arbitrary"),
        name="ada_modulation",
    )(cvec, ada_w, ada_b.reshape(n_layers, 1, n))


def _two_source_specs(tm, width, n_first, off_second):
    first = pl.BlockSpec((tm, width), lambda i: (jnp.minimum(i, n_first - 1), 0))
    second = pl.BlockSpec((tm, width), lambda i: (jnp.maximum(i - n_first, 0) + off_second, 0))
    return [first, second]


def _pick(first_ref, second_ref, n_first):
    return jnp.where(pl.program_id(0) < n_first, first_ref[...], second_ref[...])


def _prenorm_kernel(xa_ref, xb_ref, nw_ref, sh_ref, sc_ref, h_ref, *, n_first):
    x = _pick(xa_ref, xb_ref, n_first)
    h = _rms(x, nw_ref[...]) * (1.0 + sc_ref[0]) + sh_ref[0]
    h_ref[...] = h.astype(h_ref.dtype)


def prenorm(xa, xb, nw, shift, scale, tg, out_dtype, tm=1024):
    d = xa.shape[1]
    m = xa.shape[0] + xb.shape[0]
    per = tg // tm
    n_first = xa.shape[0] // tm
    vec = pl.BlockSpec((1, 1, d), lambda i: (i // per, 0, 0))
    return pl.pallas_call(
        functools.partial(_prenorm_kernel, n_first=n_first),
        out_shape=jax.ShapeDtypeStruct((m, d), out_dtype),
        grid=(m // tm,),
        in_specs=_two_source_specs(tm, d, n_first, 0) + [pl.BlockSpec((1, d), lambda i: (0, 0)), vec, vec],
        out_specs=pl.BlockSpec((tm, d), lambda i: (i, 0)),
        compiler_params=_params("arbitrary"),
        name="prenorm",
    )(xa, xb, nw.reshape(1, d), shift, scale)


def _mm_kernel(x_ref, w_ref, o_ref, wb_ref, *, silu):
    @pl.when(pl.program_id(1) == 0)
    def _():
        wb_ref[...] = w_ref[0].astype(BF16)

    y = jnp.dot(x_ref[...], wb_ref[...], preferred_element_type=F32)
    o_ref[...] = (_silu(y) if silu else y).astype(o_ref.dtype)


def matmul(x, w, layer, out_dtype, row0=0, rows=None, col0=0, cols=None, silu=False, tm=1024,
           tn=1024):
    k = x.shape[1]
    rows = x.shape[0] - row0 if rows is None else rows
    cols = w.shape[2] - col0 if cols is None else cols
    tn = max(t for t in range(LANES, min(tn, cols) + 1, LANES) if cols % t == 0 and col0 % t == 0)
    tm = min(tm, rows)
    r_off, c_off = row0 // tm, col0 // tn
    return pl.pallas_call(
        functools.partial(_mm_kernel, silu=silu),
        out_shape=jax.ShapeDtypeStruct((rows, cols), out_dtype),
        grid=(cols // tn, rows // tm),
        in_specs=[pl.BlockSpec((tm, k), lambda j, i: (r_off + i, 0)),
                  pl.BlockSpec((1, k, tn), lambda j, i: (layer, 0, c_off + j))],
        out_specs=pl.BlockSpec((tm, tn), lambda j, i: (i, j)),
        scratch_shapes=[pltpu.VMEM((k, tn), BF16)],
        compiler_params=_params("arbitrary", "arbitrary"),
        name="matmul",
    )(x, w)


LOG2E = 1.4426950408889634


def _softmax_pv(s_list, v_list):
    s = jnp.concatenate(s_list, axis=-1) if len(s_list) > 1 else s_list[0]
    p = jnp.exp2(s - s.max(axis=-1, keepdims=True))
    den = p.sum(axis=-1, keepdims=True)
    p = p.astype(BF16)
    acc = None
    col = 0
    for v in v_list:
        pv = jnp.dot(p[:, col:col + v.shape[0]], v, preferred_element_type=F32)
        acc = pv if acc is None else acc + pv
        col += v.shape[0]
    return acc / den


def _ctx_attn_kernel(q_ref, k_ref, v_ref, o_ref, *, n_heads):
    dh = q_ref.shape[1] // n_heads
    scale = dh ** -0.5 * LOG2E
    seq = q_ref.shape[0]
    first = lax.broadcasted_iota(jnp.int32, (seq, 2 * dh), 1) < dh
    for j in range(n_heads // 2):
        sl = slice(2 * j * dh, 2 * (j + 1) * dh)
        q2 = q_ref[:, sl].astype(BF16)
        qq = jnp.concatenate([jnp.where(first, q2, 0), jnp.where(first, 0, q2)], axis=0)
        k = k_ref[:, sl].astype(BF16)
        v = v_ref[:, sl].astype(BF16)
        s = lax.dot_general(qq, k, NT_DIMS, preferred_element_type=F32) * scale
        o2 = _softmax_pv([s], [v])
        o_ref[:, sl] = jnp.where(first, o2[:seq], o2[seq:]).astype(o_ref.dtype)


def ctx_attention(q, k, v, seq):
    m, d = q.shape
    blk = pl.BlockSpec((seq, d), lambda r: (r, 0))
    return pl.pallas_call(
        functools.partial(_ctx_attn_kernel, n_heads=N_HEADS),
        out_shape=jax.ShapeDtypeStruct((m, d), BF16),
        grid=(m // seq,),
        in_specs=[blk, blk, blk],
        out_specs=blk,
        compiler_params=_params("arbitrary"),
        name="ctx_attention",
    )(q, k, v)


def _na_row_start(r, rows, kh):
    return jnp.clip(r - kh // 2, 0, rows - kh)


def na_bias_table(rpb, rows):
    w, kw = GRID_W, WIN_W
    kh = min(WIN_H, rows)
    col = jnp.arange(w)
    cs = jnp.clip(col - kw // 2, 0, w - kw)
    dc = col[None, :] - col[:, None] + (WIN_W - 1)
    valid = (col[None, :] >= cs[:, None]) & (col[None, :] < cs[:, None] + kw)
    pick = (dc[None] == jnp.arange(2 * WIN_W - 1)[:, None, None]).astype(F32)
    c_full = jnp.einsum('hrd,dqk->hrqk', rpb, pick, precision=HIGHEST)
    c_full = jnp.where(valid[None, None], c_full * LOG2E, NEG_INF)
    n_off = WIN_H
    tab = jnp.stack([jnp.concatenate([c_full[:, off + j] for j in range(kh)], axis=-1)
                     for off in range(n_off)])
    return tab.reshape(n_off, rpb.shape[0] // 2, 2 * w, kh * w)


NA_ROWS_PER_STEP = 2


def _na_kernel(q_ref, k_ref, v_ref, kc_ref, vc_ref, *rest, n_heads, rows, kh, w):
    bias_refs, o_ref = rest[:-1], rest[-1]
    dh = q_ref.shape[2] // n_heads
    scale = dh ** -0.5 * LOG2E
    first = lax.broadcasted_iota(jnp.int32, (w, 2 * dh), 1) < dh
    for i, b_ref in enumerate(bias_refs):
        r0 = _na_row_start(pl.program_id(1) * len(bias_refs) + i, rows, kh)
        start = pl.multiple_of(r0 * w, w)
        qrows = slice(i * w, (i + 1) * w)
        for j in range(n_heads // 2):
            sl = slice(2 * j * dh, 2 * (j + 1) * dh)
            q2 = q_ref[0, qrows, sl]
            qq = jnp.concatenate([jnp.where(first, q2, 0), jnp.where(first, 0, q2)], axis=0)
            kwin = k_ref[0, pl.ds(start, kh * w), sl]
            vwin = v_ref[0, pl.ds(start, kh * w), sl]
            s_win = (lax.dot_general(qq, kwin, NT_DIMS, preferred_element_type=F32) * scale
                     + b_ref[0, j])
            s_ctx = lax.dot_general(qq, kc_ref[0, :, sl], NT_DIMS, preferred_element_type=F32) * scale
            o2 = _softmax_pv([s_win, s_ctx], [vwin, vc_ref[0, :, sl]])
            o_ref[0, qrows, sl] = jnp.where(first, o2[:w], o2[w:]).astype(o_ref.dtype)


def na_attention(qkv, k_ctx, v_ctx, bias):
    b, t, d3 = qkv.shape
    d = d3 // 3
    w = GRID_W
    rows = t // w
    kh = min(WIN_H, rows)
    n_ctx = k_ctx.shape[1]

    rps = NA_ROWS_PER_STEP
    assert rows % rps == 0 and 2 * (d // N_HEADS) == LANES

    def bias_spec(i):
        def bias_map(bi, s):
            r = s * rps + i
            return (_na_row_start(r, rows, kh) - r + WIN_H - 1, 0, 0, 0)
        return pl.BlockSpec((1, N_HEADS // 2, 2 * w, kh * w), bias_map)

    return pl.pallas_call(
        functools.partial(_na_kernel, n_heads=N_HEADS, rows=rows, kh=kh, w=w),
        out_shape=jax.ShapeDtypeStruct((b, t, d), BF16),
        grid=(b, rows // rps),
        in_specs=[pl.BlockSpec((1, rps * w, d), lambda bi, s: (bi, s, 0)),
                  pl.BlockSpec((1, t, d), lambda bi, s: (bi, 0, 1)),
                  pl.BlockSpec((1, t, d), lambda bi, s: (bi, 0, 2)),
                  pl.BlockSpec((1, n_ctx, d), lambda bi, s: (bi, 0, 0)),
                  pl.BlockSpec((1, n_ctx, d), lambda bi, s: (bi, 0, 0))]
        + [bias_spec(i) for i in range(rps)],
        out_specs=pl.BlockSpec((1, rps * w, d), lambda bi, s: (bi, s, 0)),
        compiler_params=_params("arbitrary", "arbitrary"),
        name="na_attention",
    )(qkv, qkv, qkv, k_ctx, v_ctx, *([bias] * rps))


def _residual_router_tail(mix, x, g_ref, nw_ref, sh_ref, sc_ref, rw_ref, x1_ref, h_ref, lg_ref):
    x1 = x + g_ref[0] * mix
    x1_ref[...] = x1
    h = _rms(x1, nw_ref[...]) * (1.0 + sc_ref[0]) + sh_ref[0]
    h_ref[...] = h
    n_exp = rw_ref.shape[0]
    rw = rw_ref[...]
    rw_hi = rw.astype(BF16)
    rw_lo = (rw - rw_hi.astype(F32)).astype(BF16)
    h_hi = h.astype(BF16)
    h_lo = (h - h_hi.astype(F32)).astype(BF16)
    both = lax.dot_general(jnp.concatenate([rw_hi, rw_lo], axis=0), h_hi, NT_DIMS,
                           preferred_element_type=F32)
    lg_ref[...] = (both[:n_exp] + both[n_exp:]
                   + lax.dot_general(rw_hi, h_lo, NT_DIMS, preferred_element_type=F32))


def _attn_out_kernel(aa_ref, ab_ref, w_ref, xa_ref, xb_ref, g_ref, nw_ref, sh_ref, sc_ref, rw_ref,
                     x1_ref, h_ref, lg_ref, wb_ref, *, n_first):
    @pl.when(pl.program_id(0) == 0)
    def _():
        wb_ref[...] = w_ref[0].astype(BF16)

    mix = jnp.dot(_pick(aa_ref, ab_ref, n_first), wb_ref[...], preferred_element_type=F32)
    _residual_router_tail(mix, _pick(xa_ref, xb_ref, n_first), g_ref, nw_ref, sh_ref, sc_ref,
                          rw_ref, x1_ref, h_ref, lg_ref)


def _ssd_out_kernel(yf_ref, yb_ref, zs_ref, snw_ref, w_ref, x_ref, g_ref, nw_ref,
                    sh_ref, sc_ref, rw_ref, x1_ref, h_ref, lg_ref, wb_ref):
    @pl.when(pl.program_id(0) == 0)
    def _():
        wb_ref[...] = w_ref[0].astype(BF16)

    y = yf_ref[0].astype(F32) + yb_ref[0].astype(F32)
    y = _rms(y * zs_ref[...].astype(F32), snw_ref[...])
    mix = jnp.dot(y.astype(BF16), wb_ref[...], preferred_element_type=F32)
    _residual_router_tail(mix, x_ref[...], g_ref, nw_ref, sh_ref, sc_ref, rw_ref, x1_ref, h_ref,
                          lg_ref)


def _tail_specs(d, n_exp, tm, per):
    vec = pl.BlockSpec((1, 1, d), lambda i: (i // per, 0, 0))
    row = pl.BlockSpec((tm, d), lambda i: (i, 0))
    in_specs = [vec, pl.BlockSpec((1, d), lambda i: (0, 0)), vec, vec,
                pl.BlockSpec((n_exp, d), lambda i: (0, 0))]
    out_specs = [row, row, pl.BlockSpec((n_exp, tm), lambda i: (0, i))]
    return in_specs, out_specs


def _tail_out_shape(m, d, n_exp):
    return [jax.ShapeDtypeStruct((m, d), F32), jax.ShapeDtypeStruct((m, d), F32),
            jax.ShapeDtypeStruct((n_exp, m), F32)]


def attn_out(a_first, a_second, w, layer, x_first, x_second, x_second_row0, gate, nw, shift, scale,
             router_w, tg, tm=512):
    k = a_first.shape[1]
    d = w.shape[2]
    n_first = a_first.shape[0] // tm
    m = a_first.shape[0] + a_second.shape[0]
    n_exp = router_w.shape[1]
    tail_in, tail_out = _tail_specs(d, n_exp, tm, tg // tm)
    return pl.pallas_call(
        functools.partial(_attn_out_kernel, n_first=n_first),
        out_shape=_tail_out_shape(m, d, n_exp),
        grid=(m // tm,),
        in_specs=_two_source_specs(tm, k, n_first, 0)
        + [pl.BlockSpec((1, k, d), lambda i: (layer, 0, 0))]
        + _two_source_specs(tm, d, n_first, x_second_row0 // tm) + tail_in,
        out_specs=tail_out,
        scratch_shapes=[pltpu.VMEM((k, d), BF16)],
        compiler_params=_params("arbitrary"),
        name="attn_out",
    )(a_first, a_second, w, x_first, x_second, gate, nw.reshape(1, d), shift, scale, router_w.T)


def ssd_out(y2, zs, snw, w, layer, x, gate, nw, shift, scale, router_w, tg, tm=512):
    _, k, d = w.shape
    m = x.shape[0]
    n_exp = router_w.shape[1]
    tail_in, tail_out = _tail_specs(d, n_exp, tm, tg // tm)
    return pl.pallas_call(
        _ssd_out_kernel,
        out_shape=_tail_out_shape(m, d, n_exp),
        grid=(m // tm,),
        in_specs=[pl.BlockSpec((1, tm, k), lambda i: (0, i, 0)),
                  pl.BlockSpec((1, tm, k), lambda i: (1, i, 0)),
                  pl.BlockSpec((tm, k), lambda i: (i, 0)),
                  pl.BlockSpec((1, k), lambda i: (0, 0)),
                  pl.BlockSpec((1, k, d), lambda i: (layer, 0, 0)),
                  pl.BlockSpec((tm, d), lambda i: (i, 0))] + tail_in,
        out_specs=tail_out,
        scratch_shapes=[pltpu.VMEM((k, d), BF16)],
        compiler_params=_params("arbitrary"),
        name="ssd_out",
    )(y2, y2, zs, snw.reshape(1, k), w, x, gate, nw.reshape(1, d), shift, scale, router_w.T)


def _excl_prefix_lanes(m01):
    e, t = m01.shape
    r = lax.broadcasted_iota(jnp.int32, (LANES, LANES), 0)
    c = lax.broadcasted_iota(jnp.int32, (LANES, LANES), 1)
    upper = jnp.where(r < c, 1.0, 0.0).astype(BF16)
    outs = []
    carry = jnp.zeros((e, 1), F32)
    for j in range(t // LANES):
        blk = m01[:, j * LANES:(j + 1) * LANES]
        outs.append(jnp.dot(blk.astype(BF16), upper, preferred_element_type=F32) + carry)
        carry = carry + blk.sum(axis=1, keepdims=True)
    return jnp.concatenate(outs, axis=1)


def _route_kernel(lg_ref, idx_ref, gate_ref, aff_ref, pos_ref, *, cap, set_len, tchunk):
    n_exp, t = lg_ref.shape
    n_sets = t // set_len
    lg = lg_ref[...]
    ex = jnp.exp(lg - lg.max(axis=0, keepdims=True))
    aff = ex / ex.sum(axis=0, keepdims=True)
    bits = pltpu.bitcast(aff, jnp.int32)

    def set_slices(x):
        return [x[:, s * set_len:(s + 1) * set_len] for s in range(n_sets)]

    def per_set_sum(m01):
        if n_sets == 1:
            return m01.sum(axis=1, keepdims=True)
        return jnp.concatenate([jnp.broadcast_to(blk.sum(axis=1, keepdims=True), (n_exp, set_len))
                                for blk in set_slices(m01)], axis=1)

    def per_set_prefix(m01):
        return jnp.concatenate([_excl_prefix_lanes(blk) for blk in set_slices(m01)], axis=1)

    def search(i, cur):
        cand = cur | jnp.left_shift(jnp.int32(1), 30 - i)
        cnt = per_set_sum(jnp.where(bits >= cand, 1.0, 0.0))
        return jnp.where(cnt >= cap, cand, cur)

    thr = lax.fori_loop(0, 31, search, jnp.zeros((n_exp, 1 if n_sets == 1 else t), jnp.int32))
    gt = bits > thr
    eq = bits == thr
    need = cap - per_set_sum(jnp.where(gt, 1.0, 0.0))
    eq_rank = per_set_prefix(jnp.where(eq, 1.0, 0.0))
    sel = gt | (eq & (eq_rank < need))
    pos = per_set_prefix(jnp.where(sel, 1.0, 0.0))
    aff_ref[...] = aff
    pos_ref[...] = jnp.where(sel, pos, -1.0).astype(jnp.int32)

    tok = lax.broadcasted_iota(jnp.int32, (1, tchunk), 1)
    slot = lax.broadcasted_iota(jnp.int32, (cap, tchunk), 0)
    feat_rows = 2 * SUBLANES
    zero_rows = jnp.zeros((feat_rows - 5, tchunk), F32)

    def per_expert(e, carry):
        for s in range(n_sets):
            res = jnp.zeros((feat_rows, cap), F32)
            for j in range(set_len // tchunk):
                cols = pl.ds(s * set_len + j * tchunk, tchunk)
                a = aff_ref[pl.ds(e, 1), cols]
                a_hi = a.astype(BF16).astype(F32)
                a_mid = (a - a_hi).astype(BF16).astype(F32)
                a_lo = a - a_hi - a_mid
                tj = tok + j * tchunk
                feats = jnp.concatenate(
                    [jnp.right_shift(tj, 6).astype(F32), jnp.bitwise_and(tj, 63).astype(F32),
                     a_hi, a_mid, a_lo, zero_rows], axis=0).astype(BF16)
                onehot = jnp.where(pos_ref[pl.ds(e, 1), cols] == slot, 1.0, 0.0).astype(BF16)
                res = res + lax.dot_general(feats, onehot, NT_DIMS, preferred_element_type=F32)
            idx_ref[s, pl.ds(e, 1), :] = (res[0:1] * 64.0 + res[1:2]).astype(jnp.int32)
            gate_ref[s, pl.ds(e, 1), :] = res[2:3] + res[3:4] + res[4:5]
        return carry

    lax.fori_loop(0, n_exp, per_expert, 0)


def route(logits_t, set_len, col0, n_sets, sets_per_step):
    n_exp = logits_t.shape[0]
    cap = CAPACITY_FACTOR * set_len // n_exp
    width = sets_per_step * set_len
    blk0 = col0 // width
    out_spec = pl.BlockSpec((sets_per_step, n_exp, cap), lambda s: (s, 0, 0))
    return pl.pallas_call(
        functools.partial(_route_kernel, cap=cap, set_len=set_len, tchunk=min(set_len, 1024)),
        out_shape=[jax.ShapeDtypeStruct((n_sets, n_exp, cap), jnp.int32),
                   jax.ShapeDtypeStruct((n_sets, n_exp, cap), F32)],
        grid=(n_sets // sets_per_step,),
        in_specs=[pl.BlockSpec((n_exp, width), lambda s: (0, blk0 + s))],
        out_specs=[out_spec, out_spec],
        scratch_shapes=[pltpu.VMEM((n_exp, width), F32), pltpu.VMEM((n_exp, width), jnp.int32)],
        compiler_params=_params("arbitrary"),
        name="route",
    )(logits_t)


GATHER_ROWS = 16


def _gather_kernel(idx_ref, h_ref, o_ref, *, n_groups, slots):
    g = pl.program_id(0)
    e = pl.program_id(1)
    base = (e * n_groups + g) * slots

    def body(j, carry):
        s0 = pl.multiple_of(j * GATHER_ROWS, GATHER_ROWS)
        rows = [h_ref[0, pl.ds(idx_ref[base + s0 + i], 1), :] for i in range(GATHER_ROWS)]
        o_ref[0, 0, pl.ds(s0, GATHER_ROWS), :] = jnp.concatenate(rows, axis=0).astype(o_ref.dtype)
        return carry

    lax.fori_loop(0, slots // GATHER_ROWS, body, 0)


def moe_gather(idx_flat, h, n_exp, n_groups, slots):
    _, tg, d = h.shape
    return pl.pallas_call(
        functools.partial(_gather_kernel, n_groups=n_groups, slots=slots),
        out_shape=jax.ShapeDtypeStruct((n_exp, n_groups, slots, d), BF16),
        grid_spec=pltpu.PrefetchScalarGridSpec(
            num_scalar_prefetch=1,
            grid=(n_groups, n_exp),
            in_specs=[pl.BlockSpec((1, tg, d), lambda g, e, idx: (g, 0, 0))],
            out_specs=pl.BlockSpec((1, 1, slots, d), lambda g, e, idx: (e, g, 0, 0)),
        ),
        compiler_params=_params("arbitrary", "arbitrary"),
        name="moe_gather",
    )(idx_flat, h)


def _ffn_kernel(xs_ref, wg_ref, wu_ref, wd_ref, gt_ref, o_ref, *, tm):
    f = pl.program_id(1)
    wg = wg_ref[0, 0].astype(BF16)
    wu = wu_ref[0, 0].astype(BF16)
    wd = wd_ref[0, 0].astype(BF16)

    @pl.when(f == 0)
    def _():
        o_ref[...] = jnp.zeros_like(o_ref)

    for i in range(xs_ref.shape[1] // tm):
        rows = slice(i * tm, (i + 1) * tm)
        x = xs_ref[0, rows, :]
        hid = _silu(jnp.dot(x, wg, preferred_element_type=F32)) * jnp.dot(
            x, wu, preferred_element_type=F32)
        o_ref[0, rows, :] += jnp.dot(hid.astype(BF16), wd, preferred_element_type=F32)

    @pl.when(f == pl.num_programs(1) - 1)
    def _():
        for i in range(xs_ref.shape[1] // LANES):
            g_rows = jnp.broadcast_to(gt_ref[0, :, i * LANES:(i + 1) * LANES], (LANES, LANES)).T
            for k in range(o_ref.shape[2] // LANES):
                blk = (0, slice(i * LANES, (i + 1) * LANES), slice(k * LANES, (k + 1) * LANES))
                o_ref[blk] = o_ref[blk] * g_rows


def moe_ffn(xs, w_gate, w_up, w_down, layer, gate_row, tf=512, tm=512):
    n_exp, m, d = xs.shape
    f = w_gate.shape[3]
    tf = min(tf, f)
    return pl.pallas_call(
        functools.partial(_ffn_kernel, tm=min(tm, m)),
        out_shape=jax.ShapeDtypeStruct((n_exp, m, d), F32),
        grid=(n_exp, f // tf),
        in_specs=[pl.BlockSpec((1, m, d), lambda e, j: (e, 0, 0)),
                  pl.BlockSpec((1, 1, d, tf), lambda e, j: (layer, e, 0, j)),
                  pl.BlockSpec((1, 1, d, tf), lambda e, j: (layer, e, 0, j)),
                  pl.BlockSpec((1, 1, tf, d), lambda e, j: (layer, e, j, 0)),
                  pl.BlockSpec((1, 1, m), lambda e, j: (e, 0, 0))],
        out_specs=pl.BlockSpec((1, m, d), lambda e, j: (e, 0, 0)),
        compiler_params=_params("arbitrary", "arbitrary"),
        name="moe_ffn",
    )(xs, w_gate, w_up, w_down, gate_row)


def _rows_to_tiles(tile_ref, rows):
    for k in range(tile_ref.shape[1]):
        tile_ref[:, k, :] = rows[:, k * LANES:(k + 1) * LANES]


def _tiles_to_rows(tile_ref):
    return jnp.concatenate([tile_ref[:, k, :] for k in range(tile_ref.shape[1])], axis=1)


COMBINE_UNROLL = 8


def _combine_kernel(idx_ref, ys_ref, o_ref, stage_ref, *, n_groups, slots):
    g = pl.program_id(0)
    e = pl.program_id(1)
    base = (e * n_groups + g) * slots

    @pl.when(e == 0)
    def _():
        o_ref[...] = jnp.zeros_like(o_ref)

    _rows_to_tiles(stage_ref, ys_ref[0, 0])

    def body(j, carry):
        s0 = pl.multiple_of(j * COMBINE_UNROLL, COMBINE_UNROLL)
        toks = [idx_ref[base + s0 + i] for i in range(COMBINE_UNROLL)]
        sums = [o_ref[0, t] + stage_ref[s0 + i] for i, t in enumerate(toks)]
        for t, v in zip(toks, sums):
            o_ref[0, t] = v
        return carry

    lax.fori_loop(0, slots // COMBINE_UNROLL, body, 0)


def moe_combine(idx_flat, ys, tg):
    n_exp, n_groups, slots, d = ys.shape
    c = d // LANES
    return pl.pallas_call(
        functools.partial(_combine_kernel, n_groups=n_groups, slots=slots),
        out_shape=jax.ShapeDtypeStruct((n_groups, tg, c, LANES), F32),
        grid_spec=pltpu.PrefetchScalarGridSpec(
            num_scalar_prefetch=1,
            grid=(n_groups, n_exp),
            in_specs=[pl.BlockSpec((1, 1, slots, d), lambda g, e, idx: (e, g, 0, 0))],
            out_specs=pl.BlockSpec((1, tg, c, LANES), lambda g, e, idx: (g, 0, 0, 0)),
            scratch_shapes=[pltpu.VMEM((slots, c, LANES), F32)],
        ),
        compiler_params=_params("arbitrary", "arbitrary"),
        name="moe_combine",
    )(idx_flat, ys)


def _moe_res_kernel(x_ref, y_ref, g_ref, nw_ref, sh_ref, sc_ref, x2_ref, h_ref):
    x2 = x_ref[...] + g_ref[0] * _tiles_to_rows(y_ref)
    x2_ref[...] = x2
    h_ref[...] = (_rms(x2, nw_ref[...]) * (1.0 + sc_ref[0]) + sh_ref[0]).astype(h_ref.dtype)


def moe_residual(x, y, gate, nw, shift, scale, tg, tm=1024):
    m, d = x.shape
    per = tg // tm
    vec = pl.BlockSpec((1, 1, d), lambda i: (i // per, 0, 0))
    row = pl.BlockSpec((tm, d), lambda i: (i, 0))
    tiles = pl.BlockSpec((tm,) + y.shape[1:], lambda i: (i, 0, 0))
    return pl.pallas_call(
        _moe_res_kernel,
        out_shape=[jax.ShapeDtypeStruct((m, d), F32), jax.ShapeDtypeStruct((m, d), BF16)],
        grid=(m // tm,),
        in_specs=[row, tiles, vec, pl.BlockSpec((1, d), lambda i: (0, 0)), vec, vec],
        out_specs=[row, row],
        compiler_params=_params("arbitrary"),
        name="moe_residual",
    )(x, y, gate, nw.reshape(1, d), shift, scale)


def _final_kernel(x_ref, y_ref, g_ref, nw_ref, o_ref):
    o_ref[...] = _rms(x_ref[...] + g_ref[0] * _tiles_to_rows(y_ref), nw_ref[...])


def final_norm(x, y, gate, nw, tg, row0, rows, tm=1024):
    d = x.shape[1]
    per = tg // tm
    off = row0 // tm
    row = pl.BlockSpec((tm, d), lambda i: (off + i, 0))
    return pl.pallas_call(
        _final_kernel,
        out_shape=jax.ShapeDtypeStruct((rows, d), F32),
        grid=(rows // tm,),
        in_specs=[row, pl.BlockSpec((tm,) + y.shape[1:], lambda i: (off + i, 0, 0)),
                  pl.BlockSpec((1, 1, d), lambda i: ((off + i) // per, 0, 0)),
                  pl.BlockSpec((1, d), lambda i: (0, 0))],
        out_specs=pl.BlockSpec((tm, d), lambda i: (i, 0)),
        compiler_params=_params("arbitrary"),
        name="final_norm",
    )(x, y, gate, nw.reshape(1, d))


CONV_HALO = 16
CONV_SUB = 256


def _shift_rows(ext3, o):
    rot = pltpu.roll(ext3, (SUBLANES - o) % SUBLANES, axis=1)
    r = lax.broadcasted_iota(jnp.int32, (ext3.shape[0] - 2,) + ext3.shape[1:], 1)
    if o > 0:
        return jnp.where(r < SUBLANES - o, rot[1:-1], rot[2:])
    return jnp.where(r >= -o, rot[1:-1], rot[:-2])


def _proj_conv_kernel(xp_ref, x_ref, xn_ref, w_ref, cw_ref, cb_ref, o_ref, wb_ref, *,
                      tm, seq_a, n_a, seq_b):
    i = pl.program_id(1)

    @pl.when(i == 0)
    def _():
        wb_ref[...] = w_ref[0].astype(BF16)

    wb = wb_ref[...]
    y = jnp.dot(x_ref[...], wb, preferred_element_type=F32)
    y_prev = jnp.dot(xp_ref[...], wb, preferred_element_type=F32)[CONV_HALO - SUBLANES:]
    y_next = jnp.dot(xn_ref[...], wb, preferred_element_type=F32)[:SUBLANES]
    row0 = i * tm
    seq = jnp.where(row0 < n_a, seq_a, seq_b)
    off = jnp.where(row0 < n_a, row0, row0 - n_a)
    half = D_CONV // 2
    sub = CONV_SUB
    n_sub = tm // sub
    for s in range(n_sub):
        cur = y[s * sub:(s + 1) * sub]
        prev = y_prev if s == 0 else y[s * sub - SUBLANES:s * sub]
        nxt = y_next if s == n_sub - 1 else y[(s + 1) * sub:(s + 1) * sub + SUBLANES]
        prev = jnp.where(lax.rem(off + s * sub, seq) == 0, 0.0, prev)
        nxt = jnp.where(lax.rem(off + (s + 1) * sub, seq) == 0, 0.0, nxt)
        ext3 = jnp.concatenate([prev, cur, nxt], axis=0).reshape(sub // SUBLANES + 2, SUBLANES, -1)
        acc = cb_ref[...] + cw_ref[half:half + 1, :] * cur
        for o in range(-half, half + 1):
            if o != 0:
                acc = acc + cw_ref[half + o:half + o + 1, :] * _shift_rows(ext3, o).reshape(sub, -1)
        o_ref[s * sub:(s + 1) * sub, :] = _silu(acc).astype(o_ref.dtype)


def proj_conv(x, w, layer, col0, cols, conv_w, conv_b, seq_a, n_a, seq_b, tm=1024, tn=1024):
    m, k = x.shape
    tn = max(t for t in range(LANES, min(tn, cols) + 1, LANES) if cols % t == 0 and col0 % t == 0)
    tm = min(tm, n_a)
    assert n_a % tm == 0 and m % tm == 0 and tm % CONV_SUB == 0
    assert seq_a % CONV_SUB == 0 and seq_b % CONV_SUB == 0
    c_off = col0 // tn
    hb = tm // CONV_HALO
    n_halo = m // CONV_HALO
    return pl.pallas_call(
        functools.partial(_proj_conv_kernel, tm=tm, seq_a=seq_a, n_a=n_a, seq_b=seq_b),
        out_shape=jax.ShapeDtypeStruct((m, cols), BF16),
        grid=(cols // tn, m // tm),
        in_specs=[pl.BlockSpec((CONV_HALO, k), lambda j, i: (jnp.maximum(i * hb - 1, 0), 0)),
                  pl.BlockSpec((tm, k), lambda j, i: (i, 0)),
                  pl.BlockSpec((CONV_HALO, k), lambda j, i: (jnp.minimum((i + 1) * hb, n_halo - 1), 0)),
                  pl.BlockSpec((1, k, tn), lambda j, i: (layer, 0, c_off + j)),
                  pl.BlockSpec((D_CONV, tn), lambda j, i: (0, j)),
                  pl.BlockSpec((1, tn), lambda j, i: (0, j))],
        out_specs=pl.BlockSpec((tm, tn), lambda j, i: (i, j)),
        scratch_shapes=[pltpu.VMEM((k, tn), BF16)],
        compiler_params=_params("arbitrary", "arbitrary"),
        name="proj_conv",
    )(x, x, x, w, conv_w, conv_b.reshape(1, cols))


def _softplus(x):
    return jnp.maximum(x, 0.0) + jnp.log(1.0 + jnp.exp(-jnp.abs(x)))


DT_REPLICAS = 3


def _split3(v, lane, n_heads):
    hi = v.astype(BF16)
    r1 = v - hi.astype(F32)
    mid = r1.astype(BF16)
    lo = (r1 - mid.astype(F32)).astype(BF16)
    zero = jnp.zeros_like(lo)
    return jnp.where(lane < n_heads, hi, jnp.where(
        lane < 2 * n_heads, mid, jnp.where(lane < DT_REPLICAS * n_heads, lo, zero)))


def _scan_kernel(rb_ref, dir_ref, first_ref, last_ref, zero_ref, s0i_ref, soi_ref,
                 xbc_ref, dt_ref, dtb_ref, alog_ref, dsk_ref, e64_ref, e128_ref, s0_ref,
                 y_ref, sout_ref, st_ref, *, n_heads):
    i = pl.program_id(0)
    q = xbc_ref.shape[0]
    p, n, g_n = SSD_HEADDIM, D_STATE, N_GROUPS
    hpg = n_heads // g_n
    d_inner = n_heads * p
    fwd = dir_ref[i] == 0

    @pl.when((first_ref[i] == 1) & (zero_ref[i] == 1))
    def _():
        st_ref[...] = jnp.zeros_like(st_ref)

    @pl.when((first_ref[i] == 1) & (zero_ref[i] == 0))
    def _():
        for g in range(g_n):
            st_ref[g] = s0_ref[0, 0, g].T

    dt = _softplus(dt_ref[...] + dtb_ref[0])
    da = dt * (-LOG2E * jnp.exp(alog_ref[0]))
    li = lax.broadcasted_iota(jnp.int32, (q, q), 0)
    si = lax.broadcasted_iota(jnp.int32, (q, q), 1)
    tri = (li - si) * jnp.where(fwd, 1, -1) >= 0
    lane = lax.broadcasted_iota(jnp.int32, (q, LANES), 1)
    acs = jnp.dot(jnp.where(tri, 1.0, 0.0), da, preferred_element_type=F32, precision=HIGHEST)
    acs_t = acs.T

    f_dt = _split3(dt, lane, n_heads)
    f_acs = _split3(acs, lane, n_heads)
    dt_all = jnp.dot(f_dt, e64_ref[...], preferred_element_type=F32)
    acs_all = jnp.dot(f_acs, e64_ref[...], preferred_element_type=F32)
    acs_end = jnp.where(fwd, acs_all[q - 1:q, :], acs_all[0:1, :])
    e_in = jnp.exp2(acs_all)
    dec = jnp.exp2(acs_end)
    x = xbc_ref[:, :d_inner].astype(F32)
    xdt = x * dt_all
    skip = x * jnp.where(fwd, dsk_ref[...], 0.0)
    xdt_b = xdt.astype(BF16)
    xd_b = (xdt * jnp.exp2(acs_end - acs_all)).astype(BF16)
    first_head = lax.broadcasted_iota(jnp.int32, (q, 2 * p), 1) < p

    for g in range(g_n):
        gcols = slice(g * hpg * p, (g + 1) * hpg * p)
        bm = xbc_ref[:, d_inner + g * n:d_inner + (g + 1) * n]
        cm = xbc_ref[:, d_inner + g_n * n + g * n:d_inner + g_n * n + (g + 1) * n]
        cb = lax.dot_general(cm, bm, NT_DIMS, preferred_element_type=F32)
        st = st_ref[g]
        y_off = jnp.dot(cm, st.astype(BF16), preferred_element_type=F32) * e_in[:, gcols]
        a_g = jnp.dot(f_acs, e128_ref[:, g * hpg * q:(g + 1) * hpg * q],
                      preferred_element_type=F32)
        for e2 in range(hpg // 2):
            h0 = g * hpg + 2 * e2
            pc = slice(h0 * p, (h0 + 2) * p)
            ms = []
            for k in range(2):
                e = 2 * e2 + k
                seg = a_g[:, e * q:(e + 1) * q] - acs_t[h0 + k:h0 + k + 1, :]
                ms.append((jnp.where(tri, jnp.exp2(seg), 0.0) * cb).astype(BF16))
            yd = jnp.dot(jnp.concatenate(ms, axis=0), xdt_b[:, pc], preferred_element_type=F32)
            y = (jnp.where(first_head, yd[:q], yd[q:]) + y_off[:, 2 * e2 * p:(2 * e2 + 2) * p]
                 + skip[:, pc])
            y_ref[0, :, pc] = y.astype(y_ref.dtype)
        bm_t = bm.astype(F32).T.astype(BF16)
        s_new = jnp.dot(bm_t, xd_b[:, gcols], preferred_element_type=F32)
        st_ref[g] = st * dec[:, gcols] + s_new

    @pl.when(last_ref[i] == 1)
    def _():
        for g in range(g_n):
            sout_ref[0, 0, g] = st_ref[g].T


def _expansion(n_heads, width):
    row_head = jnp.arange(LANES) % n_heads
    row_ok = jnp.arange(LANES) < DT_REPLICAS * n_heads
    col_head = jnp.arange(n_heads * width) // width
    return ((row_head[:, None] == col_head[None, :]) & row_ok[:, None]).astype(BF16)


def ssd_scan(xconv, dt2, dtb2, alog2, d_cols, s0_t, n_heads, seqs):
    m, c = xconv.shape
    q = SSD_CHUNK
    hp = n_heads // N_GROUPS * SSD_HEADDIM
    d_inner = n_heads * SSD_HEADDIM
    assert DT_REPLICAS * n_heads <= LANES and 2 * SSD_HEADDIM == LANES == q
    assert (n_heads // N_GROUPS) % 2 == 0
    tabs = [[] for _ in range(7)]
    for row0, length, s0_slot, out_slot in seqs:
        nc = length // q
        for d in range(2):
            for ci in range(nc):
                vals = (row0 // q + (ci if d == 0 else nc - 1 - ci), d, int(ci == 0),
                        int(ci == nc - 1), int(s0_slot is None), s0_slot or 0, out_slot)
                for t, v in zip(tabs, vals):
                    t.append(v)
    n_steps = len(tabs[0])
    n_out = max(t[3] for t in seqs) + 1
    tabs = [jnp.asarray(t, jnp.int32) for t in tabs]

    st_block = (1, 1, N_GROUPS, hp, D_STATE)
    const = lambda i, *_: (0, 0)
    return pl.pallas_call(
        functools.partial(_scan_kernel, n_heads=n_heads),
        out_shape=[jax.ShapeDtypeStruct((2, m, d_inner), BF16),
                   jax.ShapeDtypeStruct((2, n_out, N_GROUPS, hp, D_STATE), F32)],
        grid_spec=pltpu.PrefetchScalarGridSpec(
            num_scalar_prefetch=7,
            grid=(n_steps,),
            in_specs=[pl.BlockSpec((q, c), lambda i, rb, dr, *_: (rb[i], 0)),
                      pl.BlockSpec((q, LANES), lambda i, rb, dr, *_: (rb[i], dr[i])),
                      pl.BlockSpec((1, 1, LANES), lambda i, rb, dr, *_: (dr[i], 0, 0)),
                      pl.BlockSpec((1, 1, LANES), lambda i, rb, dr, *_: (dr[i], 0, 0)),
                      pl.BlockSpec((1, d_inner), const),
                      pl.BlockSpec((LANES, d_inner), const),
                      pl.BlockSpec((LANES, n_heads * q), const),
                      pl.BlockSpec(st_block, lambda i, rb, dr, f, l, z, s0i, soi: (dr[i], s0i[i], 0, 0, 0))],
            out_specs=[pl.BlockSpec((1, q, d_inner), lambda i, rb, dr, *_: (dr[i], rb[i], 0)),
                       pl.BlockSpec(st_block, lambda i, rb, dr, f, l, z, s0i, soi: (dr[i], soi[i], 0, 0, 0))],
            scratch_shapes=[pltpu.VMEM((N_GROUPS, D_STATE, hp), F32)],
        ),
        compiler_params=_params("arbitrary"),
        name="ssd_scan",
    )(*tabs, xconv, dt2, dtb2, alog2, d_cols.reshape(1, d_inner), _expansion(n_heads, SSD_HEADDIM),
      _expansion(n_heads, q), s0_t)


def _replicate_heads(v, n_heads):
    lead = v.shape[:-1]
    v = v.astype(F32).reshape(lead + (2, 1, n_heads))
    v = jnp.broadcast_to(v, lead + (2, DT_REPLICAS, n_heads)).reshape(lead + (2, DT_REPLICAS * n_heads))
    pad = [(0, 0)] * (len(lead) + 1) + [(0, LANES - DT_REPLICAS * n_heads)]
    return jnp.pad(v, pad).reshape(lead + (2 * LANES,))


def _state_groups(s, n_heads):
    return s.reshape(s.shape[0], N_GROUPS, n_heads // N_GROUPS * SSD_HEADDIM, D_STATE)


def _moe(h, logits_t, w_gate, w_up, w_down, layer, dims):
    n_req, seq, n_lat, tg, d = dims
    n_groups = 1 + n_lat
    n_exp = logits_t.shape[0]
    idx_p, gate_p = route(logits_t, seq, 0, n_req, n_req)
    idx_s, gate_s = route(logits_t[:, tg:], tg, 0, n_lat, n_lat)
    slots = idx_s.shape[2]
    idx_p = idx_p + (jnp.arange(n_req, dtype=jnp.int32) * seq)[:, None, None]
    idx = jnp.concatenate([idx_p.transpose(1, 0, 2).reshape(n_exp, 1, slots),
                           idx_s.transpose(1, 0, 2)], axis=1)
    gate = jnp.concatenate([gate_p.transpose(1, 0, 2).reshape(n_exp, 1, slots),
                            gate_s.transpose(1, 0, 2)], axis=1)
    idx_flat = idx.reshape(-1)
    xs = moe_gather(idx_flat, h.reshape(n_groups, tg, d), n_exp, n_groups, slots)
    ys = moe_ffn(xs.reshape(n_exp, n_groups * slots, d), w_gate, w_up, w_down, layer,
                 gate.reshape(n_exp, 1, n_groups * slots))
    out = moe_combine(idx_flat, ys.reshape(n_exp, n_groups, slots, d), tg)
    return out.reshape((n_groups * tg,) + out.shape[2:])


def kernel(x_prompt, x_sample, cache_k_na, cache_v_na, state_ssd_fwd, state_ssd_bwd, c, c_ctx, ada_w, ada_b, norm1_w, norm2_w, final_norm_w, na_qkv_w, na_out_w, na_rpb, ssd_in_w, ssd_conv_w, ssd_conv_b, ssd_dt_bias, ssd_a_log, ssd_d, ssd_norm_w, ssd_out_w, router_w, moe_w_gate, moe_w_up, moe_w_down):
    n_req, seq, d = x_prompt.shape
    n_lat, tg, _ = x_sample.shape
    assert n_req * seq == tg, "token groups must have equal size"
    assert n_lat + 1 <= MOD_ROWS
    depth = ada_w.shape[0]
    n_groups = 1 + n_lat
    m_p = n_req * seq
    dims = (n_req, seq, n_lat, tg, d)

    cvec = jnp.zeros((MOD_ROWS, d), F32).at[0].set(c_ctx).at[1:1 + n_lat].set(c)
    mod = ada_modulation(cvec, ada_w, ada_b)
    mod = mod.reshape(depth, MOD_ROWS, N_MOD, 1, d).transpose(0, 2, 1, 3, 4)

    m_s = n_lat * tg
    x_first, x_second, x_second_row0 = x_prompt.reshape(m_p, d), x_sample.reshape(m_s, d), 0
    h = prenorm(x_first, x_second, norm1_w[0], mod[0, 0], mod[0, 1], tg, BF16)

    new_k, new_v, new_sf, new_sb = [], [], [], []
    y_prompt = y_sample = None
    for layer in range(depth):
        j = layer // 2
        ml = mod[layer]
        if layer % 2 == 0:
            q_p, k_p, v_p = (matmul(h, na_qkv_w, j, F32, rows=m_p, col0=i * d, cols=d)
                             for i in range(3))
            qkv_s = matmul(h, na_qkv_w, j, BF16, row0=m_p)
            new_k.append(k_p.reshape(n_req, seq, N_HEADS, d // N_HEADS))
            new_v.append(v_p.reshape(n_req, seq, N_HEADS, d // N_HEADS))
            att_p = ctx_attention(q_p, k_p, v_p, seq)
            past = cache_k_na.shape[2]
            att_s = na_attention(qkv_s.reshape(n_lat, tg, 3 * d),
                                 cache_k_na[:, j].reshape(n_lat, past, d).astype(BF16),
                                 cache_v_na[:, j].reshape(n_lat, past, d).astype(BF16),
                                 na_bias_table(na_rpb[j], tg // GRID_W))
            x1, hm, logits_t = attn_out(att_p, att_s.reshape(m_s, d), na_out_w, j, x_first, x_second,
                                        x_second_row0, ml[2], norm2_w[layer], ml[3], ml[4],
                                        router_w[layer], tg)
        else:
            n_heads = ssd_d.shape[1]
            d_inner = n_heads * SSD_HEADDIM
            conv_dim = ssd_conv_w.shape[2]
            zs = matmul(h, ssd_in_w, j, BF16, cols=d_inner, silu=True)
            xconv = proj_conv(h, ssd_in_w, j, d_inner, conv_dim, ssd_conv_w[j], ssd_conv_b[j],
                              seq, m_p, tg)
            w_dt = _replicate_heads(ssd_in_w[j, :, d_inner + conv_dim:], n_heads)
            dt2 = matmul(h, w_dt[None], 0, F32)
            dtb2 = _replicate_heads(ssd_dt_bias[j].reshape(-1), n_heads).reshape(2, 1, LANES)
            alog2 = _replicate_heads(ssd_a_log[j].reshape(-1), n_heads).reshape(2, 1, LANES)
            s0 = jnp.stack([_state_groups(state_ssd_fwd[:, j], n_heads),
                            _state_groups(state_ssd_bwd[:, j], n_heads)])
            seqs = [(r * seq, seq, None, r) for r in range(n_req)]
            seqs += [(m_p + b * tg, tg, b, n_req + b) for b in range(n_lat)]
            d_cols = jnp.repeat(ssd_d[j], SSD_HEADDIM)
            y2, st = ssd_scan(xconv, dt2, dtb2, alog2, d_cols, s0, n_heads, seqs)
            new_sf.append(st[0, :n_req].reshape(n_req, n_heads, SSD_HEADDIM, D_STATE))
            new_sb.append(st[1, :n_req].reshape(n_req, n_heads, SSD_HEADDIM, D_STATE))
            x1, hm, logits_t = ssd_out(y2, zs, ssd_norm_w[j], ssd_out_w, j, x_first,
                                       ml[2], norm2_w[layer], ml[3], ml[4], router_w[layer], tg)
        y_moe = _moe(hm, logits_t, moe_w_gate, moe_w_up, moe_w_down, layer, dims)
        if layer + 1 < depth:
            mn = mod[layer + 1]
            x, h = moe_residual(x1, y_moe, ml[5], norm1_w[layer + 1], mn[0], mn[1], tg)
            x_first, x_second, x_second_row0 = x, x, m_p
        else:
            y_prompt = final_norm(x1, y_moe, ml[5], final_norm_w, tg, 0, m_p)
            y_sample = final_norm(x1, y_moe, ml[5], final_norm_w, tg, m_p, m_s)

    return (y_prompt.reshape(n_req, seq, d), y_sample.reshape(n_lat, tg, d),
            jnp.stack(new_k, axis=1), jnp.stack(new_v, axis=1),
            jnp.stack(new_sf, axis=1), jnp.stack(new_sb, axis=1))
```
